```python
import math
import jax, jax.numpy as jnp
from jax import lax
import numpy as np

D_MODEL = 2048
BATCH = 4
SEQ = 8192
DEPTH = 2

CHUNK = 64
Q_BLOCK = 128
HEAD_DIM = 128
N_FOX_HEADS = 8
N_DIFF_HEADS = 4
FOX_WIDTH = N_FOX_HEADS * HEAD_DIM
DIFF_QK_WIDTH = N_DIFF_HEADS * 2 * HEAD_DIM
DIFF_V_WIDTH = N_DIFF_HEADS * 2 * HEAD_DIM
BRANCH_WIDTH = FOX_WIDTH
N_BRANCHES = 2
IN_WIDTH = 3 * FOX_WIDTH + N_FOX_HEADS + 2 * DIFF_QK_WIDTH + DIFF_V_WIDTH + N_BRANCHES * D_MODEL
ROT_DIM = HEAD_DIM // 4
ROPE_THETA = 500000.0
N_EXPERTS = 64
N_GROUPS = 8
TOPK_GROUPS = 4
TOP_K = 8
EXPERT_DIM = D_MODEL // 4
SHARED_DIM = D_MODEL // 4
ROUTED_SCALE = 2.5
EXPERT_BLOCK = 256
RMS_EPS = 1e-6

kernel_name = "hybrid_fox_diffattn_moe_adaln"


def rms_norm(x, g):
    xf = x.astype(jnp.float32)
    y = xf * lax.rsqrt(jnp.mean(xf * xf, axis=-1, keepdims=True) + RMS_EPS)
    return (y * g.astype(jnp.float32)).astype(x.dtype)


def modulate(h, shift, scale):
    return h * (1 + scale[:, None, :]) + shift[:, None, :]


def rope_tables(positions):
    inv_freq = ROPE_THETA ** (-jnp.arange(0, ROT_DIM, 2, dtype=jnp.float32) / ROT_DIM)
    ang = positions.astype(jnp.float32)[..., None] * inv_freq
    return jnp.cos(ang)[:, None, None], jnp.sin(ang)[:, None, None]


def partial_rope(x, cos, sin):
    half = ROT_DIM // 2
    xr = x[..., :ROT_DIM].astype(jnp.float32)
    x1, x2 = xr[..., :half], xr[..., half:]
    rot = jnp.concatenate([x1 * cos - x2 * sin, x2 * cos + x1 * sin], axis=-1).astype(x.dtype)
    return jnp.concatenate([rot, x[..., ROT_DIM:]], axis=-1)


def forgetting_attention(q, k, v, cum):
    B, H, S, d = q.shape
    nq = S // Q_BLOCK
    qb = q.reshape(B, H, nq, Q_BLOCK, d).transpose(2, 0, 1, 3, 4)
    cb = cum.reshape(B, H, nq, Q_BLOCK).transpose(2, 0, 1, 3)
    k_pos = jnp.arange(S)
    scale = d ** -0.5

    def step(args):
        qi, ci, i = args
        s = jnp.einsum('bhqd,bhkd->bhqk', qi, k).astype(jnp.float32) * scale
        s = s + ci[..., None] - cum[:, :, None, :]
        q_pos = i * Q_BLOCK + jnp.arange(Q_BLOCK)
        mask = k_pos[None, :] <= q_pos[:, None]
        p = jax.nn.softmax(jnp.where(mask, s, -jnp.inf), axis=-1)
        return jnp.einsum('bhqk,bhkd->bhqd', p.astype(v.dtype), v)

    out = lax.map(step, (qb, cb, jnp.arange(nq)))
    return out.transpose(1, 2, 0, 3, 4).reshape(B, H, S, d)


def differential_attention(q, k, v, lam):
    B, H, _, S, d = q.shape
    nq = S // Q_BLOCK
    qb = q.reshape(B, H, 2, nq, Q_BLOCK, d).transpose(3, 0, 1, 2, 4, 5)
    k_chunk = jnp.arange(S) // CHUNK
    scale = d ** -0.5

    def step(args):
        qi, i = args
        s = jnp.einsum('bhjqd,bhjkd->bhjqk', qi, k).astype(jnp.float32) * scale
        q_chunk = (i * Q_BLOCK + jnp.arange(Q_BLOCK)) // CHUNK
        mask = k_chunk[None, :] <= q_chunk[:, None]
        p = jax.nn.softmax(jnp.where(mask, s, -jnp.inf), axis=-1)
        a = p[:, :, 0] - lam * p[:, :, 1]
        return jnp.einsum('bhqk,bhke->bhqe', a.astype(v.dtype), v)

    out = lax.map(step, (qb, jnp.arange(nq)))
    return out.transpose(1, 2, 0, 3, 4).reshape(B, H, S, 2 * d)


def token_mixer(h, w_in, b_fgt, fox_qn_g, fox_kn_g, diff_qn_g, diff_kn_g,
                lam_q1, lam_k1, lam_q2, lam_k2, diff_out_g, w_branch, w_out,
                cos, sin, lam_init):
    B, S, _ = h.shape
    proj = h @ w_in
    widths = [FOX_WIDTH, FOX_WIDTH, FOX_WIDTH, N_FOX_HEADS,
              DIFF_QK_WIDTH, DIFF_QK_WIDTH, DIFF_V_WIDTH, D_MODEL]
    idx = [int(i) for i in np.cumsum(widths)]
    fq, fk, fv, ff, dq, dk, dv, ga, gb = jnp.split(proj, idx, axis=-1)

    def heads(t, n, d):
        return t.reshape(B, S, n, d).transpose(0, 2, 1, 3)
    q = rms_norm(heads(fq, N_FOX_HEADS, HEAD_DIM), fox_qn_g)
    k = rms_norm(heads(fk, N_FOX_HEADS, HEAD_DIM), fox_kn_g)
    v = heads(fv, N_FOX_HEADS, HEAD_DIM)
    log_f = jax.nn.log_sigmoid(ff.astype(jnp.float32) + b_fgt.astype(jnp.float32))
    cum = jnp.cumsum(log_f, axis=1).transpose(0, 2, 1)
    o_fox = forgetting_attention(q, k, v, cum)
    o_fox = o_fox.transpose(0, 2, 1, 3).reshape(B, S, BRANCH_WIDTH)

    qd = dq.reshape(B, S, N_DIFF_HEADS, 2, HEAD_DIM).transpose(0, 2, 3, 1, 4)
    kd = dk.reshape(B, S, N_DIFF_HEADS, 2, HEAD_DIM).transpose(0, 2, 3, 1, 4)
    qd = partial_rope(rms_norm(qd, diff_qn_g), cos, sin)
    kd = partial_rope(rms_norm(kd, diff_kn_g), cos, sin)
    vd = dv.reshape(B, S, N_DIFF_HEADS, 2 * HEAD_DIM).transpose(0, 2, 1, 3)
    f32 = jnp.float32
    lam = (jnp.exp(jnp.sum(lam_q1.astype(f32) * lam_k1.astype(f32)))
           - jnp.exp(jnp.sum(lam_q2.astype(f32) * lam_k2.astype(f32))) + lam_init)
    o_d = differential_attention(qd, kd, vd, lam)
    o_d = rms_norm(o_d, diff_out_g) * (1 - lam_init)
    o_d = o_d.transpose(0, 2, 1, 3).reshape(B, S, BRANCH_WIDTH)

    y = jnp.einsum('nbsw,nwd->nbsd', jnp.stack([o_fox, o_d]), w_branch)
    merged = jax.nn.sigmoid(ga) * y[0] + jax.nn.sigmoid(gb) * y[1]
    return merged @ w_out


def swiglu(x, wg, wu, wd):
    return (jax.nn.silu(x @ wg) * (x @ wu)) @ wd


def routed_experts(xt, top_e, top_w, wg, wu, wd):
    T, D = xt.shape
    N = T * TOP_K
    flat_e = top_e.reshape(N).astype(jnp.int32)
    flat_tok = (jnp.arange(N, dtype=jnp.int32) // TOP_K)
    flat_w = top_w.reshape(N).astype(jnp.float32)
    order = jnp.argsort(flat_e)
    sorted_e = flat_e[order]
    counts = jnp.bincount(flat_e, length=N_EXPERTS)
    starts = jnp.cumsum(counts) - counts
    padded = ((counts + EXPERT_BLOCK - 1) // EXPERT_BLOCK) * EXPERT_BLOCK
    pends = jnp.cumsum(padded)
    pstarts = pends - padded
    dest = pstarts[sorted_e] + (jnp.arange(N) - starts[sorted_e])
    n_blocks = -(-N // EXPERT_BLOCK) + N_EXPERTS
    P = n_blocks * EXPERT_BLOCK
    slot_tok = jnp.full((P,), T, jnp.int32).at[dest].set(flat_tok[order])
    slot_w = jnp.zeros((P,), jnp.float32).at[dest].set(flat_w[order])
    block_e = jnp.searchsorted(pends, jnp.arange(n_blocks) * EXPERT_BLOCK, side='right')
    block_e = jnp.minimum(block_e, N_EXPERTS - 1)
    x_pad = jnp.concatenate([xt, jnp.zeros((1, D), xt.dtype)], axis=0)

    def body(acc, blk):
        tok, wt, e = blk
        xb = x_pad[tok]
        yb = swiglu(xb, wg[e], wu[e], wd[e]).astype(jnp.float32) * wt[:, None]
        return acc.at[tok].add(yb), None

    acc0 = jnp.zeros((T + 1, D), jnp.float32)
    acc, _ = lax.scan(body, acc0, (slot_tok.reshape(n_blocks, EXPERT_BLOCK),
                                   slot_w.reshape(n_blocks, EXPERT_BLOCK), block_e))
    return acc[:T].astype(xt.dtype)


def moe_ffn(h, w_router, b_router, wg, wu, wd, wsg, wsu, wsd):
    B, S, D = h.shape
    T = B * S
    xt = h.reshape(T, D)
    scores = jax.nn.sigmoid((xt @ w_router).astype(jnp.float32))
    sel = scores + b_router.astype(jnp.float32)
    grp = sel.reshape(T, N_GROUPS, N_EXPERTS // N_GROUPS)
    grp_score = lax.top_k(grp, 2)[0].sum(-1)
    _, top_grp = lax.top_k(grp_score, TOPK_GROUPS)
    grp_mask = jnp.any(top_grp[..., None] == jnp.arange(N_GROUPS), axis=1)
    expert_mask = jnp.repeat(grp_mask, N_EXPERTS // N_GROUPS, axis=1)
    _, top_e = lax.top_k(jnp.where(expert_mask, sel, -jnp.inf), TOP_K)
    w = jnp.take_along_axis(scores, top_e, axis=1)
    w = w / jnp.sum(w, axis=-1, keepdims=True) * ROUTED_SCALE
    routed = routed_experts(xt, top_e, w, wg, wu, wd)
    shared = swiglu(xt, wsg, wsu, wsd)
    return (shared + routed).reshape(B, S, D)


def setup_inputs(seed: int = 0) -> dict:
    key = jax.random.key(seed)
    ks = iter(jax.random.split(key, 32))
    L, D = DEPTH, D_MODEL
    nrm = lambda shape, s: jax.random.normal(next(ks), shape, jnp.float32) * s
    gain = lambda shape: 1.0 + nrm(shape, 0.02)
    x = nrm((BATCH, SEQ, D), 1.0)
    c = nrm((BATCH, D), 1.0)
    offsets = jax.random.randint(next(ks), (BATCH,), 0, 64) * CHUNK
    positions = (offsets[:, None] + jnp.arange(SEQ)[None, :]).astype(jnp.int32)
    fcol = 3 * FOX_WIDTH
    w_in = nrm((L, D, IN_WIDTH), D ** -0.5)
    w_in = w_in.at[:, :, fcol:fcol + N_FOX_HEADS].multiply(0.1)
    return {
        "x": x, "c": c, "positions": positions,
        "norm1_g": gain((L, D)), "norm2_g": gain((L, D)),
        "w_ada": nrm((L, D, 6 * D), 0.5 * D ** -0.5), "b_ada": nrm((L, 6 * D), 0.01),
        "w_in": w_in, "b_fgt": 3.0 + nrm((L, N_FOX_HEADS), 0.3),
        "fox_qn_g": gain((L, HEAD_DIM)), "fox_kn_g": gain((L, HEAD_DIM)),
        "diff_qn_g": gain((L, HEAD_DIM)), "diff_kn_g": gain((L, HEAD_DIM)),
        "lam_q1": nrm((L, HEAD_DIM), 0.1), "lam_k1": nrm((L, HEAD_DIM), 0.1),
        "lam_q2": nrm((L, HEAD_DIM), 0.1), "lam_k2": nrm((L, HEAD_DIM), 0.1),
        "diff_out_g": gain((L, 2 * HEAD_DIM)),
        "w_branch": nrm((L, N_BRANCHES, BRANCH_WIDTH, D), BRANCH_WIDTH ** -0.5),
        "w_out": nrm((L, D, D), D ** -0.5),
        "w_router": nrm((L, D, N_EXPERTS), D ** -0.5), "b_router": nrm((L, N_EXPERTS), 0.01),
        "w_exp_gate": nrm((L, N_EXPERTS, D, EXPERT_DIM), D ** -0.5),
        "w_exp_up": nrm((L, N_EXPERTS, D, EXPERT_DIM), D ** -0.5),
        "w_exp_down": nrm((L, N_EXPERTS, EXPERT_DIM, D), EXPERT_DIM ** -0.5),
        "w_sh_gate": nrm((L, D, SHARED_DIM), D ** -0.5),
        "w_sh_up": nrm((L, D, SHARED_DIM), D ** -0.5),
        "w_sh_down": nrm((L, SHARED_DIM, D), SHARED_DIM ** -0.5),
    }


def reference(x, c, positions, norm1_g, norm2_g, w_ada, b_ada, w_in, b_fgt,
              fox_qn_g, fox_kn_g, diff_qn_g, diff_kn_g, lam_q1, lam_k1, lam_q2, lam_k2,
              diff_out_g, w_branch, w_out, w_router, b_router,
              w_exp_gate, w_exp_up, w_exp_down, w_sh_gate, w_sh_up, w_sh_down):
    cos, sin = rope_tables(positions)
    cond = jax.nn.silu(c)
    for l in range(DEPTH):
        mod = cond @ w_ada[l] + b_ada[l]
        sh1, sc1, g1, sh2, sc2, g2 = jnp.split(mod, 6, axis=-1)
        lam_init = 0.8 - 0.6 * math.exp(-0.3 * l)
        h = modulate(rms_norm(x, norm1_g[l]), sh1, sc1)
        mix = token_mixer(h, w_in[l], b_fgt[l], fox_qn_g[l], fox_kn_g[l], diff_qn_g[l], diff_kn_g[l],
                          lam_q1[l], lam_k1[l], lam_q2[l], lam_k2[l], diff_out_g[l],
                          w_branch[l], w_out[l], cos, sin, lam_init)
        x = x + g1[:, None, :] * mix
        h = modulate(rms_norm(x, norm2_g[l]), sh2, sc2)
        ffn = moe_ffn(h, w_router[l], b_router[l], w_exp_gate[l], w_exp_up[l], w_exp_down[l],
                      w_sh_gate[l], w_sh_up[l], w_sh_down[l])
        x = x + g2[:, None, :] * ffn
    return x
```

```python
import functools
import math

import jax
import jax.numpy as jnp
import numpy as np
from jax import lax
from jax.experimental import pallas as pl
from jax.experimental.pallas import tpu as pltpu

F32 = jnp.float32
BF16 = jnp.bfloat16

HEAD_DIM = 128
N_FOX_HEADS = 8
N_DIFF_HEADS = 4
ATTN_WIDTH = N_FOX_HEADS * HEAD_DIM
CHUNK = 64
ROT_DIM = HEAD_DIM // 4
ROPE_THETA = 500000.0
N_EXPERTS = 64
N_GROUPS = 8
GROUP_SIZE = N_EXPERTS // N_GROUPS
TOPK_GROUPS = 4
TOP_K = 8
ROUTED_SCALE = 2.5
RMS_EPS = 1e-6
LANES = 128
V7X_VMEM_BYTES = 64 * 1024 * 1024
VMEM_LIMIT = V7X_VMEM_BYTES - 8 * 1024 * 1024

SEC_FQ, SEC_FK, SEC_FV, SEC_DQ, SEC_DK, SEC_DV, SEC_GATE = range(7)


def _cparams(sem):
    return pltpu.CompilerParams(dimension_semantics=sem, vmem_limit_bytes=VMEM_LIMIT)


def _rms(y, axis=-1):
    return y * lax.rsqrt(jnp.mean(y * y, axis=axis, keepdims=True) + RMS_EPS)


def _sigmoid(z):
    return 1.0 / (1.0 + jnp.exp(-z))


def _adaln_kernel(c_ref, w_ref, b_ref, o_ref):
    c = c_ref[...]
    cond = (c * _sigmoid(c)).astype(BF16)
    o_ref[0] = jnp.dot(cond, w_ref[0].astype(BF16), preferred_element_type=F32) + b_ref[0]


def adaln(c, w_ada, b_ada, *, tn=1024):
    L, D, N = w_ada.shape
    B = c.shape[0]
    tn = min(tn, N)
    return pl.pallas_call(
        _adaln_kernel,
        out_shape=jax.ShapeDtypeStruct((L, B, N), F32),
        grid=(L, N // tn),
        in_specs=[
            pl.BlockSpec((B, D), lambda l, j: (0, 0)),
            pl.BlockSpec((1, D, tn), lambda l, j: (l, 0, j)),
            pl.BlockSpec((1, 1, tn), lambda l, j: (l, 0, j)),
        ],
        out_specs=pl.BlockSpec((1, B, tn), lambda l, j: (l, 0, j)),
        compiler_params=_cparams(("arbitrary", "arbitrary")),
        name="adaln",
    )(c, w_ada, b_ada.reshape(L, 1, N))


def _rope_kernel(pos_ref, invf_ref, cos_ref, sin_ref):
    ang = pos_ref[...].astype(F32) * invf_ref[...]
    lane = lax.broadcasted_iota(jnp.int32, ang.shape, 1)
    c = jnp.cos(ang)
    s = jnp.sin(ang)
    cos_ref[...] = jnp.where(lane < ROT_DIM, c, 1.0)
    sin_ref[...] = jnp.where(lane < ROT_DIM // 2, -s, jnp.where(lane < ROT_DIM, s, 0.0))


def rope_tables(positions, *, tm=1024):
    T = positions.size
    tm = min(tm, T)
    half = ROT_DIM // 2
    inv_freq = ROPE_THETA ** (-jnp.arange(0, ROT_DIM, 2, dtype=F32) / ROT_DIM)
    invf = jnp.zeros((1, LANES), F32).at[0, :half].set(inv_freq).at[0, half:ROT_DIM].set(inv_freq)
    return pl.pallas_call(
        _rope_kernel,
        out_shape=(jax.ShapeDtypeStruct((T, LANES), F32),) * 2,
        grid=(T // tm,),
        in_specs=[
            pl.BlockSpec((tm, 1), lambda i: (i, 0)),
            pl.BlockSpec((1, LANES), lambda i: (0, 0)),
        ],
        out_specs=(pl.BlockSpec((tm, LANES), lambda i: (i, 0)),) * 2,
        compiler_params=_cparams(("arbitrary",)),
        name="rope_tables",
    )(positions.reshape(T, 1), invf)


def _inproj_kernel(x_ref, g_ref, sh_ref, sc_ref, w_ref, wff_ref, cg_ref, cos_ref, sin_ref,
                   o_ref, ff_ref, h_scr):
    j = pl.program_id(1)

    @pl.when(j == 0)
    def _():
        y = _rms(x_ref[...]) * g_ref[...]
        h = (y * (1.0 + sc_ref[0]) + sh_ref[0]).astype(BF16)
        h_scr[...] = h
        ff_ref[...] = jnp.dot(h, wff_ref[...], preferred_element_type=F32)

    acc = jnp.dot(h_scr[...], w_ref[...], preferred_element_type=F32)
    n_heads = ATTN_WIDTH // HEAD_DIM

    def head(a, hd):
        return a[:, hd * HEAD_DIM:(hd + 1) * HEAD_DIM]

    is_norm = (j == SEC_FQ) | (j == SEC_FK)
    is_rope = (j == SEC_DQ) | (j == SEC_DK)
    is_plain = (j == SEC_FV) | (j == SEC_DV)

    @pl.when(is_norm)
    def _():
        for hd in range(n_heads):
            o_ref[:, hd * HEAD_DIM:(hd + 1) * HEAD_DIM] = (
                _rms(head(acc, hd)) * head(cg_ref[...], hd)).astype(o_ref.dtype)

    @pl.when(is_rope)
    def _():
        cos_t = cos_ref[...]
        sin_t = sin_ref[...]
        lane = lax.broadcasted_iota(jnp.int32, cos_t.shape, 1)
        low = lane < ROT_DIM // 2
        for hd in range(n_heads):
            y = _rms(head(acc, hd)) * head(cg_ref[...], hd)
            rot = jnp.where(low, pltpu.roll(y, HEAD_DIM - ROT_DIM // 2, 1),
                            pltpu.roll(y, ROT_DIM // 2, 1))
            o_ref[:, hd * HEAD_DIM:(hd + 1) * HEAD_DIM] = (y * cos_t + rot * sin_t).astype(o_ref.dtype)

    @pl.when(is_plain)
    def _():
        o_ref[...] = acc.astype(o_ref.dtype)

    @pl.when(j >= SEC_GATE)
    def _():
        o_ref[...] = _sigmoid(acc).astype(o_ref.dtype)


def inproj(x2, g, shift, scale, w_cat, w_ff, colgain, cos_t, sin_t, *, seq, tm=1024):
    T, D = x2.shape
    NC = w_cat.shape[1]
    tn = ATTN_WIDTH
    tm = min(tm, seq)
    bpb = seq // tm
    return pl.pallas_call(
        _inproj_kernel,
        out_shape=(jax.ShapeDtypeStruct((T, NC), BF16), jax.ShapeDtypeStruct((T, LANES), F32)),
        grid=(T // tm, NC // tn),
        in_specs=[
            pl.BlockSpec((tm, D), lambda i, j: (i, 0)),
            pl.BlockSpec((1, D), lambda i, j: (0, 0)),
            pl.BlockSpec((1, 1, D), lambda i, j: (i // bpb, 0, 0)),
            pl.BlockSpec((1, 1, D), lambda i, j: (i // bpb, 0, 0)),
            pl.BlockSpec((D, tn), lambda i, j: (0, j)),
            pl.BlockSpec((D, LANES), lambda i, j: (0, 0)),
            pl.BlockSpec((1, tn), lambda i, j: (0, j)),
            pl.BlockSpec((tm, LANES), lambda i, j: (i, 0)),
            pl.BlockSpec((tm, LANES), lambda i, j: (i, 0)),
        ],
        out_specs=(
            pl.BlockSpec((tm, tn), lambda i, j: (i, j)),
            pl.BlockSpec((tm, LANES), lambda i, j: (i, 0)),
        ),
        scratch_shapes=[pltpu.VMEM((tm, D), BF16)],
        compiler_params=_cparams(("arbitrary", "arbitrary")),
        name="inproj",
    )(x2, g, shift, scale, w_cat, w_ff, colgain, cos_t, sin_t)


def _split3(v):
    hi = v.astype(BF16)
    r = v - hi.astype(F32)
    mid = r.astype(BF16)
    lo = (r - mid.astype(F32)).astype(BF16)
    return hi, mid, lo


def _fgate_kernel(ff_ref, b_ref, o_ref, carry_scr):
    @pl.when(pl.program_id(1) == 0)
    def _():
        carry_scr[...] = jnp.zeros_like(carry_scr)

    z = ff_ref[...] + b_ref[...]
    logf = jnp.minimum(z, 0.0) - jnp.log(1.0 + jnp.exp(-jnp.abs(z)))
    tm = z.shape[0]
    row = lax.broadcasted_iota(jnp.int32, (tm, tm), 0)
    col = lax.broadcasted_iota(jnp.int32, (tm, tm), 1)
    tri = jnp.where(row >= col, 1.0, 0.0).astype(BF16)
    cum = carry_scr[...]
    for part in _split3(logf):
        cum = cum + jnp.dot(tri, part, preferred_element_type=F32)
    carry_scr[...] = cum[tm - 1:tm, :]
    o_ref[0] = (-cum).T[:N_FOX_HEADS, :]


def fgate_bias(ff, b_fgt, *, batch, seq, tm=512):
    tm = min(tm, seq)
    b_pad = jnp.zeros((1, LANES), F32).at[0, :N_FOX_HEADS].set(b_fgt.astype(F32))
    nb = seq // tm
    return pl.pallas_call(
        _fgate_kernel,
        out_shape=jax.ShapeDtypeStruct((batch, N_FOX_HEADS, seq), F32),
        grid=(batch, nb),
        in_specs=[
            pl.BlockSpec((tm, LANES), lambda b, i: (b * nb + i, 0)),
            pl.BlockSpec((1, LANES), lambda b, i: (0, 0)),
        ],
        out_specs=pl.BlockSpec((1, N_FOX_HEADS, tm), lambda b, i: (b, 0, i)),
        scratch_shapes=[pltpu.VMEM((1, LANES), F32)],
        compiler_params=_cparams(("arbitrary", "arbitrary")),
        name="fgate",
    )(ff, b_pad)


def _softmax_step(s, v, m_ref, l_ref, acc_ref):
    m_old = m_ref[...]
    m_new = jnp.maximum(m_old, jnp.max(s, axis=-1, keepdims=True))
    alpha = jnp.exp(m_old - m_new)
    p = jnp.exp(s - m_new)
    l_ref[...] = alpha * l_ref[...] + jnp.sum(p, axis=-1, keepdims=True)
    acc_ref[...] = alpha * acc_ref[...] + jnp.dot(p.astype(v.dtype), v, preferred_element_type=F32)
    m_ref[...] = m_new


_NT = (((1,), (1,)), ((), ()))


def _fox_kernel(q_ref, k_ref, v_ref, bias_ref, o_ref, m_scr, l_scr, acc_scr, *, tq):
    i = pl.program_id(2)
    q = q_ref[0]
    m_scr[...] = jnp.full_like(m_scr, -jnp.inf)
    l_scr[...] = jnp.zeros_like(l_scr)
    acc_scr[...] = jnp.zeros_like(acc_scr)

    def block(j, masked):
        ks = pl.multiple_of(j * tq, tq)
        k = k_ref[0, pl.ds(ks, tq), :]
        v = v_ref[0, pl.ds(ks, tq), :]
        s = lax.dot_general(q, k, _NT, preferred_element_type=F32) + bias_ref[0, 0, :, pl.ds(ks, tq)]
        if masked:
            row = lax.broadcasted_iota(jnp.int32, s.shape, 0)
            col = lax.broadcasted_iota(jnp.int32, s.shape, 1)
            s = jnp.where(col <= row, s, -jnp.inf)
        _softmax_step(s, v, m_scr, l_scr, acc_scr)

    def body(j, carry):
        block(j, False)
        return carry

    lax.fori_loop(0, i, body, 0)
    block(i, True)
    o_ref[0] = (acc_scr[...] / l_scr[...]).astype(o_ref.dtype)


def fox_attention(proj3, bias, *, tq=512):
    B, S, _ = proj3.shape
    tq = min(tq, S)
    H = N_FOX_HEADS
    return pl.pallas_call(
        functools.partial(_fox_kernel, tq=tq),
        out_shape=jax.ShapeDtypeStruct((B, S, ATTN_WIDTH), BF16),
        grid=(B, H, S // tq),
        in_specs=[
            pl.BlockSpec((1, tq, HEAD_DIM), lambda b, h, i: (b, i, SEC_FQ * H + h)),
            pl.BlockSpec((1, S, HEAD_DIM), lambda b, h, i: (b, 0, SEC_FK * H + h)),
            pl.BlockSpec((1, S, HEAD_DIM), lambda b, h, i: (b, 0, SEC_FV * H + h)),
            pl.BlockSpec((1, 1, 1, S), lambda b, h, i: (b, h, 0, 0)),
        ],
        out_specs=pl.BlockSpec((1, tq, HEAD_DIM), lambda b, h, i: (b, i, h)),
        scratch_shapes=[pltpu.VMEM((tq, 1), F32), pltpu.VMEM((tq, 1), F32),
                        pltpu.VMEM((tq, HEAD_DIM), F32)],
        compiler_params=_cparams(("arbitrary", "arbitrary", "arbitrary")),
        name="fox_attn",
    )(proj3, proj3, proj3, bias)


def _diff_kernel(q_ref, k_ref, v_ref, lq1_ref, lk1_ref, lq2_ref, lk2_ref, g_ref, o_ref,
                 m1, l1, a1, m2, l2, a2, *, tq, lam_init):
    i = pl.program_id(2)
    for m, l, a in ((m1, l1, a1), (m2, l2, a2)):
        m[...] = jnp.full_like(m, -jnp.inf)
        l[...] = jnp.zeros_like(l)
        a[...] = jnp.zeros_like(a)

    def block(j, masked):
        ks = pl.multiple_of(j * tq, tq)
        v = v_ref[0, pl.ds(ks, tq), :]
        for mp, (m, l, a) in enumerate(((m1, l1, a1), (m2, l2, a2))):
            q = q_ref[0, :, mp * HEAD_DIM:(mp + 1) * HEAD_DIM]
            k = k_ref[0, pl.ds(ks, tq), mp * HEAD_DIM:(mp + 1) * HEAD_DIM]
            s = lax.dot_general(q, k, _NT, preferred_element_type=F32)
            if masked:
                row = lax.broadcasted_iota(jnp.int32, s.shape, 0)
                col = lax.broadcasted_iota(jnp.int32, s.shape, 1)
                s = jnp.where(col // CHUNK <= row // CHUNK, s, -jnp.inf)
            _softmax_step(s, v, m, l, a)

    def body(j, carry):
        block(j, False)
        return carry

    lax.fori_loop(0, i, body, 0)
    block(i, True)
    lam = (jnp.exp(jnp.sum(lq1_ref[...] * lk1_ref[...], axis=-1, keepdims=True))
           - jnp.exp(jnp.sum(lq2_ref[...] * lk2_ref[...], axis=-1, keepdims=True)) + lam_init)
    o = a1[...] / l1[...] - lam * (a2[...] / l2[...])
    o_ref[0] = (_rms(o) * g_ref[...] * (1.0 - lam_init)).astype(o_ref.dtype)


def diff_attention(proj3, lam_vecs, out_gain, *, lam_init, tq=512):
    B, S, _ = proj3.shape
    tq = min(tq, S)
    H = N_DIFF_HEADS
    dv = 2 * HEAD_DIM
    vec = pl.BlockSpec((1, HEAD_DIM), lambda b, h, i: (0, 0))
    return pl.pallas_call(
        functools.partial(_diff_kernel, tq=tq, lam_init=lam_init),
        out_shape=jax.ShapeDtypeStruct((B, S, ATTN_WIDTH), BF16),
        grid=(B, H, S // tq),
        in_specs=[
            pl.BlockSpec((1, tq, dv), lambda b, h, i: (b, i, SEC_DQ * H + h)),
            pl.BlockSpec((1, S, dv), lambda b, h, i: (b, 0, SEC_DK * H + h)),
            pl.BlockSpec((1, S, dv), lambda b, h, i: (b, 0, SEC_DV * H + h)),
            vec, vec, vec, vec,
            pl.BlockSpec((1, dv), lambda b, h, i: (0, 0)),
        ],
        out_specs=pl.BlockSpec((1, tq, dv), lambda b, h, i: (b, i, h)),
        scratch_shapes=[pltpu.VMEM((tq, 1), F32), pltpu.VMEM((tq, 1), F32), pltpu.VMEM((tq, dv), F32)] * 2,
        compiler_params=_cparams(("arbitrary", "arbitrary", "arbitrary")),
        name="diff_attn",
    )(proj3, proj3, proj3, *lam_vecs, out_gain)


def _post_kernel(of_ref, od_ref, ga_ref, gb_ref, wb0_ref, wb1_ref, wo_ref, x_ref, g1_ref, o_ref):
    y0 = jnp.dot(of_ref[...], wb0_ref[...], preferred_element_type=F32)
    y1 = jnp.dot(od_ref[...], wb1_ref[...], preferred_element_type=F32)
    merged = ga_ref[...].astype(F32) * y0 + gb_ref[...].astype(F32) * y1
    mix = jnp.dot(merged.astype(BF16), wo_ref[...], preferred_element_type=F32)
    o_ref[...] = x_ref[...] + g1_ref[0] * mix


def _resident(shape):
    nd = len(shape)
    return pl.BlockSpec(shape, lambda *_: (0,) * nd, pipeline_mode=pl.Buffered(1))


def post_attention(o_fox, o_diff, proj, wb0, wb1, wo, x2, g1, *, seq, tm=256):
    T, D = x2.shape
    tm = min(tm, seq)
    bpb = seq // tm
    gate_blk = SEC_GATE * ATTN_WIDTH // D
    return pl.pallas_call(
        _post_kernel,
        out_shape=jax.ShapeDtypeStruct((T, D), F32),
        grid=(T // tm,),
        in_specs=[
            pl.BlockSpec((tm, ATTN_WIDTH), lambda i: (i, 0)),
            pl.BlockSpec((tm, ATTN_WIDTH), lambda i: (i, 0)),
            pl.BlockSpec((tm, D), lambda i: (i, gate_blk)),
            pl.BlockSpec((tm, D), lambda i: (i, gate_blk + 1)),
            _resident(wb0.shape), _resident(wb1.shape), _resident(wo.shape),
            pl.BlockSpec((tm, D), lambda i: (i, 0)),
            pl.BlockSpec((1, 1, D), lambda i: (i // bpb, 0, 0)),
        ],
        out_specs=pl.BlockSpec((tm, D), lambda i: (i, 0)),
        compiler_params=_cparams(("arbitrary",)),
        name="post_attn",
    )(o_fox, o_diff, proj, proj, wb0, wb1, wo, x2, g1)


def _router_kernel(x_ref, g_ref, sh_ref, sc_ref, wr_ref, br_ref, h_ref, e_ref, w_ref):
    y = _rms(x_ref[...]) * g_ref[...]
    h = y * (1.0 + sc_ref[0]) + sh_ref[0]
    hb = h.astype(BF16)
    h_ref[...] = hb
    logits = lax.dot_general(wr_ref[...], hb, _NT, preferred_element_type=F32)
    scores = _sigmoid(logits)
    tm = scores.shape[1]
    sc3 = scores.reshape(N_GROUPS, GROUP_SIZE, tm)
    sel = sc3 + br_ref[...]
    neg = -jnp.inf
    member = lax.broadcasted_iota(jnp.int32, sel.shape, 1)
    group = lax.broadcasted_iota(jnp.int32, sel.shape, 0)

    m1 = jnp.max(sel, axis=1, keepdims=True)
    i1 = jnp.min(jnp.where(sel == m1, member, GROUP_SIZE), axis=1, keepdims=True)
    m2 = jnp.max(jnp.where(member == i1, neg, sel), axis=1, keepdims=True)
    gscore = m1 + m2

    gidx = lax.broadcasted_iota(jnp.int32, gscore.shape, 0)
    gkeep = jnp.zeros(gscore.shape, jnp.int32)
    for _ in range(TOPK_GROUPS):
        best = jnp.max(gscore, axis=0, keepdims=True)
        bi = jnp.min(jnp.where(gscore == best, gidx, N_GROUPS), axis=0, keepdims=True)
        hit = gidx == bi
        gkeep = jnp.where(hit, 1, gkeep)
        gscore = jnp.where(hit, neg, gscore)

    cand = jnp.where(gkeep > 0, sel, neg)
    eidx = group * GROUP_SIZE + member
    weights = []
    for k in range(TOP_K):
        best = jnp.max(jnp.max(cand, axis=1, keepdims=True), axis=0, keepdims=True)
        bi = jnp.min(jnp.min(jnp.where(cand == best, eidx, N_EXPERTS), axis=1, keepdims=True),
                     axis=0, keepdims=True)
        hit = eidx == bi
        e_ref[k:k + 1, :] = bi[0]
        weights.append(jnp.sum(jnp.sum(jnp.where(hit, sc3, 0.0), axis=1, keepdims=True),
                               axis=0, keepdims=True)[0])
        cand = jnp.where(hit, neg, cand)
    wsum = functools.reduce(lambda a, b: a + b, weights)
    for k in range(TOP_K):
        w_ref[k:k + 1, :] = weights[k] / wsum * ROUTED_SCALE


def router(x2, g, shift, scale, wr_t, b_router, *, seq, tm=512):
    T, D = x2.shape
    tm = min(tm, seq)
    bpb = seq // tm
    return pl.pallas_call(
        _router_kernel,
        out_shape=(jax.ShapeDtypeStruct((T, D), BF16),
                   jax.ShapeDtypeStruct((TOP_K, T), jnp.int32),
                   jax.ShapeDtypeStruct((TOP_K, T), F32)),
        grid=(T // tm,),
        in_specs=[
            pl.BlockSpec((tm, D), lambda i: (i, 0)),
            pl.BlockSpec((1, D), lambda i: (0, 0)),
            pl.BlockSpec((1, 1, D), lambda i: (i // bpb, 0, 0)),
            pl.BlockSpec((1, 1, D), lambda i: (i // bpb, 0, 0)),
            pl.BlockSpec((N_EXPERTS, D), lambda i: (0, 0)),
            pl.BlockSpec((N_GROUPS, GROUP_SIZE, 1), lambda i: (0, 0, 0)),
        ],
        out_specs=(pl.BlockSpec((tm, D), lambda i: (i, 0)),
                   pl.BlockSpec((TOP_K, tm), lambda i: (0, i)),
                   pl.BlockSpec((TOP_K, tm), lambda i: (0, i))),
        compiler_params=_cparams(("arbitrary",)),
        name="router",
    )(x2, g, shift, scale, wr_t, b_router.reshape(N_GROUPS, GROUP_SIZE, 1).astype(F32))


def _expert_kernel(be_ref, nb_ref, x_ref, sw_ref, wg_ref, wu_ref, wd_ref, o_ref):
    @pl.when(pl.program_id(0) < nb_ref[0])
    def _():
        x = x_ref[...]
        a = jnp.dot(x, wg_ref[0], preferred_element_type=F32)
        u = jnp.dot(x, wu_ref[0], preferred_element_type=F32)
        hmid = (a * _sigmoid(a) * u).astype(BF16)
        y = jnp.dot(hmid, wd_ref[0], preferred_element_type=F32)
        o_ref[...] = (y * sw_ref[...]).astype(o_ref.dtype)

    @pl.when(pl.program_id(0) >= nb_ref[0])
    def _():
        o_ref[...] = jnp.zeros_like(o_ref)


def experts(x_sorted, slot_w, block_e, n_used, wg, wu, wd, *, blk):
    P, D = x_sorted.shape
    E = wg.shape[2]
    nblk = P // blk
    grid_spec = pltpu.PrefetchScalarGridSpec(
        num_scalar_prefetch=2,
        grid=(nblk,),
        in_specs=[
            pl.BlockSpec((blk, D), lambda b, be, nb: (b, 0)),
            pl.BlockSpec((blk, 1), lambda b, be, nb: (b, 0)),
            pl.BlockSpec((1, D, E), lambda b, be, nb: (be[b], 0, 0)),
            pl.BlockSpec((1, D, E), lambda b, be, nb: (be[b], 0, 0)),
            pl.BlockSpec((1, E, D), lambda b, be, nb: (be[b], 0, 0)),
        ],
        out_specs=pl.BlockSpec((blk, D), lambda b, be, nb: (b, 0)),
    )
    return pl.pallas_call(
        _expert_kernel,
        out_shape=jax.ShapeDtypeStruct((P, D), BF16),
        grid_spec=grid_spec,
        compiler_params=_cparams(("arbitrary",)),
        name="experts",
    )(block_e, n_used, x_sorted, slot_w, wg, wu, wd)


def _ffn_out_kernel(h_ref, r_ref, wg_ref, wu_ref, wd_ref, x_ref, g2_ref, o_ref):
    h = h_ref[...]
    a = jnp.dot(h, wg_ref[...], preferred_element_type=F32)
    u = jnp.dot(h, wu_ref[...], preferred_element_type=F32)
    hmid = (a * _sigmoid(a) * u).astype(BF16)
    shared = jnp.dot(hmid, wd_ref[...], preferred_element_type=F32)
    o_ref[...] = x_ref[...] + g2_ref[0] * (shared + r_ref[...])


def ffn_out(h, routed, wsg, wsu, wsd, x2, g2, *, seq, tm=512):
    T, D = x2.shape
    tm = min(tm, seq)
    bpb = seq // tm
    return pl.pallas_call(
        _ffn_out_kernel,
        out_shape=jax.ShapeDtypeStruct((T, D), F32),
        grid=(T // tm,),
        in_specs=[
            pl.BlockSpec((tm, D), lambda i: (i, 0)),
            pl.BlockSpec((tm, D), lambda i: (i, 0)),
            _resident(wsg.shape), _resident(wsu.shape), _resident(wsd.shape),
            pl.BlockSpec((tm, D), lambda i: (i, 0)),
            pl.BlockSpec((1, 1, D), lambda i: (i // bpb, 0, 0)),
        ],
        out_specs=pl.BlockSpec((tm, D), lambda i: (i, 0)),
        compiler_params=_cparams(("arbitrary",)),
        name="ffn_out",
    )(h, routed, wsg, wsu, wsd, x2, g2)


def dispatch_plan(top_e, top_w, *, blk):
    K, T = top_e.shape
    N = T * K
    flat_e = top_e.T.reshape(N)
    flat_w = top_w.T.reshape(N)
    flat_tok = jnp.arange(N, dtype=jnp.int32) // K
    order = jnp.argsort(flat_e)
    sorted_e = flat_e[order]
    counts = jnp.bincount(flat_e, length=N_EXPERTS)
    starts = jnp.cumsum(counts) - counts
    padded = ((counts + blk - 1) // blk) * blk
    pends = jnp.cumsum(padded)
    pstarts = pends - padded
    dest = (pstarts[sorted_e] + (jnp.arange(N) - starts[sorted_e])).astype(jnp.int32)
    nblk = -(-N // blk) + N_EXPERTS
    P = nblk * blk
    slot_tok = jnp.full((P,), T, jnp.int32).at[dest].set(flat_tok[order])
    slot_w = jnp.zeros((P,), F32).at[dest].set(flat_w[order])
    block_e = jnp.searchsorted(pends, jnp.arange(nblk) * blk, side='right')
    block_e = jnp.minimum(block_e, N_EXPERTS - 1).astype(jnp.int32)
    n_used = (pends[-1] // blk).astype(jnp.int32).reshape(1)
    return slot_tok, slot_w, block_e, n_used


def _layer_weights(l, D, norm1_g, norm2_g, w_in, b_fgt, fox_qn_g, fox_kn_g, diff_qn_g, diff_kn_g,
                   diff_out_g, w_branch, w_out, w_router, w_exp_gate, w_exp_up, w_exp_down,
                   w_sh_gate, w_sh_up, w_sh_down):
    W = ATTN_WIDTH
    fcol = 3 * W
    w = w_in[l]
    w_cat = jnp.concatenate([w[:, :fcol], w[:, fcol + N_FOX_HEADS:]], axis=1).astype(BF16)
    w_ff = jnp.zeros((D, LANES), BF16).at[:, :N_FOX_HEADS].set(
        w[:, fcol:fcol + N_FOX_HEADS].astype(BF16))
    qscale = HEAD_DIM ** -0.5
    ones = jnp.ones((W,), F32)
    colgain = jnp.concatenate([
        jnp.tile(fox_qn_g[l].astype(F32) * qscale, N_FOX_HEADS),
        jnp.tile(fox_kn_g[l].astype(F32), N_FOX_HEADS),
        ones,
        jnp.tile(diff_qn_g[l].astype(F32) * qscale, 2 * N_DIFF_HEADS),
        jnp.tile(diff_kn_g[l].astype(F32), 2 * N_DIFF_HEADS),
        ones,
        jnp.ones((2 * D,), F32),
    ]).reshape(1, -1)
    return dict(
        w_cat=w_cat, w_ff=w_ff, colgain=colgain,
        wb0=w_branch[l, 0].astype(BF16), wb1=w_branch[l, 1].astype(BF16), wo=w_out[l].astype(BF16),
        wr_t=w_router[l].T.astype(BF16),
        wg=w_exp_gate[l].astype(BF16), wu=w_exp_up[l].astype(BF16), wd=w_exp_down[l].astype(BF16),
        wsg=w_sh_gate[l].astype(BF16), wsu=w_sh_up[l].astype(BF16), wsd=w_sh_down[l].astype(BF16),
    )


def kernel(x, c, positions, norm1_g, norm2_g, w_ada, b_ada, w_in, b_fgt, fox_qn_g, fox_kn_g,
           diff_qn_g, diff_kn_g, lam_q1, lam_k1, lam_q2, lam_k2, diff_out_g, w_branch, w_out,
           w_router, b_router, w_exp_gate, w_exp_up, w_exp_down, w_sh_gate, w_sh_up, w_sh_down):
    B, S, D = x.shape
    L = w_ada.shape[0]
    T = B * S
    blk = 256
    cos_t, sin_t = rope_tables(positions)
    mod = adaln(c, w_ada, b_ada)
    x2 = x.reshape(T, D)
    for l in range(L):
        lw = _layer_weights(l, D, norm1_g, norm2_g, w_in, b_fgt, fox_qn_g, fox_kn_g, diff_qn_g,
                            diff_kn_g, diff_out_g, w_branch, w_out, w_router, w_exp_gate,
                            w_exp_up, w_exp_down, w_sh_gate, w_sh_up, w_sh_down)
        sh1, sc1, g1, sh2, sc2, g2 = [mod[l, :, k * D:(k + 1) * D].reshape(B, 1, D) for k in range(6)]
        lam_init = 0.8 - 0.6 * math.exp(-0.3 * l)

        proj, ff = inproj(x2, norm1_g[l].reshape(1, D), sh1, sc1, lw['w_cat'], lw['w_ff'],
                          lw['colgain'], cos_t, sin_t, seq=S)
        bias = fgate_bias(ff, b_fgt[l], batch=B, seq=S).reshape(B, N_FOX_HEADS, 1, S)
        proj3 = proj.reshape(B, S, -1)
        o_fox = fox_attention(proj3, bias).reshape(T, ATTN_WIDTH)
        lam_vecs = [v[l].reshape(1, HEAD_DIM).astype(F32) for v in (lam_q1, lam_k1, lam_q2, lam_k2)]
        o_diff = diff_attention(proj3, lam_vecs, diff_out_g[l].reshape(1, -1).astype(F32),
                                lam_init=lam_init).reshape(T, ATTN_WIDTH)
        x2 = post_attention(o_fox, o_diff, proj, lw['wb0'], lw['wb1'], lw['wo'], x2, g1, seq=S)

        h, top_e, top_w = router(x2, norm2_g[l].reshape(1, D), sh2, sc2, lw['wr_t'], b_router[l], seq=S)
        slot_tok, slot_w, block_e, n_used = dispatch_plan(top_e, top_w, blk=blk)
        h_pad = jnp.concatenate([h, jnp.zeros((1, D), h.dtype)], axis=0)
        y_sorted = experts(h_pad[slot_tok], slot_w.reshape(-1, 1), block_e, n_used,
                           lw['wg'], lw['wu'], lw['wd'], blk=blk)
        routed = jnp.zeros((T + 1, D), F32).at[slot_tok].add(y_sorted.astype(F32))[:T]
        x2 = ffn_out(h, routed, lw['wsg'], lw['wsu'], lw['wsd'], x2, g2, seq=S)
    return x2.reshape(B, S, D)
```

```python
import functools
import math

import jax
import jax.numpy as jnp
import numpy as np
from jax import lax
from jax.experimental import pallas as pl
from jax.experimental.pallas import tpu as pltpu

F32 = jnp.float32
BF16 = jnp.bfloat16

HEAD_DIM = 128
N_FOX_HEADS = 8
N_DIFF_HEADS = 4
ATTN_WIDTH = N_FOX_HEADS * HEAD_DIM
CHUNK = 64
ROT_DIM = HEAD_DIM // 4
ROPE_THETA = 500000.0
N_EXPERTS = 64
N_GROUPS = 8
GROUP_SIZE = N_EXPERTS // N_GROUPS
TOPK_GROUPS = 4
TOP_K = 8
ROUTED_SCALE = 2.5
RMS_EPS = 1e-6
LANES = 128
V7X_VMEM_BYTES = 64 * 1024 * 1024
VMEM_LIMIT = V7X_VMEM_BYTES - 8 * 1024 * 1024

SEC_FQ, SEC_FK, SEC_FV, SEC_DQ, SEC_DK, SEC_DV, SEC_GATE = range(7)


def _cparams(sem):
    return pltpu.CompilerParams(dimension_semantics=sem, vmem_limit_bytes=VMEM_LIMIT)


def _rms(y, axis=-1):
    return y * lax.rsqrt(jnp.mean(y * y, axis=axis, keepdims=True) + RMS_EPS)


def _sigmoid(z):
    return 1.0 / (1.0 + jnp.exp(-z))


def _adaln_kernel(c_ref, w_ref, b_ref, o_ref):
    c = c_ref[...]
    cond = (c * _sigmoid(c)).astype(BF16)
    o_ref[0] = jnp.dot(cond, w_ref[0].astype(BF16), preferred_element_type=F32) + b_ref[0]


def adaln(c, w_ada, b_ada, *, tn=1024):
    L, D, N = w_ada.shape
    B = c.shape[0]
    tn = min(tn, N)
    return pl.pallas_call(
        _adaln_kernel,
        out_shape=jax.ShapeDtypeStruct((L, B, N), F32),
        grid=(L, N // tn),
        in_specs=[
            pl.BlockSpec((B, D), lambda l, j: (0, 0)),
            pl.BlockSpec((1, D, tn), lambda l, j: (l, 0, j)),
            pl.BlockSpec((1, 1, tn), lambda l, j: (l, 0, j)),
        ],
        out_specs=pl.BlockSpec((1, B, tn), lambda l, j: (l, 0, j)),
        compiler_params=_cparams(("arbitrary", "arbitrary")),
        name="adaln",
    )(c, w_ada, b_ada.reshape(L, 1, N))


def _rope_kernel(pos_ref, invf_ref, cos_ref, sin_ref):
    ang = pos_ref[...].astype(F32) * invf_ref[...]
    lane = lax.broadcasted_iota(jnp.int32, ang.shape, 1)
    c = jnp.cos(ang)
    s = jnp.sin(ang)
    cos_ref[...] = jnp.where(lane < ROT_DIM, c, 1.0)
    sin_ref[...] = jnp.where(lane < ROT_DIM // 2, -s, jnp.where(lane < ROT_DIM, s, 0.0))


def rope_tables(positions, *, tm=1024):
    T = positions.size
    tm = min(tm, T)
    half = ROT_DIM // 2
    inv_freq = ROPE_THETA ** (-jnp.arange(0, ROT_DIM, 2, dtype=F32) / ROT_DIM)
    invf = jnp.zeros((1, LANES), F32).at[0, :half].set(inv_freq).at[0, half:ROT_DIM].set(inv_freq)
    return pl.pallas_call(
        _rope_kernel,
        out_shape=(jax.ShapeDtypeStruct((T, LANES), F32),) * 2,
        grid=(T // tm,),
        in_specs=[
            pl.BlockSpec((tm, 1), lambda i: (i, 0)),
            pl.BlockSpec((1, LANES), lambda i: (0, 0)),
        ],
        out_specs=(pl.BlockSpec((tm, LANES), lambda i: (i, 0)),) * 2,
        compiler_params=_cparams(("arbitrary",)),
        name="rope_tables",
    )(positions.reshape(T, 1), invf)


def _inproj_kernel(x_ref, g_ref, sh_ref, sc_ref, w_ref, wff_ref, cg_ref, cos_ref, sin_ref,
                   o_ref, ff_ref, h_scr):
    j = pl.program_id(1)

    @pl.when(j == 0)
    def _():
        y = _rms(x_ref[...]) * g_ref[...]
        h = (y * (1.0 + sc_ref[0]) + sh_ref[0]).astype(BF16)
        h_scr[...] = h
        ff_ref[...] = jnp.dot(h, wff_ref[...], preferred_element_type=F32)

    acc = jnp.dot(h_scr[...], w_ref[...], preferred_element_type=F32)
    n_heads = ATTN_WIDTH // HEAD_DIM

    def head(a, hd):
        return a[:, hd * HEAD_DIM:(hd + 1) * HEAD_DIM]

    is_norm = (j == SEC_FQ) | (j == SEC_FK)
    is_rope = (j == SEC_DQ) | (j == SEC_DK)
    is_plain = (j == SEC_FV) | (j == SEC_DV)

    @pl.when(is_norm)
    def _():
        for hd in range(n_heads):
            o_ref[:, hd * HEAD_DIM:(hd + 1) * HEAD_DIM] = (
                _rms(head(acc, hd)) * head(cg_ref[...], hd)).astype(o_ref.dtype)

    @pl.when(is_rope)
    def _():
        cos_t = cos_ref[...]
        sin_t = sin_ref[...]
        lane = lax.broadcasted_iota(jnp.int32, cos_t.shape, 1)
        low = lane < ROT_DIM // 2
        for hd in range(n_heads):
            y = _rms(head(acc, hd)) * head(cg_ref[...], hd)
            rot = jnp.where(low, pltpu.roll(y, HEAD_DIM - ROT_DIM // 2, 1),
                            pltpu.roll(y, ROT_DIM // 2, 1))
            o_ref[:, hd * HEAD_DIM:(hd + 1) * HEAD_DIM] = (y * cos_t + rot * sin_t).astype(o_ref.dtype)

    @pl.when(is_plain)
    def _():
        o_ref[...] = acc.astype(o_ref.dtype)

    @pl.when(j >= SEC_GATE)
    def _():
        o_ref[...] = _sigmoid(acc).astype(o_ref.dtype)


def inproj(x2, g, shift, scale, w_cat, w_ff, colgain, cos_t, sin_t, *, seq, tm=1024):
    T, D = x2.shape
    NC = w_cat.shape[1]
    tn = ATTN_WIDTH
    tm = min(tm, seq)
    bpb = seq // tm
    return pl.pallas_call(
        _inproj_kernel,
        out_shape=(jax.ShapeDtypeStruct((T, NC), BF16), jax.ShapeDtypeStruct((T, LANES), F32)),
        grid=(T // tm, NC // tn),
        in_specs=[
            pl.BlockSpec((tm, D), lambda i, j: (i, 0)),
            pl.BlockSpec((1, D), lambda i, j: (0, 0)),
            pl.BlockSpec((1, 1, D), lambda i, j: (i // bpb, 0, 0)),
            pl.BlockSpec((1, 1, D), lambda i, j: (i // bpb, 0, 0)),
            pl.BlockSpec((D, tn), lambda i, j: (0, j)),
            pl.BlockSpec((D, LANES), lambda i, j: (0, 0)),
            pl.BlockSpec((1, tn), lambda i, j: (0, j)),
            pl.BlockSpec((tm, LANES), lambda i, j: (i, 0)),
            pl.BlockSpec((tm, LANES), lambda i, j: (i, 0)),
        ],
        out_specs=(
            pl.BlockSpec((tm, tn), lambda i, j: (i, j)),
            pl.BlockSpec((tm, LANES), lambda i, j: (i, 0)),
        ),
        scratch_shapes=[pltpu.VMEM((tm, D), BF16)],
        compiler_params=_cparams(("arbitrary", "arbitrary")),
        name="inproj",
    )(x2, g, shift, scale, w_cat, w_ff, colgain, cos_t, sin_t)


def _split3(v):
    hi = v.astype(BF16)
    r = v - hi.astype(F32)
    mid = r.astype(BF16)
    lo = (r - mid.astype(F32)).astype(BF16)
    return hi, mid, lo


def _fgate_kernel(ff_ref, b_ref, o_ref, carry_scr):
    @pl.when(pl.program_id(1) == 0)
    def _():
        carry_scr[...] = jnp.zeros_like(carry_scr)

    z = ff_ref[...] + b_ref[...]
    logf = jnp.minimum(z, 0.0) - jnp.log(1.0 + jnp.exp(-jnp.abs(z)))
    tm = z.shape[0]
    row = lax.broadcasted_iota(jnp.int32, (tm, tm), 0)
    col = lax.broadcasted_iota(jnp.int32, (tm, tm), 1)
    tri = jnp.where(row >= col, 1.0, 0.0).astype(BF16)
    cum = carry_scr[...]
    for part in _split3(logf):
        cum = cum + jnp.dot(tri, part, preferred_element_type=F32)
    carry_scr[...] = cum[tm - 1:tm, :]
    o_ref[0] = (-cum).T[:N_FOX_HEADS, :]


def fgate_bias(ff, b_fgt, *, batch, seq, tm=512):
    tm = min(tm, seq)
    b_pad = jnp.zeros((1, LANES), F32).at[0, :N_FOX_HEADS].set(b_fgt.astype(F32))
    nb = seq // tm
    return pl.pallas_call(
        _fgate_kernel,
        out_shape=jax.ShapeDtypeStruct((batch, N_FOX_HEADS, seq), F32),
        grid=(batch, nb),
        in_specs=[
            pl.BlockSpec((tm, LANES), lambda b, i: (b * nb + i, 0)),
            pl.BlockSpec((1, LANES), lambda b, i: (0, 0)),
        ],
        out_specs=pl.BlockSpec((1, N_FOX_HEADS, tm), lambda b, i: (b, 0, i)),
        scratch_shapes=[pltpu.VMEM((1, LANES), F32)],
        compiler_params=_cparams(("arbitrary", "arbitrary")),
        name="fgate",
    )(ff, b_pad)


def _softmax_step(s, v, m_ref, l_ref, acc_ref):
    m_old = m_ref[...]
    m_new = jnp.maximum(m_old, jnp.max(s, axis=-1, keepdims=True))
    alpha = jnp.exp(m_old - m_new)
    p = jnp.exp(s - m_new)
    l_ref[...] = alpha * l_ref[...] + jnp.sum(p, axis=-1, keepdims=True)
    acc_ref[...] = alpha * acc_ref[...] + jnp.dot(p.astype(v.dtype), v, preferred_element_type=F32)
    m_ref[...] = m_new


_NT = (((1,), (1,)), ((), ()))


def _fox_kernel(q_ref, k_ref, v_ref, bias_ref, o_ref, m_scr, l_scr, acc_scr, *, tq):
    i = pl.program_id(2)
    q = q_ref[0]
    m_scr[...] = jnp.full_like(m_scr, -jnp.inf)
    l_scr[...] = jnp.zeros_like(l_scr)
    acc_scr[...] = jnp.zeros_like(acc_scr)

    def block(j, masked):
        ks = pl.multiple_of(j * tq, tq)
        k = k_ref[0, pl.ds(ks, tq), :]
        v = v_ref[0, pl.ds(ks, tq), :]
        s = lax.dot_general(q, k, _NT, preferred_element_type=F32) + bias_ref[0, 0, :, pl.ds(ks, tq)]
        if masked:
            row = lax.broadcasted_iota(jnp.int32, s.shape, 0)
            col = lax.broadcasted_iota(jnp.int32, s.shape, 1)
            s = jnp.where(col <= row, s, -jnp.inf)
        _softmax_step(s, v, m_scr, l_scr, acc_scr)

    def body(j, carry):
        block(j, False)
        return carry

    lax.fori_loop(0, i, body, 0)
    block(i, True)
    o_ref[0] = (acc_scr[...] / l_scr[...]).astype(o_ref.dtype)


def fox_attention(proj3, bias, *, tq=512):
    B, S, _ = proj3.shape
    tq = min(tq, S)
    H = N_FOX_HEADS
    return pl.pallas_call(
        functools.partial(_fox_kernel, tq=tq),
        out_shape=jax.ShapeDtypeStruct((B, S, ATTN_WIDTH), BF16),
        grid=(B, H, S // tq),
        in_specs=[
            pl.BlockSpec((1, tq, HEAD_DIM), lambda b, h, i: (b, i, SEC_FQ * H + h)),
            pl.BlockSpec((1, S, HEAD_DIM), lambda b, h, i: (b, 0, SEC_FK * H + h)),
            pl.BlockSpec((1, S, HEAD_DIM), lambda b, h, i: (b, 0, SEC_FV * H + h)),
            pl.BlockSpec((1, 1, 1, S), lambda b, h, i: (b, h, 0, 0)),
        ],
        out_specs=pl.BlockSpec((1, tq, HEAD_DIM), lambda b, h, i: (b, i, h)),
        scratch_shapes=[pltpu.VMEM((tq, 1), F32), pltpu.VMEM((tq, 1), F32),
                        pltpu.VMEM((tq, HEAD_DIM), F32)],
        compiler_params=_cparams(("arbitrary", "arbitrary", "arbitrary")),
        name="fox_attn",
    )(proj3, proj3, proj3, bias)


def _diff_kernel(q_ref, k_ref, v_ref, lq1_ref, lk1_ref, lq2_ref, lk2_ref, g_ref, o_ref,
                 m1, l1, a1, m2, l2, a2, *, tq, lam_init):
    i = pl.program_id(2)
    for m, l, a in ((m1, l1, a1), (m2, l2, a2)):
        m[...] = jnp.full_like(m, -jnp.inf)
        l[...] = jnp.zeros_like(l)
        a[...] = jnp.zeros_like(a)

    def block(j, masked):
        ks = pl.multiple_of(j * tq, tq)
        v = v_ref[0, pl.ds(ks, tq), :]
        for mp, (m, l, a) in enumerate(((m1, l1, a1), (m2, l2, a2))):
            q = q_ref[0, :, mp * HEAD_DIM:(mp + 1) * HEAD_DIM]
            k = k_ref[0, pl.ds(ks, tq), mp * HEAD_DIM:(mp + 1) * HEAD_DIM]
            s = lax.dot_general(q, k, _NT, preferred_element_type=F32)
            if masked:
                row = lax.broadcasted_iota(jnp.int32, s.shape, 0)
                col = lax.broadcasted_iota(jnp.int32, s.shape, 1)
                s = jnp.where(col // CHUNK <= row // CHUNK, s, -jnp.inf)
            _softmax_step(s, v, m, l, a)

    def body(j, carry):
        block(j, False)
        return carry

    lax.fori_loop(0, i, body, 0)
    block(i, True)
    lam = (jnp.exp(jnp.sum(lq1_ref[...] * lk1_ref[...], axis=-1, keepdims=True))
           - jnp.exp(jnp.sum(lq2_ref[...] * lk2_ref[...], axis=-1, keepdims=True)) + lam_init)
    o = a1[...] / l1[...] - lam * (a2[...] / l2[...])
    o_ref[0] = (_rms(o) * g_ref[...] * (1.0 - lam_init)).astype(o_ref.dtype)


def diff_attention(proj3, lam_vecs, out_gain, *, lam_init, tq=512):
    B, S, _ = proj3.shape
    tq = min(tq, S)
    H = N_DIFF_HEADS
    dv = 2 * HEAD_DIM
    vec = pl.BlockSpec((1, HEAD_DIM), lambda b, h, i: (0, 0))
    return pl.pallas_call(
        functools.partial(_diff_kernel, tq=tq, lam_init=lam_init),
        out_shape=jax.ShapeDtypeStruct((B, S, ATTN_WIDTH), BF16),
        grid=(B, H, S // tq),
        in_specs=[
            pl.BlockSpec((1, tq, dv), lambda b, h, i: (b, i, SEC_DQ * H + h)),
            pl.BlockSpec((1, S, dv), lambda b, h, i: (b, 0, SEC_DK * H + h)),
            pl.BlockSpec((1, S, dv), lambda b, h, i: (b, 0, SEC_DV * H + h)),
            vec, vec, vec, vec,
            pl.BlockSpec((1, dv), lambda b, h, i: (0, 0)),
        ],
        out_specs=pl.BlockSpec((1, tq, dv), lambda b, h, i: (b, i, h)),
        scratch_shapes=[pltpu.VMEM((tq, 1), F32), pltpu.VMEM((tq, 1), F32), pltpu.VMEM((tq, dv), F32)] * 2,
        compiler_params=_cparams(("arbitrary", "arbitrary", "arbitrary")),
        name="diff_attn",
    )(proj3, proj3, proj3, *lam_vecs, out_gain)


def _post_kernel(of_ref, od_ref, ga_ref, gb_ref, wb0_ref, wb1_ref, wo_ref, x_ref, g1_ref, o_ref):
    y0 = jnp.dot(of_ref[...], wb0_ref[...], preferred_element_type=F32)
    y1 = jnp.dot(od_ref[...], wb1_ref[...], preferred_element_type=F32)
    merged = ga_ref[...].astype(F32) * y0 + gb_ref[...].astype(F32) * y1
    mix = jnp.dot(merged.astype(BF16), wo_ref[...], preferred_element_type=F32)
    o_ref[...] = x_ref[...] + g1_ref[0] * mix


def _resident(shape):
    nd = len(shape)
    return pl.BlockSpec(shape, lambda *_: (0,) * nd, pipeline_mode=pl.Buffered(1))


def post_attention(o_fox, o_diff, proj, wb0, wb1, wo, x2, g1, *, seq, tm=256):
    T, D = x2.shape
    tm = min(tm, seq)
    bpb = seq // tm
    gate_blk = SEC_GATE * ATTN_WIDTH // D
    return pl.pallas_call(
        _post_kernel,
        out_shape=jax.ShapeDtypeStruct((T, D), F32),
        grid=(T // tm,),
        in_specs=[
            pl.BlockSpec((tm, ATTN_WIDTH), lambda i: (i, 0)),
            pl.BlockSpec((tm, ATTN_WIDTH), lambda i: (i, 0)),
            pl.BlockSpec((tm, D), lambda i: (i, gate_blk)),
            pl.BlockSpec((tm, D), lambda i: (i, gate_blk + 1)),
            _resident(wb0.shape), _resident(wb1.shape), _resident(wo.shape),
            pl.BlockSpec((tm, D), lambda i: (i, 0)),
            pl.BlockSpec((1, 1, D), lambda i: (i // bpb, 0, 0)),
        ],
        out_specs=pl.BlockSpec((tm, D), lambda i: (i, 0)),
        compiler_params=_cparams(("arbitrary",)),
        name="post_attn",
    )(o_fox, o_diff, proj, proj, wb0, wb1, wo, x2, g1)


def _pack_halves(y):
    half = y.shape[1] // 2
    bits = lax.bitcast_convert_type(y.astype(BF16).astype(F32), jnp.uint32)
    return (bits[:, half:] & jnp.uint32(0xFFFF0000)) | (bits[:, :half] >> 16)


def _unpack_halves(w):
    lo = lax.bitcast_convert_type(w << 16, F32)
    hi = lax.bitcast_convert_type(w & jnp.uint32(0xFFFF0000), F32)
    return lo, hi


def _store_token_major(ref, words, s8):
    m = words.shape[0]
    for s in range(s8):
        ref[pl.ds(s, m, stride=s8), :] = words[:, s * LANES:(s + 1) * LANES]


def _load_token_major(ref, m, s8):
    parts = [_unpack_halves(ref[pl.ds(s, m, stride=s8), :]) for s in range(s8)]
    return jnp.concatenate([p[0].astype(BF16) for p in parts] + [p[1].astype(BF16) for p in parts],
                           axis=1)


def _router_kernel(x_ref, g_ref, sh_ref, sc_ref, wr_ref, br_ref, h_ref, e_ref, w_ref, *, s8):
    y = _rms(x_ref[...]) * g_ref[...]
    h = y * (1.0 + sc_ref[0]) + sh_ref[0]
    hb = h.astype(BF16)
    _store_token_major(h_ref, _pack_halves(h), s8)
    logits = lax.dot_general(wr_ref[...], hb, _NT, preferred_element_type=F32)
    scores = _sigmoid(logits)
    tm = scores.shape[1]
    sc3 = scores.reshape(N_GROUPS, GROUP_SIZE, tm)
    sel = sc3 + br_ref[...]
    neg = -jnp.inf
    member = lax.broadcasted_iota(jnp.int32, sel.shape, 1)
    group = lax.broadcasted_iota(jnp.int32, sel.shape, 0)

    m1 = jnp.max(sel, axis=1, keepdims=True)
    i1 = jnp.min(jnp.where(sel == m1, member, GROUP_SIZE), axis=1, keepdims=True)
    m2 = jnp.max(jnp.where(member == i1, neg, sel), axis=1, keepdims=True)
    gscore = m1 + m2

    gidx = lax.broadcasted_iota(jnp.int32, gscore.shape, 0)
    gkeep = jnp.zeros(gscore.shape, jnp.int32)
    for _ in range(TOPK_GROUPS):
        best = jnp.max(gscore, axis=0, keepdims=True)
        bi = jnp.min(jnp.where(gscore == best, gidx, N_GROUPS), axis=0, keepdims=True)
        hit = gidx == bi
        gkeep = jnp.where(hit, 1, gkeep)
        gscore = jnp.where(hit, neg, gscore)

    cand = jnp.where(gkeep > 0, sel, neg)
    eidx = group * GROUP_SIZE + member
    weights = []
    for k in range(TOP_K):
        best = jnp.max(jnp.max(cand, axis=1, keepdims=True), axis=0, keepdims=True)
        bi = jnp.min(jnp.min(jnp.where(cand == best, eidx, N_EXPERTS), axis=1, keepdims=True),
                     axis=0, keepdims=True)
        hit = eidx == bi
        e_ref[k:k + 1, :] = bi[0]
        weights.append(jnp.sum(jnp.sum(jnp.where(hit, sc3, 0.0), axis=1, keepdims=True),
                               axis=0, keepdims=True)[0])
        cand = jnp.where(hit, neg, cand)
    wsum = functools.reduce(lambda a, b: a + b, weights)
    row = lax.broadcasted_iota(jnp.int32, (LANES, tm), 0)
    wmat = jnp.zeros((LANES, tm), F32)
    for k in range(TOP_K):
        wmat = jnp.where(row == k, weights[k] / wsum * ROUTED_SCALE, wmat)
    w_ref[...] = wmat.T


def router(x2, g, shift, scale, wr_t, b_router, *, seq, tm=512):
    T, D = x2.shape
    tm = min(tm, seq)
    bpb = seq // tm
    s8 = D // (2 * LANES)
    return pl.pallas_call(
        functools.partial(_router_kernel, s8=s8),
        out_shape=(jax.ShapeDtypeStruct((T * s8, LANES), jnp.uint32),
                   jax.ShapeDtypeStruct((TOP_K, T), jnp.int32),
                   jax.ShapeDtypeStruct((T, LANES), F32)),
        grid=(T // tm,),
        in_specs=[
            pl.BlockSpec((tm, D), lambda i: (i, 0)),
            pl.BlockSpec((1, D), lambda i: (0, 0)),
            pl.BlockSpec((1, 1, D), lambda i: (i // bpb, 0, 0)),
            pl.BlockSpec((1, 1, D), lambda i: (i // bpb, 0, 0)),
            pl.BlockSpec((N_EXPERTS, D), lambda i: (0, 0)),
            pl.BlockSpec((N_GROUPS, GROUP_SIZE, 1), lambda i: (0, 0, 0)),
        ],
        out_specs=(pl.BlockSpec((tm * s8, LANES), lambda i: (i, 0)),
                   pl.BlockSpec((TOP_K, tm), lambda i: (0, i)),
                   pl.BlockSpec((tm, LANES), lambda i: (i, 0))),
        compiler_params=_cparams(("arbitrary",)),
        name="router",
    )(x2, g, shift, scale, wr_t, b_router.reshape(N_GROUPS, GROUP_SIZE, 1).astype(F32))


def _slot_rank_kernel(e_ref, rank_ref, cnt_ref, base_scr, tri_scr):
    tm = e_ref.shape[1]

    @pl.when(pl.program_id(0) == 0)
    def _():
        base_scr[...] = jnp.zeros_like(base_scr)
        r = lax.broadcasted_iota(jnp.int32, (tm, tm), 0)
        c = lax.broadcasted_iota(jnp.int32, (tm, tm), 1)
        tri_scr[...] = jnp.where(r <= c, 1.0, 0.0).astype(BF16)

    expert = lax.broadcasted_iota(jnp.int32, (N_EXPERTS, tm), 0)
    base = base_scr[...]
    for k in range(TOP_K):
        hot = expert == e_ref[k:k + 1, :]
        onehot = jnp.where(hot, 1.0, 0.0)
        incl = jnp.dot(onehot.astype(BF16), tri_scr[...], preferred_element_type=F32)
        rank = jnp.sum(jnp.where(hot, base + incl - 1.0, 0.0), axis=0, keepdims=True)
        rank_ref[k:k + 1, :] = rank.astype(jnp.int32)
        base = base + jnp.sum(onehot, axis=1, keepdims=True)
    base_scr[...] = base
    cnt_ref[...] = jnp.broadcast_to(base, cnt_ref.shape).astype(jnp.int32)


def slot_rank(top_e, *, tm=512):
    K, T = top_e.shape
    tm = min(tm, T)
    return pl.pallas_call(
        _slot_rank_kernel,
        out_shape=(jax.ShapeDtypeStruct((K, T), jnp.int32),
                   jax.ShapeDtypeStruct((N_EXPERTS, LANES), jnp.int32)),
        grid=(T // tm,),
        in_specs=[pl.BlockSpec((K, tm), lambda i: (0, i))],
        out_specs=(pl.BlockSpec((K, tm), lambda i: (0, i)),
                   pl.BlockSpec((N_EXPERTS, LANES), lambda i: (0, 0))),
        scratch_shapes=[pltpu.VMEM((N_EXPERTS, 1), F32), pltpu.VMEM((tm, tm), BF16)],
        compiler_params=_cparams(("arbitrary",)),
        name="slot_rank",
    )(top_e)


def _slot_dest_kernel(pstart_ref, e_ref, rank_ref, dest_ref):
    e = e_ref[...]
    dest = rank_ref[...]
    for x in range(N_EXPERTS):
        dest = dest + jnp.where(e == x, pstart_ref[x], 0)
    dest_ref[...] = dest


def slot_dest(pstarts, top_e, rank, *, tm=4096):
    K, T = top_e.shape
    tm = min(tm, T)
    blockspec = pl.BlockSpec((K, tm), lambda i, ps: (0, i))
    return pl.pallas_call(
        _slot_dest_kernel,
        out_shape=jax.ShapeDtypeStruct((K, T), jnp.int32),
        grid_spec=pltpu.PrefetchScalarGridSpec(
            num_scalar_prefetch=1, grid=(T // tm,),
            in_specs=[blockspec, blockspec], out_specs=blockspec),
        compiler_params=_cparams(("arbitrary",)),
        name="slot_dest",
    )(pstarts, top_e, rank)


def _dispatch_kernel(zrow_ref, zon_ref, dest_ref, h_ref, xs_ref, zero_scr, sem, zsem, *, tm, s8, blk):
    def zero_copy(x):
        row = pl.multiple_of(zrow_ref[x] * s8, 8)
        return pltpu.make_async_copy(zero_scr, xs_ref.at[pl.ds(row, blk * s8)], zsem)

    @pl.when(pl.program_id(0) == 0)
    def _():
        zero_scr[...] = jnp.zeros_like(zero_scr)

        def start(x, c):
            @pl.when(zon_ref[x] > 0)
            def _():
                zero_copy(x).start()
            return c

        def wait(x, c):
            @pl.when(zon_ref[x] > 0)
            def _():
                zero_copy(x).wait()
            return c

        lax.fori_loop(0, 2 * N_EXPERTS, start, 0)
        lax.fori_loop(0, 2 * N_EXPERTS, wait, 0)

    def issue(t, c):
        src = h_ref.at[pl.ds(pl.multiple_of(t * s8, s8), s8)]
        for k in range(TOP_K):
            row = pl.multiple_of(dest_ref[k, t] * s8, s8)
            pltpu.make_async_copy(src, xs_ref.at[pl.ds(row, s8)], sem).start()
        return c

    lax.fori_loop(0, tm, issue, 0)
    for _ in range(TOP_K):
        pltpu.make_async_copy(h_ref, xs_ref.at[pl.ds(0, tm * s8)], sem).wait()


def dispatch(hp, dest, zrow, zon, *, n_rows, blk, tm=256):
    K, T = dest.shape
    s8 = hp.shape[0] // T
    tm = min(tm, T)
    return pl.pallas_call(
        functools.partial(_dispatch_kernel, tm=tm, s8=s8, blk=blk),
        out_shape=jax.ShapeDtypeStruct((n_rows * s8, LANES), jnp.uint32),
        grid_spec=pltpu.PrefetchScalarGridSpec(
            num_scalar_prefetch=2, grid=(T // tm,),
            in_specs=[
                pl.BlockSpec((K, tm), lambda i, zr, zc: (0, i), memory_space=pltpu.SMEM),
                pl.BlockSpec((tm * s8, LANES), lambda i, zr, zc: (i, 0)),
            ],
            out_specs=pl.BlockSpec(memory_space=pl.ANY),
            scratch_shapes=[pltpu.VMEM((blk * s8, LANES), jnp.uint32),
                            pltpu.SemaphoreType.DMA, pltpu.SemaphoreType.DMA],
        ),
        compiler_params=_cparams(("arbitrary",)),
        name="dispatch",
    )(zrow, zon, dest, hp)


def _expert_kernel(be_ref, nb_ref, x_ref, wg_ref, wu_ref, wd_ref, o_ref, *, blk, s8):
    @pl.when(pl.program_id(0) < nb_ref[0])
    def _():
        x = _load_token_major(x_ref, blk, s8)
        a = jnp.dot(x, wg_ref[0], preferred_element_type=F32)
        u = jnp.dot(x, wu_ref[0], preferred_element_type=F32)
        hmid = (a * _sigmoid(a) * u).astype(BF16)
        y = jnp.dot(hmid, wd_ref[0], preferred_element_type=F32)
        _store_token_major(o_ref, _pack_halves(y), s8)

    @pl.when(pl.program_id(0) >= nb_ref[0])
    def _():
        o_ref[...] = jnp.zeros_like(o_ref)


def experts(xs, block_e, n_used, wg, wu, wd, *, blk):
    D, E = wg.shape[1], wg.shape[2]
    s8 = D // (2 * LANES)
    nblk = xs.shape[0] // (blk * s8)

    def used(b, nb):
        return jnp.minimum(b, nb[0] - 1)

    rows = pl.BlockSpec((blk * s8, LANES), lambda b, be, nb: (b, 0))
    grid_spec = pltpu.PrefetchScalarGridSpec(
        num_scalar_prefetch=2,
        grid=(nblk,),
        in_specs=[
            pl.BlockSpec((blk * s8, LANES), lambda b, be, nb: (used(b, nb), 0)),
            pl.BlockSpec((1, D, E), lambda b, be, nb: (be[used(b, nb)], 0, 0)),
            pl.BlockSpec((1, D, E), lambda b, be, nb: (be[used(b, nb)], 0, 0)),
            pl.BlockSpec((1, E, D), lambda b, be, nb: (be[used(b, nb)], 0, 0)),
        ],
        out_specs=rows,
    )
    return pl.pallas_call(
        functools.partial(_expert_kernel, blk=blk, s8=s8),
        out_shape=jax.ShapeDtypeStruct(xs.shape, jnp.uint32),
        grid_spec=grid_spec,
        compiler_params=_cparams(("arbitrary",)),
        name="experts",
    )(block_e, n_used, xs, wg, wu, wd)


def _ffn_out_kernel(dest_ref, h_ref, wt_ref, wg_ref, wu_ref, wd_ref, x_ref, g2_ref, ys_ref, o_ref,
                    buf, sem, *, tm, s8):
    def issue(t, c):
        for k in range(TOP_K):
            row = pl.multiple_of(dest_ref[k, t] * s8, s8)
            pltpu.make_async_copy(ys_ref.at[pl.ds(row, s8)],
                                  buf.at[k, pl.ds(pl.multiple_of(t * s8, s8), s8)], sem).start()
        return c

    lax.fori_loop(0, tm, issue, 0)

    h = _load_token_major(h_ref, tm, s8)
    a = jnp.dot(h, wg_ref[...], preferred_element_type=F32)
    u = jnp.dot(h, wu_ref[...], preferred_element_type=F32)
    hmid = (a * _sigmoid(a) * u).astype(BF16)
    shared = jnp.dot(hmid, wd_ref[...], preferred_element_type=F32)

    for k in range(TOP_K):
        pltpu.make_async_copy(ys_ref.at[pl.ds(0, tm * s8)], buf.at[k], sem).wait()

    half = s8 * LANES
    wts = [jnp.broadcast_to(wt_ref[:, k:k + 1], (tm, LANES)) for k in range(TOP_K)]
    g2 = g2_ref[0]
    for s in range(s8):
        acc_lo = jnp.zeros((tm, LANES), F32)
        acc_hi = jnp.zeros((tm, LANES), F32)
        for k in range(TOP_K):
            lo, hi = _unpack_halves(buf.at[k][pl.ds(s, tm, stride=s8), :])
            acc_lo = acc_lo + wts[k] * lo
            acc_hi = acc_hi + wts[k] * hi
        for off, acc in ((s * LANES, acc_lo), (half + s * LANES, acc_hi)):
            cols = slice(off, off + LANES)
            o_ref[:, cols] = x_ref[:, cols] + g2[:, cols] * (shared[:, cols] + acc)


def ffn_out(hp, ys, dest, w_tok, wsg, wsu, wsd, x2, g2, *, seq, tm=256):
    T, D = x2.shape
    tm = min(tm, seq)
    bpb = seq // tm
    s8 = D // (2 * LANES)
    return pl.pallas_call(
        functools.partial(_ffn_out_kernel, tm=tm, s8=s8),
        out_shape=jax.ShapeDtypeStruct((T, D), F32),
        grid=(T // tm,),
        in_specs=[
            pl.BlockSpec((TOP_K, tm), lambda i: (0, i), memory_space=pltpu.SMEM),
            pl.BlockSpec((tm * s8, LANES), lambda i: (i, 0)),
            pl.BlockSpec((tm, LANES), lambda i: (i, 0)),
            _resident(wsg.shape), _resident(wsu.shape), _resident(wsd.shape),
            pl.BlockSpec((tm, D), lambda i: (i, 0)),
            pl.BlockSpec((1, 1, D), lambda i: (i // bpb, 0, 0)),
            pl.BlockSpec(memory_space=pl.ANY),
        ],
        out_specs=pl.BlockSpec((tm, D), lambda i: (i, 0)),
        scratch_shapes=[pltpu.VMEM((TOP_K, tm * s8, LANES), jnp.uint32), pltpu.SemaphoreType.DMA],
        compiler_params=_cparams(("arbitrary",)),
        name="ffn_out",
    )(dest, hp, w_tok, wsg, wsu, wsd, x2, g2, ys)


def block_plan(counts, *, blk, n_slots):
    padded = ((counts + blk - 1) // blk) * blk
    pends = jnp.cumsum(padded)
    pstarts = (pends - padded).astype(jnp.int32)
    nblk = -(-n_slots // blk) + N_EXPERTS
    block_e = jnp.searchsorted(pends, jnp.arange(nblk) * blk, side='right')
    block_e = jnp.minimum(block_e, N_EXPERTS - 1).astype(jnp.int32)
    n_used = (pends[-1] // blk).astype(jnp.int32)
    tail = n_used + jnp.arange(N_EXPERTS, dtype=jnp.int32)
    zrow = jnp.concatenate([pends - blk, jnp.minimum(tail, nblk - 1) * blk]).astype(jnp.int32)
    zon = jnp.concatenate([counts > 0, tail < nblk]).astype(jnp.int32)
    return pstarts, block_e, n_used.reshape(1), zrow, zon, nblk


def _layer_weights(l, D, norm1_g, norm2_g, w_in, b_fgt, fox_qn_g, fox_kn_g, diff_qn_g, diff_kn_g,
                   diff_out_g, w_branch, w_out, w_router, w_exp_gate, w_exp_up, w_exp_down,
                   w_sh_gate, w_sh_up, w_sh_down):
    W = ATTN_WIDTH
    fcol = 3 * W
    w = w_in[l]
    w_cat = jnp.concatenate([w[:, :fcol], w[:, fcol + N_FOX_HEADS:]], axis=1).astype(BF16)
    w_ff = jnp.zeros((D, LANES), BF16).at[:, :N_FOX_HEADS].set(
        w[:, fcol:fcol + N_FOX_HEADS].astype(BF16))
    qscale = HEAD_DIM ** -0.5
    ones = jnp.ones((W,), F32)
    colgain = jnp.concatenate([
        jnp.tile(fox_qn_g[l].astype(F32) * qscale, N_FOX_HEADS),
        jnp.tile(fox_kn_g[l].astype(F32), N_FOX_HEADS),
        ones,
        jnp.tile(diff_qn_g[l].astype(F32) * qscale, 2 * N_DIFF_HEADS),
        jnp.tile(diff_kn_g[l].astype(F32), 2 * N_DIFF_HEADS),
        ones,
        jnp.ones((2 * D,), F32),
    ]).reshape(1, -1)
    return dict(
        w_cat=w_cat, w_ff=w_ff, colgain=colgain,
        wb0=w_branch[l, 0].astype(BF16), wb1=w_branch[l, 1].astype(BF16), wo=w_out[l].astype(BF16),
        wr_t=w_router[l].T.astype(BF16),
        wg=w_exp_gate[l].astype(BF16), wu=w_exp_up[l].astype(BF16), wd=w_exp_down[l].astype(BF16),
        wsg=w_sh_gate[l].astype(BF16), wsu=w_sh_up[l].astype(BF16), wsd=w_sh_down[l].astype(BF16),
    )


def kernel(x, c, positions, norm1_g, norm2_g, w_ada, b_ada, w_in, b_fgt, fox_qn_g, fox_kn_g,
           diff_qn_g, diff_kn_g, lam_q1, lam_k1, lam_q2, lam_k2, diff_out_g, w_branch, w_out,
           w_router, b_router, w_exp_gate, w_exp_up, w_exp_down, w_sh_gate, w_sh_up, w_sh_down):
    B, S, D = x.shape
    L = w_ada.shape[0]
    T = B * S
    blk = 256
    cos_t, sin_t = rope_tables(positions)
    mod = adaln(c, w_ada, b_ada)
    x2 = x.reshape(T, D)
    for l in range(L):
        lw = _layer_weights(l, D, norm1_g, norm2_g, w_in, b_fgt, fox_qn_g, fox_kn_g, diff_qn_g,
                            diff_kn_g, diff_out_g, w_branch, w_out, w_router, w_exp_gate,
                            w_exp_up, w_exp_down, w_sh_gate, w_sh_up, w_sh_down)
        sh1, sc1, g1, sh2, sc2, g2 = [mod[l, :, k * D:(k + 1) * D].reshape(B, 1, D) for k in range(6)]
        lam_init = 0.8 - 0.6 * math.exp(-0.3 * l)

        proj, ff = inproj(x2, norm1_g[l].reshape(1, D), sh1, sc1, lw['w_cat'], lw['w_ff'],
                          lw['colgain'], cos_t, sin_t, seq=S)
        bias = fgate_bias(ff, b_fgt[l], batch=B, seq=S).reshape(B, N_FOX_HEADS, 1, S)
        proj3 = proj.reshape(B, S, -1)
        o_fox = fox_attention(proj3, bias).reshape(T, ATTN_WIDTH)
        lam_vecs = [v[l].reshape(1, HEAD_DIM).astype(F32) for v in (lam_q1, lam_k1, lam_q2, lam_k2)]
        o_diff = diff_attention(proj3, lam_vecs, diff_out_g[l].reshape(1, -1).astype(F32),
                                lam_init=lam_init).reshape(T, ATTN_WIDTH)
        x2 = post_attention(o_fox, o_diff, proj, lw['wb0'], lw['wb1'], lw['wo'], x2, g1, seq=S)

        hp, top_e, w_tok = router(x2, norm2_g[l].reshape(1, D), sh2, sc2, lw['wr_t'], b_router[l], seq=S)
        rank, counts = slot_rank(top_e)
        counts = counts[:, 0]
        pstarts, block_e, n_used, zrow, zon, nblk = block_plan(counts, blk=blk, n_slots=T * TOP_K)
        dest = slot_dest(pstarts, top_e, rank)
        xs = dispatch(hp, dest, zrow, zon, n_rows=nblk * blk, blk=blk)
        ys = experts(xs, block_e, n_used, lw['wg'], lw['wu'], lw['wd'], blk=blk)
        x2 = ffn_out(hp, ys, dest, w_tok, lw['wsg'], lw['wsu'], lw['wsd'], x2, g2, seq=S)
    return x2.reshape(B, S, D)
```

```python
import functools
import math

import jax
import jax.numpy as jnp
import numpy as np
from jax import lax
from jax.experimental import pallas as pl
from jax.experimental.pallas import tpu as pltpu

F32 = jnp.float32
BF16 = jnp.bfloat16

HEAD_DIM = 128
N_FOX_HEADS = 8
N_DIFF_HEADS = 4
ATTN_WIDTH = N_FOX_HEADS * HEAD_DIM
CHUNK = 64
ROT_DIM = HEAD_DIM // 4
ROPE_THETA = 500000.0
N_EXPERTS = 64
N_GROUPS = 8
GROUP_SIZE = N_EXPERTS // N_GROUPS
TOPK_GROUPS = 4
TOP_K = 8
ROUTED_SCALE = 2.5
RMS_EPS = 1e-6
LANES = 128
V7X_VMEM_BYTES = 64 * 1024 * 1024
VMEM_LIMIT = V7X_VMEM_BYTES - 8 * 1024 * 1024

SEC_FQ, SEC_FK, SEC_DQ, SEC_DK, SEC_GATE = range(5)
LOG2E = math.log2(math.e)
BIAS_LANES = 3


def _cparams(sem):
    return pltpu.CompilerParams(dimension_semantics=sem, vmem_limit_bytes=VMEM_LIMIT)


def _rms(y, axis=-1):
    return y * lax.rsqrt(jnp.mean(y * y, axis=axis, keepdims=True) + RMS_EPS)


def _sigmoid(z):
    return 1.0 / (1.0 + jnp.exp(-z))


def _adaln_kernel(c_ref, w_ref, b_ref, o_ref):
    c = c_ref[...]
    cond = (c * _sigmoid(c)).astype(BF16)
    o_ref[0] = jnp.dot(cond, w_ref[0].astype(BF16), preferred_element_type=F32) + b_ref[0]


def adaln(c, w_ada, b_ada, *, tn=1024):
    L, D, N = w_ada.shape
    B = c.shape[0]
    tn = min(tn, N)
    return pl.pallas_call(
        _adaln_kernel,
        out_shape=jax.ShapeDtypeStruct((L, B, N), F32),
        grid=(L, N // tn),
        in_specs=[
            pl.BlockSpec((B, D), lambda l, j: (0, 0)),
            pl.BlockSpec((1, D, tn), lambda l, j: (l, 0, j)),
            pl.BlockSpec((1, 1, tn), lambda l, j: (l, 0, j)),
        ],
        out_specs=pl.BlockSpec((1, B, tn), lambda l, j: (l, 0, j)),
        compiler_params=_cparams(("arbitrary", "arbitrary")),
        name="adaln",
    )(c, w_ada, b_ada.reshape(L, 1, N))


def _rope_kernel(pos_ref, invf_ref, cos_ref, sin_ref):
    ang = pos_ref[...].astype(F32) * invf_ref[...]
    lane = lax.broadcasted_iota(jnp.int32, ang.shape, 1)
    c = jnp.cos(ang)
    s = jnp.sin(ang)
    cos_ref[...] = jnp.where(lane < ROT_DIM, c, 1.0)
    sin_ref[...] = jnp.where(lane < ROT_DIM // 2, -s, jnp.where(lane < ROT_DIM, s, 0.0))


def rope_tables(positions, *, tm=1024):
    T = positions.size
    tm = min(tm, T)
    half = ROT_DIM // 2
    inv_freq = ROPE_THETA ** (-jnp.arange(0, ROT_DIM, 2, dtype=F32) / ROT_DIM)
    invf = jnp.zeros((1, LANES), F32).at[0, :half].set(inv_freq).at[0, half:ROT_DIM].set(inv_freq)
    return pl.pallas_call(
        _rope_kernel,
        out_shape=(jax.ShapeDtypeStruct((T, LANES), F32),) * 2,
        grid=(T // tm,),
        in_specs=[
            pl.BlockSpec((tm, 1), lambda i: (i, 0)),
            pl.BlockSpec((1, LANES), lambda i: (0, 0)),
        ],
        out_specs=(pl.BlockSpec((tm, LANES), lambda i: (i, 0)),) * 2,
        compiler_params=_cparams(("arbitrary",)),
        name="rope_tables",
    )(positions.reshape(T, 1), invf)


def _inproj_kernel(x_ref, g_ref, sh_ref, sc_ref, w_ref, wff_ref, cg_ref, cos_ref, sin_ref,
                   o_ref, vt_ref, ff_ref, h_scr, *, sec_v):
    j = pl.program_id(1)

    @pl.when(j == 0)
    def _():
        y = _rms(x_ref[...]) * g_ref[...]
        h = (y * (1.0 + sc_ref[0]) + sh_ref[0]).astype(BF16)
        h_scr[...] = h
        ff_ref[...] = jnp.dot(h, wff_ref[...], preferred_element_type=F32)

    acc = jnp.dot(h_scr[...], w_ref[...], preferred_element_type=F32)
    n_heads = ATTN_WIDTH // HEAD_DIM

    def head(a, hd):
        return a[:, hd * HEAD_DIM:(hd + 1) * HEAD_DIM]

    is_norm = (j == SEC_FQ) | (j == SEC_FK)
    is_rope = (j == SEC_DQ) | (j == SEC_DK)

    @pl.when(is_norm)
    def _():
        for hd in range(n_heads):
            o_ref[:, hd * HEAD_DIM:(hd + 1) * HEAD_DIM] = (
                _rms(head(acc, hd)) * head(cg_ref[...], hd)).astype(o_ref.dtype)

    @pl.when(is_rope)
    def _():
        cos_t = cos_ref[...]
        sin_t = sin_ref[...]
        lane = lax.broadcasted_iota(jnp.int32, cos_t.shape, 1)
        low = lane < ROT_DIM // 2
        for hd in range(n_heads):
            y = _rms(head(acc, hd)) * head(cg_ref[...], hd)
            rot = jnp.where(low, pltpu.roll(y, HEAD_DIM - ROT_DIM // 2, 1),
                            pltpu.roll(y, ROT_DIM // 2, 1))
            o_ref[:, hd * HEAD_DIM:(hd + 1) * HEAD_DIM] = (y * cos_t + rot * sin_t).astype(o_ref.dtype)

    @pl.when((j >= SEC_GATE) & (j < sec_v))
    def _():
        o_ref[...] = _sigmoid(acc).astype(o_ref.dtype)

    @pl.when(j >= sec_v)
    def _():
        vt_ref[...] = acc.T.astype(vt_ref.dtype)


def inproj(x2, g, shift, scale, w_cat, w_ff, colgain, cos_t, sin_t, *, seq, tm=1024):
    T, D = x2.shape
    NC = w_cat.shape[1]
    tn = ATTN_WIDTH
    tm = min(tm, seq)
    bpb = seq // tm
    sec_v = NC // tn - 2
    return pl.pallas_call(
        functools.partial(_inproj_kernel, sec_v=sec_v),
        out_shape=(jax.ShapeDtypeStruct((T, sec_v * tn), BF16),
                   jax.ShapeDtypeStruct((2 * tn, T), BF16),
                   jax.ShapeDtypeStruct((T, LANES), F32)),
        grid=(T // tm, NC // tn),
        in_specs=[
            pl.BlockSpec((tm, D), lambda i, j: (i, 0)),
            pl.BlockSpec((1, D), lambda i, j: (0, 0)),
            pl.BlockSpec((1, 1, D), lambda i, j: (i // bpb, 0, 0)),
            pl.BlockSpec((1, 1, D), lambda i, j: (i // bpb, 0, 0)),
            pl.BlockSpec((D, tn), lambda i, j: (0, j)),
            pl.BlockSpec((D, LANES), lambda i, j: (0, 0)),
            pl.BlockSpec((1, tn), lambda i, j: (0, j)),
            pl.BlockSpec((tm, LANES), lambda i, j: (i, 0)),
            pl.BlockSpec((tm, LANES), lambda i, j: (i, 0)),
        ],
        out_specs=(
            pl.BlockSpec((tm, tn), lambda i, j: (i, jnp.minimum(j, sec_v - 1))),
            pl.BlockSpec((tn, tm), lambda i, j: (jnp.maximum(j - sec_v, 0), i)),
            pl.BlockSpec((tm, LANES), lambda i, j: (i, 0)),
        ),
        scratch_shapes=[pltpu.VMEM((tm, D), BF16)],
        compiler_params=_cparams(("arbitrary", "arbitrary")),
        name="inproj",
    )(x2, g, shift, scale, w_cat, w_ff, colgain, cos_t, sin_t)


def _split3(v):
    hi = v.astype(BF16)
    r = v - hi.astype(F32)
    mid = r.astype(BF16)
    lo = (r - mid.astype(F32)).astype(BF16)
    return hi, mid, lo


def _fgate_kernel(ff_ref, b_ref, k_ref, o_ref, carry_scr):
    @pl.when(pl.program_id(1) == 0)
    def _():
        carry_scr[...] = jnp.zeros_like(carry_scr)

    z = ff_ref[...] + b_ref[...]
    logf = jnp.minimum(z, 0.0) - jnp.log(1.0 + jnp.exp(-jnp.abs(z)))
    tm = z.shape[0]
    row = lax.broadcasted_iota(jnp.int32, (tm, tm), 0)
    col = lax.broadcasted_iota(jnp.int32, (tm, tm), 1)
    tri = jnp.where(row >= col, 1.0, 0.0).astype(BF16)
    cum = carry_scr[...]
    for part in _split3(logf):
        cum = cum + jnp.dot(tri, part, preferred_element_type=F32)
    carry_scr[...] = cum[tm - 1:tm, :]
    bias = cum * (-LOG2E)
    lane = lax.broadcasted_iota(jnp.int32, (tm, HEAD_DIM), 1)
    for hd in range(N_FOX_HEADS):
        hi, mid, lo = [p.astype(F32) for p in _split3(bias[:, hd:hd + 1])]
        ext = jnp.where(lane == 0, hi, jnp.where(lane == 1, mid, jnp.where(lane == 2, lo, 0.0)))
        o_ref[0, hd, :, :HEAD_DIM] = k_ref[:, hd * HEAD_DIM:(hd + 1) * HEAD_DIM]
        o_ref[0, hd, :, HEAD_DIM:] = ext.astype(o_ref.dtype)


def fgate_keys(ff, b_fgt, proj, *, batch, seq, tm=512):
    tm = min(tm, seq)
    b_pad = jnp.zeros((1, LANES), F32).at[0, :N_FOX_HEADS].set(b_fgt.astype(F32))
    nb = seq // tm
    return pl.pallas_call(
        _fgate_kernel,
        out_shape=jax.ShapeDtypeStruct((batch, N_FOX_HEADS, seq, 2 * HEAD_DIM), BF16),
        grid=(batch, nb),
        in_specs=[
            pl.BlockSpec((tm, LANES), lambda b, i: (b * nb + i, 0)),
            pl.BlockSpec((1, LANES), lambda b, i: (0, 0)),
            pl.BlockSpec((tm, ATTN_WIDTH), lambda b, i: (b * nb + i, SEC_FK)),
        ],
        out_specs=pl.BlockSpec((1, N_FOX_HEADS, tm, 2 * HEAD_DIM), lambda b, i: (b, 0, i, 0)),
        scratch_shapes=[pltpu.VMEM((1, LANES), F32)],
        compiler_params=_cparams(("arbitrary", "arbitrary")),
        name="fgate",
    )(ff, b_pad, proj)


_NT = (((1,), (1,)), ((), ()))
ONES_ROWS = 16


def _flash_step(sT, vT, m_ref, acc_ref):
    m_old = m_ref[...]
    m_new = jnp.maximum(m_old, jnp.max(sT, axis=0, keepdims=True))
    alpha = jnp.exp2(m_old - m_new)
    pT = jnp.exp2(sT - m_new).astype(vT.dtype)
    acc_ref[...] = alpha * acc_ref[...] + jnp.dot(vT, pT, preferred_element_type=F32)
    m_ref[...] = m_new


def _kv_schedule(i, tq, step):
    def pair(jj, carry):
        step(jj * (2 * tq), 2 * tq, False)
        return carry

    lax.fori_loop(0, i // 2, pair, 0)

    @pl.when(i % 2 == 1)
    def _():
        step((i - 1) * tq, tq, False)

    step(i * tq, tq, True)


def _with_ones(vT):
    return jnp.concatenate([vT, jnp.ones((ONES_ROWS, vT.shape[1]), vT.dtype)], axis=0)


def _fox_kernel(q_ref, k_ref, vt_ref, o_ref, m_scr, acc_scr, *, tq):
    i = pl.program_id(2)
    lane = lax.broadcasted_iota(jnp.int32, (tq, HEAD_DIM), 1)
    q = jnp.concatenate([q_ref[0], jnp.where(lane < BIAS_LANES, 1.0, 0.0).astype(q_ref.dtype)], axis=1)
    m_scr[...] = jnp.full_like(m_scr, -jnp.inf)
    acc_scr[...] = jnp.zeros_like(acc_scr)

    def step(ks, n, masked):
        ks = pl.multiple_of(ks, tq)
        k = k_ref[0, 0, pl.ds(ks, n), :]
        sT = lax.dot_general(k, q, _NT, preferred_element_type=F32)
        if masked:
            key = lax.broadcasted_iota(jnp.int32, sT.shape, 0)
            qry = lax.broadcasted_iota(jnp.int32, sT.shape, 1)
            sT = jnp.where(key <= qry, sT, -jnp.inf)
        _flash_step(sT, _with_ones(vt_ref[:, pl.ds(ks, n)]), m_scr, acc_scr)

    _kv_schedule(i, tq, step)
    o = acc_scr[:HEAD_DIM, :] / acc_scr[HEAD_DIM:HEAD_DIM + 1, :]
    o_ref[0] = o.T.astype(o_ref.dtype)


def fox_attention(proj3, k_aug, vt, *, tq=512):
    B, S, _ = proj3.shape
    tq = min(tq, S)
    H = N_FOX_HEADS
    return pl.pallas_call(
        functools.partial(_fox_kernel, tq=tq),
        out_shape=jax.ShapeDtypeStruct((B, S, ATTN_WIDTH), BF16),
        grid=(B, H, S // tq),
        in_specs=[
            pl.BlockSpec((1, tq, HEAD_DIM), lambda b, h, i: (b, i, SEC_FQ * H + h)),
            pl.BlockSpec((1, 1, S, 2 * HEAD_DIM), lambda b, h, i: (b, h, 0, 0)),
            pl.BlockSpec((HEAD_DIM, S), lambda b, h, i: (h, b)),
        ],
        out_specs=pl.BlockSpec((1, tq, HEAD_DIM), lambda b, h, i: (b, i, h)),
        scratch_shapes=[pltpu.VMEM((1, tq), F32), pltpu.VMEM((HEAD_DIM + ONES_ROWS, tq), F32)],
        compiler_params=_cparams(("arbitrary", "arbitrary", "arbitrary")),
        name="fox_attn",
    )(proj3, k_aug, vt)


def _diff_kernel(q_ref, k_ref, vt_ref, lq1_ref, lk1_ref, lq2_ref, lk2_ref, g_ref, o_ref,
                 m1, a1, m2, a2, *, tq, lam_init):
    i = pl.program_id(2)
    dv = 2 * HEAD_DIM
    for m, a in ((m1, a1), (m2, a2)):
        m[...] = jnp.full_like(m, -jnp.inf)
        a[...] = jnp.zeros_like(a)

    def step(ks, n, masked):
        ks = pl.multiple_of(ks, tq)
        vT = _with_ones(vt_ref[:, pl.ds(ks, n)])
        for mp, (m, a) in enumerate(((m1, a1), (m2, a2))):
            q = q_ref[0, :, mp * HEAD_DIM:(mp + 1) * HEAD_DIM]
            k = k_ref[0, pl.ds(ks, n), mp * HEAD_DIM:(mp + 1) * HEAD_DIM]
            sT = lax.dot_general(k, q, _NT, preferred_element_type=F32)
            if masked:
                key = lax.broadcasted_iota(jnp.int32, sT.shape, 0)
                qry = lax.broadcasted_iota(jnp.int32, sT.shape, 1)
                sT = jnp.where(key // CHUNK <= qry // CHUNK, sT, -jnp.inf)
            _flash_step(sT, vT, m, a)

    _kv_schedule(i, tq, step)
    lam = (jnp.exp(jnp.sum(lq1_ref[...] * lk1_ref[...], axis=-1, keepdims=True))
           - jnp.exp(jnp.sum(lq2_ref[...] * lk2_ref[...], axis=-1, keepdims=True)) + lam_init)
    o = a1[:dv, :] / a1[dv:dv + 1, :] - lam * (a2[:dv, :] / a2[dv:dv + 1, :])
    o = _rms(o, axis=0) * g_ref[...] * (1.0 - lam_init)
    o_ref[0] = o.T.astype(o_ref.dtype)


def diff_attention(proj3, vt, lam_vecs, out_gain, *, lam_init, tq=512):
    B, S, _ = proj3.shape
    tq = min(tq, S)
    H = N_DIFF_HEADS
    dv = 2 * HEAD_DIM
    vec = pl.BlockSpec((1, HEAD_DIM), lambda b, h, i: (0, 0))
    return pl.pallas_call(
        functools.partial(_diff_kernel, tq=tq, lam_init=lam_init),
        out_shape=jax.ShapeDtypeStruct((B, S, ATTN_WIDTH), BF16),
        grid=(B, H, S // tq),
        in_specs=[
            pl.BlockSpec((1, tq, dv), lambda b, h, i: (b, i, SEC_DQ * H + h)),
            pl.BlockSpec((1, S, dv), lambda b, h, i: (b, 0, SEC_DK * H + h)),
            pl.BlockSpec((dv, S), lambda b, h, i: (H + h, b)),
            vec, vec, vec, vec,
            pl.BlockSpec((dv, 1), lambda b, h, i: (0, 0)),
        ],
        out_specs=pl.BlockSpec((1, tq, dv), lambda b, h, i: (b, i, h)),
        scratch_shapes=[pltpu.VMEM((1, tq), F32), pltpu.VMEM((dv + ONES_ROWS, tq), F32)] * 2,
        compiler_params=_cparams(("arbitrary", "arbitrary", "arbitrary")),
        name="diff_attn",
    )(proj3, proj3, vt, *lam_vecs, out_gain)


def _post_kernel(of_ref, od_ref, ga_ref, gb_ref, wb0_ref, wb1_ref, wo_ref, x_ref, g1_ref, o_ref):
    y0 = jnp.dot(of_ref[...], wb0_ref[...], preferred_element_type=F32)
    y1 = jnp.dot(od_ref[...], wb1_ref[...], preferred_element_type=F32)
    merged = ga_ref[...].astype(F32) * y0 + gb_ref[...].astype(F32) * y1
    mix = jnp.dot(merged.astype(BF16), wo_ref[...], preferred_element_type=F32)
    o_ref[...] = x_ref[...] + g1_ref[0] * mix


def _resident(shape):
    nd = len(shape)
    return pl.BlockSpec(shape, lambda *_: (0,) * nd, pipeline_mode=pl.Buffered(1))


def post_attention(o_fox, o_diff, proj, wb0, wb1, wo, x2, g1, *, seq, tm=256):
    T, D = x2.shape
    tm = min(tm, seq)
    bpb = seq // tm
    gate_blk = SEC_GATE * ATTN_WIDTH // D
    return pl.pallas_call(
        _post_kernel,
        out_shape=jax.ShapeDtypeStruct((T, D), F32),
        grid=(T // tm,),
        in_specs=[
            pl.BlockSpec((tm, ATTN_WIDTH), lambda i: (i, 0)),
            pl.BlockSpec((tm, ATTN_WIDTH), lambda i: (i, 0)),
            pl.BlockSpec((tm, D), lambda i: (i, gate_blk)),
            pl.BlockSpec((tm, D), lambda i: (i, gate_blk + 1)),
            _resident(wb0.shape), _resident(wb1.shape), _resident(wo.shape),
            pl.BlockSpec((tm, D), lambda i: (i, 0)),
            pl.BlockSpec((1, 1, D), lambda i: (i // bpb, 0, 0)),
        ],
        out_specs=pl.BlockSpec((tm, D), lambda i: (i, 0)),
        compiler_params=_cparams(("arbitrary",)),
        name="post_attn",
    )(o_fox, o_diff, proj, proj, wb0, wb1, wo, x2, g1)


def _pack_halves(y):
    half = y.shape[1] // 2
    bits = lax.bitcast_convert_type(y.astype(BF16).astype(F32), jnp.uint32)
    return (bits[:, half:] & jnp.uint32(0xFFFF0000)) | (bits[:, :half] >> 16)


def _unpack_halves(w):
    lo = lax.bitcast_convert_type(w << 16, F32)
    hi = lax.bitcast_convert_type(w & jnp.uint32(0xFFFF0000), F32)
    return lo, hi


def _store_token_major(ref, words, s8):
    m = words.shape[0]
    for s in range(s8):
        ref[pl.ds(s, m, stride=s8), :] = words[:, s * LANES:(s + 1) * LANES]


def _load_token_major(ref, m, s8):
    parts = [_unpack_halves(ref[pl.ds(s, m, stride=s8), :]) for s in range(s8)]
    return jnp.concatenate([p[0].astype(BF16) for p in parts] + [p[1].astype(BF16) for p in parts],
                           axis=1)


def _router_kernel(x_ref, g_ref, sh_ref, sc_ref, wr_ref, br_ref, h_ref, e_ref, w_ref, *, s8):
    y = _rms(x_ref[...]) * g_ref[...]
    h = y * (1.0 + sc_ref[0]) + sh_ref[0]
    hb = h.astype(BF16)
    _store_token_major(h_ref, _pack_halves(h), s8)
    logits = lax.dot_general(wr_ref[...], hb, _NT, preferred_element_type=F32)
    scores = _sigmoid(logits)
    tm = scores.shape[1]
    sc3 = scores.reshape(N_GROUPS, GROUP_SIZE, tm)
    sel = sc3 + br_ref[...]
    neg = -jnp.inf
    member = lax.broadcasted_iota(jnp.int32, sel.shape, 1)
    group = lax.broadcasted_iota(jnp.int32, sel.shape, 0)

    m1 = jnp.max(sel, axis=1, keepdims=True)
    i1 = jnp.min(jnp.where(sel == m1, member, GROUP_SIZE), axis=1, keepdims=True)
    m2 = jnp.max(jnp.where(member == i1, neg, sel), axis=1, keepdims=True)
    gscore = m1 + m2

    gidx = lax.broadcasted_iota(jnp.int32, gscore.shape, 0)
    gkeep = jnp.zeros(gscore.shape, jnp.int32)
    for _ in range(TOPK_GROUPS):
        best = jnp.max(gscore, axis=0, keepdims=True)
        bi = jnp.min(jnp.where(gscore == best, gidx, N_GROUPS), axis=0, keepdims=True)
        hit = gidx == bi
        gkeep = jnp.where(hit, 1, gkeep)
        gscore = jnp.where(hit, neg, gscore)

    cand = jnp.where(gkeep > 0, sel, neg)
    eidx = group * GROUP_SIZE + member
    weights = []
    for k in range(TOP_K):
        best = jnp.max(jnp.max(cand, axis=1, keepdims=True), axis=0, keepdims=True)
        bi = jnp.min(jnp.min(jnp.where(cand == best, eidx, N_EXPERTS), axis=1, keepdims=True),
                     axis=0, keepdims=True)
        hit = eidx == bi
        e_ref[k:k + 1, :] = bi[0]
        weights.append(jnp.sum(jnp.sum(jnp.where(hit, sc3, 0.0), axis=1, keepdims=True),
                               axis=0, keepdims=True)[0])
        cand = jnp.where(hit, neg, cand)
    wsum = functools.reduce(lambda a, b: a + b, weights)
    row = lax.broadcasted_iota(jnp.int32, (LANES, tm), 0)
    wmat = jnp.zeros((LANES, tm), F32)
    for k in range(TOP_K):
        wmat = jnp.where(row == k, weights[k] / wsum * ROUTED_SCALE, wmat)
    w_ref[...] = wmat.T


def router(x2, g, shift, scale, wr_t, b_router, *, seq, tm=512):
    T, D = x2.shape
    tm = min(tm, seq)
    bpb = seq // tm
    s8 = D // (2 * LANES)
    return pl.pallas_call(
        functools.partial(_router_kernel, s8=s8),
        out_shape=(jax.ShapeDtypeStruct((T * s8, LANES), jnp.uint32),
                   jax.ShapeDtypeStruct((TOP_K, T), jnp.int32),
                   jax.ShapeDtypeStruct((T, LANES), F32)),
        grid=(T // tm,),
        in_specs=[
            pl.BlockSpec((tm, D), lambda i: (i, 0)),
            pl.BlockSpec((1, D), lambda i: (0, 0)),
            pl.BlockSpec((1, 1, D), lambda i: (i // bpb, 0, 0)),
            pl.BlockSpec((1, 1, D), lambda i: (i // bpb, 0, 0)),
            pl.BlockSpec((N_EXPERTS, D), lambda i: (0, 0)),
            pl.BlockSpec((N_GROUPS, GROUP_SIZE, 1), lambda i: (0, 0, 0)),
        ],
        out_specs=(pl.BlockSpec((tm * s8, LANES), lambda i: (i, 0)),
                   pl.BlockSpec((TOP_K, tm), lambda i: (0, i)),
                   pl.BlockSpec((tm, LANES), lambda i: (i, 0))),
        compiler_params=_cparams(("arbitrary",)),
        name="router",
    )(x2, g, shift, scale, wr_t, b_router.reshape(N_GROUPS, GROUP_SIZE, 1).astype(F32))


def _slot_rank_kernel(e_ref, rank_ref, cnt_ref, base_scr, tri_scr):
    tm = e_ref.shape[1]

    @pl.when(pl.program_id(0) == 0)
    def _():
        base_scr[...] = jnp.zeros_like(base_scr)
        r = lax.broadcasted_iota(jnp.int32, (tm, tm), 0)
        c = lax.broadcasted_iota(jnp.int32, (tm, tm), 1)
        tri_scr[...] = jnp.where(r <= c, 1.0, 0.0).astype(BF16)

    expert = lax.broadcasted_iota(jnp.int32, (N_EXPERTS, tm), 0)
    base = base_scr[...]
    for k in range(TOP_K):
        hot = expert == e_ref[k:k + 1, :]
        onehot = jnp.where(hot, 1.0, 0.0)
        incl = jnp.dot(onehot.astype(BF16), tri_scr[...], preferred_element_type=F32)
        rank = jnp.sum(jnp.where(hot, base + incl - 1.0, 0.0), axis=0, keepdims=True)
        rank_ref[k:k + 1, :] = rank.astype(jnp.int32)
        base = base + jnp.sum(onehot, axis=1, keepdims=True)
    base_scr[...] = base
    cnt_ref[...] = jnp.broadcast_to(base, cnt_ref.shape).astype(jnp.int32)


def slot_rank(top_e, *, tm=512):
    K, T = top_e.shape
    tm = min(tm, T)
    return pl.pallas_call(
        _slot_rank_kernel,
        out_shape=(jax.ShapeDtypeStruct((K, T), jnp.int32),
                   jax.ShapeDtypeStruct((N_EXPERTS, LANES), jnp.int32)),
        grid=(T // tm,),
        in_specs=[pl.BlockSpec((K, tm), lambda i: (0, i))],
        out_specs=(pl.BlockSpec((K, tm), lambda i: (0, i)),
                   pl.BlockSpec((N_EXPERTS, LANES), lambda i: (0, 0))),
        scratch_shapes=[pltpu.VMEM((N_EXPERTS, 1), F32), pltpu.VMEM((tm, tm), BF16)],
        compiler_params=_cparams(("arbitrary",)),
        name="slot_rank",
    )(top_e)


def _slot_dest_kernel(pstart_ref, e_ref, rank_ref, dest_ref):
    e = e_ref[...]
    dest = rank_ref[...]
    for x in range(N_EXPERTS):
        dest = dest + jnp.where(e == x, pstart_ref[x], 0)
    dest_ref[...] = dest


def slot_dest(pstarts, top_e, rank, *, tm=4096):
    K, T = top_e.shape
    tm = min(tm, T)
    blockspec = pl.BlockSpec((K, tm), lambda i, ps: (0, i))
    return pl.pallas_call(
        _slot_dest_kernel,
        out_shape=jax.ShapeDtypeStruct((K, T), jnp.int32),
        grid_spec=pltpu.PrefetchScalarGridSpec(
            num_scalar_prefetch=1, grid=(T // tm,),
            in_specs=[blockspec, blockspec], out_specs=blockspec),
        compiler_params=_cparams(("arbitrary",)),
        name="slot_dest",
    )(pstarts, top_e, rank)


def _dispatch_kernel(zrow_ref, zon_ref, dest_ref, h_ref, xs_ref, zero_scr, sem, zsem, *, tm, s8, blk):
    def zero_copy(x):
        row = pl.multiple_of(zrow_ref[x] * s8, 8)
        return pltpu.make_async_copy(zero_scr, xs_ref.at[pl.ds(row, blk * s8)], zsem)

    @pl.when(pl.program_id(0) == 0)
    def _():
        zero_scr[...] = jnp.zeros_like(zero_scr)

        def start(x, c):
            @pl.when(zon_ref[x] > 0)
            def _():
                zero_copy(x).start()
            return c

        def wait(x, c):
            @pl.when(zon_ref[x] > 0)
            def _():
                zero_copy(x).wait()
            return c

        lax.fori_loop(0, 2 * N_EXPERTS, start, 0)
        lax.fori_loop(0, 2 * N_EXPERTS, wait, 0)

    def issue(t, c):
        src = h_ref.at[pl.ds(pl.multiple_of(t * s8, s8), s8)]
        for k in range(TOP_K):
            row = pl.multiple_of(dest_ref[k, t] * s8, s8)
            pltpu.make_async_copy(src, xs_ref.at[pl.ds(row, s8)], sem).start()
        return c

    lax.fori_loop(0, tm, issue, 0)
    for _ in range(TOP_K):
        pltpu.make_async_copy(h_ref, xs_ref.at[pl.ds(0, tm * s8)], sem).wait()


def dispatch(hp, dest, zrow, zon, *, n_rows, blk, tm=256):
    K, T = dest.shape
    s8 = hp.shape[0] // T
    tm = min(tm, T)
    return pl.pallas_call(
        functools.partial(_dispatch_kernel, tm=tm, s8=s8, blk=blk),
        out_shape=jax.ShapeDtypeStruct((n_rows * s8, LANES), jnp.uint32),
        grid_spec=pltpu.PrefetchScalarGridSpec(
            num_scalar_prefetch=2, grid=(T // tm,),
            in_specs=[
                pl.BlockSpec((K, tm), lambda i, zr, zc: (0, i), memory_space=pltpu.SMEM),
                pl.BlockSpec((tm * s8, LANES), lambda i, zr, zc: (i, 0)),
            ],
            out_specs=pl.BlockSpec(memory_space=pl.ANY),
            scratch_shapes=[pltpu.VMEM((blk * s8, LANES), jnp.uint32),
                            pltpu.SemaphoreType.DMA, pltpu.SemaphoreType.DMA],
        ),
        compiler_params=_cparams(("arbitrary",)),
        name="dispatch",
    )(zrow, zon, dest, hp)


def _expert_kernel(be_ref, nb_ref, x_ref, wg_ref, wu_ref, wd_ref, o_ref, wg_s, wu_s, wd_s, *, blk, s8):
    b = pl.program_id(0)

    @pl.when((b == 0) | (be_ref[b] != be_ref[jnp.maximum(b - 1, 0)]))
    def _():
        wg_s[...] = wg_ref[0, 0].astype(BF16)
        wu_s[...] = wu_ref[0, 0].astype(BF16)
        wd_s[...] = wd_ref[0, 0].astype(BF16)

    @pl.when(b < nb_ref[0])
    def _():
        x = _load_token_major(x_ref, blk, s8)
        a = jnp.dot(x, wg_s[...], preferred_element_type=F32)
        u = jnp.dot(x, wu_s[...], preferred_element_type=F32)
        hmid = (a * _sigmoid(a) * u).astype(BF16)
        y = jnp.dot(hmid, wd_s[...], preferred_element_type=F32)
        _store_token_major(o_ref, _pack_halves(y), s8)

    @pl.when(b >= nb_ref[0])
    def _():
        o_ref[...] = jnp.zeros_like(o_ref)


def experts(xs, block_e, n_used, wg, wu, wd, *, layer, blk):
    D, E = wg.shape[2], wg.shape[3]
    s8 = D // (2 * LANES)
    nblk = xs.shape[0] // (blk * s8)

    def used(b, nb):
        return jnp.minimum(b, nb[0] - 1)

    def weight(shape):
        return pl.BlockSpec((1, 1) + shape, lambda b, be, nb: (layer, be[used(b, nb)], 0, 0))

    grid_spec = pltpu.PrefetchScalarGridSpec(
        num_scalar_prefetch=2,
        grid=(nblk,),
        in_specs=[
            pl.BlockSpec((blk * s8, LANES), lambda b, be, nb: (used(b, nb), 0)),
            weight((D, E)), weight((D, E)), weight((E, D)),
        ],
        out_specs=pl.BlockSpec((blk * s8, LANES), lambda b, be, nb: (b, 0)),
        scratch_shapes=[pltpu.VMEM((D, E), BF16), pltpu.VMEM((D, E), BF16), pltpu.VMEM((E, D), BF16)],
    )
    return pl.pallas_call(
        functools.partial(_expert_kernel, blk=blk, s8=s8),
        out_shape=jax.ShapeDtypeStruct(xs.shape, jnp.uint32),
        grid_spec=grid_spec,
        compiler_params=_cparams(("arbitrary",)),
        name="experts",
    )(block_e, n_used, xs, wg, wu, wd)


def _ffn_out_kernel(dest_ref, h_ref, wt_ref, wg_ref, wu_ref, wd_ref, x_ref, g2_ref, ys_ref, o_ref,
                    buf, sem, *, tm, s8):
    def issue(t, c):
        for k in range(TOP_K):
            row = pl.multiple_of(dest_ref[k, t] * s8, s8)
            pltpu.make_async_copy(ys_ref.at[pl.ds(row, s8)],
                                  buf.at[k, pl.ds(pl.multiple_of(t * s8, s8), s8)], sem).start()
        return c

    lax.fori_loop(0, tm, issue, 0)

    h = _load_token_major(h_ref, tm, s8)
    a = jnp.dot(h, wg_ref[...], preferred_element_type=F32)
    u = jnp.dot(h, wu_ref[...], preferred_element_type=F32)
    hmid = (a * _sigmoid(a) * u).astype(BF16)
    shared = jnp.dot(hmid, wd_ref[...], preferred_element_type=F32)

    for k in range(TOP_K):
        pltpu.make_async_copy(ys_ref.at[pl.ds(0, tm * s8)], buf.at[k], sem).wait()

    half = s8 * LANES
    wts = [jnp.broadcast_to(wt_ref[:, k:k + 1], (tm, LANES)) for k in range(TOP_K)]
    g2 = g2_ref[0]
    for s in range(s8):
        acc_lo = jnp.zeros((tm, LANES), F32)
        acc_hi = jnp.zeros((tm, LANES), F32)
        for k in range(TOP_K):
            lo, hi = _unpack_halves(buf.at[k][pl.ds(s, tm, stride=s8), :])
            acc_lo = acc_lo + wts[k] * lo
            acc_hi = acc_hi + wts[k] * hi
        for off, acc in ((s * LANES, acc_lo), (half + s * LANES, acc_hi)):
            cols = slice(off, off + LANES)
            o_ref[:, cols] = x_ref[:, cols] + g2[:, cols] * (shared[:, cols] + acc)


def ffn_out(hp, ys, dest, w_tok, wsg, wsu, wsd, x2, g2, *, seq, tm=256):
    T, D = x2.shape
    tm = min(tm, seq)
    bpb = seq // tm
    s8 = D // (2 * LANES)
    return pl.pallas_call(
        functools.partial(_ffn_out_kernel, tm=tm, s8=s8),
        out_shape=jax.ShapeDtypeStruct((T, D), F32),
        grid=(T // tm,),
        in_specs=[
            pl.BlockSpec((TOP_K, tm), lambda i: (0, i), memory_space=pltpu.SMEM),
            pl.BlockSpec((tm * s8, LANES), lambda i: (i, 0)),
            pl.BlockSpec((tm, LANES), lambda i: (i, 0)),
            _resident(wsg.shape), _resident(wsu.shape), _resident(wsd.shape),
            pl.BlockSpec((tm, D), lambda i: (i, 0)),
            pl.BlockSpec((1, 1, D), lambda i: (i // bpb, 0, 0)),
            pl.BlockSpec(memory_space=pl.ANY),
        ],
        out_specs=pl.BlockSpec((tm, D), lambda i: (i, 0)),
        scratch_shapes=[pltpu.VMEM((TOP_K, tm * s8, LANES), jnp.uint32), pltpu.SemaphoreType.DMA],
        compiler_params=_cparams(("arbitrary",)),
        name="ffn_out",
    )(dest, hp, w_tok, wsg, wsu, wsd, x2, g2, ys)


def block_plan(counts, *, blk, n_slots):
    padded = ((counts + blk - 1) // blk) * blk
    pends = jnp.cumsum(padded)
    pstarts = (pends - padded).astype(jnp.int32)
    nblk = -(-n_slots // blk) + N_EXPERTS
    first_row = jnp.arange(nblk, dtype=pends.dtype) * blk
    block_e = jnp.sum(pends[None, :] <= first_row[:, None], axis=1)
    block_e = jnp.minimum(block_e, N_EXPERTS - 1).astype(jnp.int32)
    n_used = (pends[-1] // blk).astype(jnp.int32)
    tail = n_used + jnp.arange(N_EXPERTS, dtype=jnp.int32)
    zrow = jnp.concatenate([pends - blk, jnp.minimum(tail, nblk - 1) * blk]).astype(jnp.int32)
    zon = jnp.concatenate([counts > 0, tail < nblk]).astype(jnp.int32)
    return pstarts, block_e, n_used.reshape(1), zrow, zon, nblk


def _layer_weights(l, D, norm1_g, norm2_g, w_in, b_fgt, fox_qn_g, fox_kn_g, diff_qn_g, diff_kn_g,
                   diff_out_g, w_branch, w_out, w_router, w_exp_gate, w_exp_up, w_exp_down,
                   w_sh_gate, w_sh_up, w_sh_down):
    W = ATTN_WIDTH
    fcol = 3 * W
    w = w_in[l]
    dcol = fcol + N_FOX_HEADS
    w_cat = jnp.concatenate([w[:, :2 * W], w[:, dcol:dcol + 2 * W], w[:, dcol + 3 * W:],
                             w[:, 2 * W:fcol], w[:, dcol + 2 * W:dcol + 3 * W]], axis=1).astype(BF16)
    w_ff = jnp.zeros((D, LANES), BF16).at[:, :N_FOX_HEADS].set(
        w[:, fcol:fcol + N_FOX_HEADS].astype(BF16))
    qscale = HEAD_DIM ** -0.5 * LOG2E
    colgain = jnp.concatenate([
        jnp.tile(fox_qn_g[l].astype(F32) * qscale, N_FOX_HEADS),
        jnp.tile(fox_kn_g[l].astype(F32), N_FOX_HEADS),
        jnp.tile(diff_qn_g[l].astype(F32) * qscale, 2 * N_DIFF_HEADS),
        jnp.tile(diff_kn_g[l].astype(F32), 2 * N_DIFF_HEADS),
        jnp.ones((2 * D + 2 * W,), F32),
    ]).reshape(1, -1)
    return dict(
        w_cat=w_cat, w_ff=w_ff, colgain=colgain,
        wb0=w_branch[l, 0].astype(BF16), wb1=w_branch[l, 1].astype(BF16), wo=w_out[l].astype(BF16),
        wr_t=w_router[l].T.astype(BF16),
        wsg=w_sh_gate[l].astype(BF16), wsu=w_sh_up[l].astype(BF16), wsd=w_sh_down[l].astype(BF16),
    )


def kernel(x, c, positions, norm1_g, norm2_g, w_ada, b_ada, w_in, b_fgt, fox_qn_g, fox_kn_g,
           diff_qn_g, diff_kn_g, lam_q1, lam_k1, lam_q2, lam_k2, diff_out_g, w_branch, w_out,
           w_router, b_router, w_exp_gate, w_exp_up, w_exp_down, w_sh_gate, w_sh_up, w_sh_down):
    B, S, D = x.shape
    L = w_ada.shape[0]
    T = B * S
    blk = 256
    cos_t, sin_t = rope_tables(positions)
    mod = adaln(c, w_ada, b_ada)
    x2 = x.reshape(T, D)
    for l in range(L):
        lw = _layer_weights(l, D, norm1_g, norm2_g, w_in, b_fgt, fox_qn_g, fox_kn_g, diff_qn_g,
                            diff_kn_g, diff_out_g, w_branch, w_out, w_router, w_exp_gate,
                            w_exp_up, w_exp_down, w_sh_gate, w_sh_up, w_sh_down)
        sh1, sc1, g1, sh2, sc2, g2 = [mod[l, :, k * D:(k + 1) * D].reshape(B, 1, D) for k in range(6)]
        lam_init = 0.8 - 0.6 * math.exp(-0.3 * l)

        proj, vt, ff = inproj(x2, norm1_g[l].reshape(1, D), sh1, sc1, lw['w_cat'], lw['w_ff'],
                              lw['colgain'], cos_t, sin_t, seq=S)
        k_aug = fgate_keys(ff, b_fgt[l], proj, batch=B, seq=S)
        proj3 = proj.reshape(B, S, -1)
        o_fox = fox_attention(proj3, k_aug, vt).reshape(T, ATTN_WIDTH)
        lam_vecs = [v[l].reshape(1, HEAD_DIM).astype(F32) for v in (lam_q1, lam_k1, lam_q2, lam_k2)]
        o_diff = diff_attention(proj3, vt, lam_vecs, diff_out_g[l].reshape(-1, 1).astype(F32),
                                lam_init=lam_init).reshape(T, ATTN_WIDTH)
        x2 = post_attention(o_fox, o_diff, proj, lw['wb0'], lw['wb1'], lw['wo'], x2, g1, seq=S)

        hp, top_e, w_tok = router(x2, norm2_g[l].reshape(1, D), sh2, sc2, lw['wr_t'], b_router[l], seq=S)
        rank, counts = slot_rank(top_e)
        counts = counts[:, 0]
        pstarts, block_e, n_used, zrow, zon, nblk = block_plan(counts, blk=blk, n_slots=T * TOP_K)
        dest = slot_dest(pstarts, top_e, rank)
        xs = dispatch(hp, dest, zrow, zon, n_rows=nblk * blk, blk=blk)
        ys = experts(xs, block_e, n_used, w_exp_gate, w_exp_up, w_exp_down, layer=l, blk=blk)
        x2 = ffn_out(hp, ys, dest, w_tok, lw['wsg'], lw['wsu'], lw['wsd'], x2, g2, seq=S)
    return x2.reshape(B, S, D)
```

```python
import functools
import math

import jax
import jax.numpy as jnp
import numpy as np
from jax import lax
from jax.experimental import pallas as pl
from jax.experimental.pallas import tpu as pltpu

F32 = jnp.float32
BF16 = jnp.bfloat16

HEAD_DIM = 128
N_FOX_HEADS = 8
N_DIFF_HEADS = 4
ATTN_WIDTH = N_FOX_HEADS * HEAD_DIM
CHUNK = 64
ROT_DIM = HEAD_DIM // 4
ROPE_THETA = 500000.0
N_EXPERTS = 64
N_GROUPS = 8
GROUP_SIZE = N_EXPERTS // N_GROUPS
TOPK_GROUPS = 4
TOP_K = 8
ROUTED_SCALE = 2.5
RMS_EPS = 1e-6
LANES = 128
V7X_VMEM_BYTES = 64 * 1024 * 1024
VMEM_LIMIT = V7X_VMEM_BYTES - 8 * 1024 * 1024

SEC_FQ, SEC_FK, SEC_DQ, SEC_DK, SEC_GATE = range(5)
LOG2E = math.log2(math.e)
BIAS_LANES = 3


def _cparams(sem):
    return pltpu.CompilerParams(dimension_semantics=sem, vmem_limit_bytes=VMEM_LIMIT)


def _rms(y, axis=-1):
    return y * lax.rsqrt(jnp.mean(y * y, axis=axis, keepdims=True) + RMS_EPS)


def _sigmoid(z):
    return 1.0 / (1.0 + jnp.exp(-z))


def _adaln_kernel(c_ref, w_ref, b_ref, o_ref):
    c = c_ref[...]
    cond = (c * _sigmoid(c)).astype(BF16)
    o_ref[0] = jnp.dot(cond, w_ref[0].astype(BF16), preferred_element_type=F32) + b_ref[0]


def adaln(c, w_ada, b_ada, *, tn=1024):
    L, D, N = w_ada.shape
    B = c.shape[0]
    tn = min(tn, N)
    return pl.pallas_call(
        _adaln_kernel,
        out_shape=jax.ShapeDtypeStruct((L, B, N), F32),
        grid=(L, N // tn),
        in_specs=[
            pl.BlockSpec((B, D), lambda l, j: (0, 0)),
            pl.BlockSpec((1, D, tn), lambda l, j: (l, 0, j)),
            pl.BlockSpec((1, 1, tn), lambda l, j: (l, 0, j)),
        ],
        out_specs=pl.BlockSpec((1, B, tn), lambda l, j: (l, 0, j)),
        compiler_params=_cparams(("arbitrary", "arbitrary")),
        name="adaln",
    )(c, w_ada, b_ada.reshape(L, 1, N))


def _rope_kernel(pos_ref, invf_ref, cos_ref, sin_ref):
    ang = pos_ref[...].astype(F32) * invf_ref[...]
    lane = lax.broadcasted_iota(jnp.int32, ang.shape, 1)
    c = jnp.cos(ang)
    s = jnp.sin(ang)
    cos_ref[...] = jnp.where(lane < ROT_DIM, c, 1.0)
    sin_ref[...] = jnp.where(lane < ROT_DIM // 2, -s, jnp.where(lane < ROT_DIM, s, 0.0))


def rope_tables(positions, *, tm=1024):
    T = positions.size
    tm = min(tm, T)
    half = ROT_DIM // 2
    inv_freq = ROPE_THETA ** (-jnp.arange(0, ROT_DIM, 2, dtype=F32) / ROT_DIM)
    invf = jnp.zeros((1, LANES), F32).at[0, :half].set(inv_freq).at[0, half:ROT_DIM].set(inv_freq)
    return pl.pallas_call(
        _rope_kernel,
        out_shape=(jax.ShapeDtypeStruct((T, LANES), F32),) * 2,
        grid=(T // tm,),
        in_specs=[
            pl.BlockSpec((tm, 1), lambda i: (i, 0)),
            pl.BlockSpec((1, LANES), lambda i: (0, 0)),
        ],
        out_specs=(pl.BlockSpec((tm, LANES), lambda i: (i, 0)),) * 2,
        compiler_params=_cparams(("arbitrary",)),
        name="rope_tables",
    )(positions.reshape(T, 1), invf)


def _inproj_kernel(x_ref, g_ref, sh_ref, sc_ref, w_ref, wff_ref, cg_ref, cos_ref, sin_ref,
                   o_ref, vt_ref, ff_ref, h_scr, *, sec_v):
    j = pl.program_id(1)

    @pl.when(j == 0)
    def _():
        y = _rms(x_ref[...]) * g_ref[...]
        h = (y * (1.0 + sc_ref[0]) + sh_ref[0]).astype(BF16)
        h_scr[...] = h
        ff_ref[...] = jnp.dot(h, wff_ref[...], preferred_element_type=F32)

    acc = jnp.dot(h_scr[...], w_ref[...], preferred_element_type=F32)
    n_heads = ATTN_WIDTH // HEAD_DIM

    def head(a, hd):
        return a[:, hd * HEAD_DIM:(hd + 1) * HEAD_DIM]

    is_norm = (j == SEC_FQ) | (j == SEC_FK)
    is_rope = (j == SEC_DQ) | (j == SEC_DK)

    @pl.when(is_norm)
    def _():
        for hd in range(n_heads):
            o_ref[:, hd * HEAD_DIM:(hd + 1) * HEAD_DIM] = (
                _rms(head(acc, hd)) * head(cg_ref[...], hd)).astype(o_ref.dtype)

    @pl.when(is_rope)
    def _():
        cos_t = cos_ref[...]
        sin_t = sin_ref[...]
        lane = lax.broadcasted_iota(jnp.int32, cos_t.shape, 1)
        low = lane < ROT_DIM // 2
        for hd in range(n_heads):
            y = _rms(head(acc, hd)) * head(cg_ref[...], hd)
            rot = jnp.where(low, pltpu.roll(y, HEAD_DIM - ROT_DIM // 2, 1),
                            pltpu.roll(y, ROT_DIM // 2, 1))
            o_ref[:, hd * HEAD_DIM:(hd + 1) * HEAD_DIM] = (y * cos_t + rot * sin_t).astype(o_ref.dtype)

    @pl.when((j >= SEC_GATE) & (j < sec_v))
    def _():
        o_ref[...] = _sigmoid(acc).astype(o_ref.dtype)

    @pl.when(j >= sec_v)
    def _():
        vt_ref[...] = acc.T.astype(vt_ref.dtype)


def inproj(x2, g, shift, scale, w_cat, w_ff, colgain, cos_t, sin_t, *, seq, tm=1024):
    T, D = x2.shape
    NC = w_cat.shape[1]
    tn = ATTN_WIDTH
    tm = min(tm, seq)
    bpb = seq // tm
    sec_v = NC // tn - 2
    return pl.pallas_call(
        functools.partial(_inproj_kernel, sec_v=sec_v),
        out_shape=(jax.ShapeDtypeStruct((T, sec_v * tn), BF16),
                   jax.ShapeDtypeStruct((2 * tn, T), BF16),
                   jax.ShapeDtypeStruct((T, LANES), F32)),
        grid=(T // tm, NC // tn),
        in_specs=[
            pl.BlockSpec((tm, D), lambda i, j: (i, 0)),
            pl.BlockSpec((1, D), lambda i, j: (0, 0)),
            pl.BlockSpec((1, 1, D), lambda i, j: (i // bpb, 0, 0)),
            pl.BlockSpec((1, 1, D), lambda i, j: (i // bpb, 0, 0)),
            pl.BlockSpec((D, tn), lambda i, j: (0, j)),
            pl.BlockSpec((D, LANES), lambda i, j: (0, 0)),
            pl.BlockSpec((1, tn), lambda i, j: (0, j)),
            pl.BlockSpec((tm, LANES), lambda i, j: (i, 0)),
            pl.BlockSpec((tm, LANES), lambda i, j: (i, 0)),
        ],
        out_specs=(
            pl.BlockSpec((tm, tn), lambda i, j: (i, jnp.minimum(j, sec_v - 1))),
            pl.BlockSpec((tn, tm), lambda i, j: (jnp.maximum(j - sec_v, 0), i)),
            pl.BlockSpec((tm, LANES), lambda i, j: (i, 0)),
        ),
        scratch_shapes=[pltpu.VMEM((tm, D), BF16)],
        compiler_params=_cparams(("arbitrary", "arbitrary")),
        name="inproj",
    )(x2, g, shift, scale, w_cat, w_ff, colgain, cos_t, sin_t)


def _split3(v):
    hi = v.astype(BF16)
    r = v - hi.astype(F32)
    mid = r.astype(BF16)
    lo = (r - mid.astype(F32)).astype(BF16)
    return hi, mid, lo


def _fgate_kernel(ff_ref, b_ref, k_ref, o_ref, carry_scr):
    @pl.when(pl.program_id(1) == 0)
    def _():
        carry_scr[...] = jnp.zeros_like(carry_scr)

    z = ff_ref[...] + b_ref[...]
    logf = jnp.minimum(z, 0.0) - jnp.log(1.0 + jnp.exp(-jnp.abs(z)))
    tm = z.shape[0]
    row = lax.broadcasted_iota(jnp.int32, (tm, tm), 0)
    col = lax.broadcasted_iota(jnp.int32, (tm, tm), 1)
    tri = jnp.where(row >= col, 1.0, 0.0).astype(BF16)
    cum = carry_scr[...]
    for part in _split3(logf):
        cum = cum + jnp.dot(tri, part, preferred_element_type=F32)
    carry_scr[...] = cum[tm - 1:tm, :]
    bias = cum * (-LOG2E)
    lane = lax.broadcasted_iota(jnp.int32, (tm, HEAD_DIM), 1)
    for hd in range(N_FOX_HEADS):
        hi, mid, lo = [p.astype(F32) for p in _split3(bias[:, hd:hd + 1])]
        ext = jnp.where(lane == 0, hi, jnp.where(lane == 1, mid, jnp.where(lane == 2, lo, 0.0)))
        o_ref[0, hd, :, :HEAD_DIM] = k_ref[:, hd * HEAD_DIM:(hd + 1) * HEAD_DIM]
        o_ref[0, hd, :, HEAD_DIM:] = ext.astype(o_ref.dtype)


def fgate_keys(ff, b_fgt, proj, *, batch, seq, tm=512):
    tm = min(tm, seq)
    b_pad = jnp.zeros((1, LANES), F32).at[0, :N_FOX_HEADS].set(b_fgt.astype(F32))
    nb = seq // tm
    return pl.pallas_call(
        _fgate_kernel,
        out_shape=jax.ShapeDtypeStruct((batch, N_FOX_HEADS, seq, 2 * HEAD_DIM), BF16),
        grid=(batch, nb),
        in_specs=[
            pl.BlockSpec((tm, LANES), lambda b, i: (b * nb + i, 0)),
            pl.BlockSpec((1, LANES), lambda b, i: (0, 0)),
            pl.BlockSpec((tm, ATTN_WIDTH), lambda b, i: (b * nb + i, SEC_FK)),
        ],
        out_specs=pl.BlockSpec((1, N_FOX_HEADS, tm, 2 * HEAD_DIM), lambda b, i: (b, 0, i, 0)),
        scratch_shapes=[pltpu.VMEM((1, LANES), F32)],
        compiler_params=_cparams(("arbitrary", "arbitrary")),
        name="fgate",
    )(ff, b_pad, proj)


_NT = (((1,), (1,)), ((), ()))
ONES_ROWS = 16


def _flash_step(sT, vT, m_ref, acc_ref):
    m_old = m_ref[...]
    m_new = jnp.maximum(m_old, jnp.max(sT, axis=0, keepdims=True))
    alpha = jnp.exp2(m_old - m_new)
    pT = jnp.exp2(sT - m_new).astype(vT.dtype)
    acc_ref[...] = alpha * acc_ref[...] + jnp.dot(vT, pT, preferred_element_type=F32)
    m_ref[...] = m_new


def _kv_schedule(i, tq, step):
    def pair(jj, carry):
        step(jj * (2 * tq), 2 * tq, False)
        return carry

    lax.fori_loop(0, i // 2, pair, 0)

    @pl.when(i % 2 == 1)
    def _():
        step((i - 1) * tq, tq, False)

    step(i * tq, tq, True)


def _with_ones(vT):
    return jnp.concatenate([vT, jnp.ones((ONES_ROWS, vT.shape[1]), vT.dtype)], axis=0)


def _fox_kernel(q_ref, k_ref, vt_ref, o_ref, m_scr, acc_scr, *, tq):
    i = pl.program_id(2)
    lane = lax.broadcasted_iota(jnp.int32, (tq, HEAD_DIM), 1)
    q = jnp.concatenate([q_ref[0], jnp.where(lane < BIAS_LANES, 1.0, 0.0).astype(q_ref.dtype)], axis=1)
    m_scr[...] = jnp.full_like(m_scr, -jnp.inf)
    acc_scr[...] = jnp.zeros_like(acc_scr)

    def step(ks, n, masked):
        ks = pl.multiple_of(ks, tq)
        k = k_ref[0, 0, pl.ds(ks, n), :]
        sT = lax.dot_general(k, q, _NT, preferred_element_type=F32)
        if masked:
            key = lax.broadcasted_iota(jnp.int32, sT.shape, 0)
            qry = lax.broadcasted_iota(jnp.int32, sT.shape, 1)
            sT = jnp.where(key <= qry, sT, -jnp.inf)
        _flash_step(sT, _with_ones(vt_ref[:, pl.ds(ks, n)]), m_scr, acc_scr)

    _kv_schedule(i, tq, step)
    o = acc_scr[:HEAD_DIM, :] / acc_scr[HEAD_DIM:HEAD_DIM + 1, :]
    o_ref[0] = o.T.astype(o_ref.dtype)


def fox_attention(proj3, k_aug, vt, *, tq=512):
    B, S, _ = proj3.shape
    tq = min(tq, S)
    H = N_FOX_HEADS
    return pl.pallas_call(
        functools.partial(_fox_kernel, tq=tq),
        out_shape=jax.ShapeDtypeStruct((B, S, ATTN_WIDTH), BF16),
        grid=(B, H, S // tq),
        in_specs=[
            pl.BlockSpec((1, tq, HEAD_DIM), lambda b, h, i: (b, i, SEC_FQ * H + h)),
            pl.BlockSpec((1, 1, S, 2 * HEAD_DIM), lambda b, h, i: (b, h, 0, 0)),
            pl.BlockSpec((HEAD_DIM, S), lambda b, h, i: (h, b)),
        ],
        out_specs=pl.BlockSpec((1, tq, HEAD_DIM), lambda b, h, i: (b, i, h)),
        scratch_shapes=[pltpu.VMEM((1, tq), F32), pltpu.VMEM((HEAD_DIM + ONES_ROWS, tq), F32)],
        compiler_params=_cparams(("arbitrary", "arbitrary", "arbitrary")),
        name="fox_attn",
    )(proj3, k_aug, vt)


def _diff_kernel(q_ref, k_ref, vt_ref, lq1_ref, lk1_ref, lq2_ref, lk2_ref, g_ref, o_ref,
                 m1, a1, m2, a2, *, tq, lam_init):
    i = pl.program_id(2)
    dv = 2 * HEAD_DIM
    for m, a in ((m1, a1), (m2, a2)):
        m[...] = jnp.full_like(m, -jnp.inf)
        a[...] = jnp.zeros_like(a)

    def step(ks, n, masked):
        ks = pl.multiple_of(ks, tq)
        vT = _with_ones(vt_ref[:, pl.ds(ks, n)])
        for mp, (m, a) in enumerate(((m1, a1), (m2, a2))):
            q = q_ref[0, :, mp * HEAD_DIM:(mp + 1) * HEAD_DIM]
            k = k_ref[0, pl.ds(ks, n), mp * HEAD_DIM:(mp + 1) * HEAD_DIM]
            sT = lax.dot_general(k, q, _NT, preferred_element_type=F32)
            if masked:
                key = lax.broadcasted_iota(jnp.int32, sT.shape, 0)
                qry = lax.broadcasted_iota(jnp.int32, sT.shape, 1)
                sT = jnp.where(key // CHUNK <= qry // CHUNK, sT, -jnp.inf)
            _flash_step(sT, vT, m, a)

    _kv_schedule(i, tq, step)
    lam = (jnp.exp(jnp.sum(lq1_ref[...] * lk1_ref[...], axis=-1, keepdims=True))
           - jnp.exp(jnp.sum(lq2_ref[...] * lk2_ref[...], axis=-1, keepdims=True)) + lam_init)
    o = a1[:dv, :] / a1[dv:dv + 1, :] - lam * (a2[:dv, :] / a2[dv:dv + 1, :])
    o = _rms(o, axis=0) * g_ref[...] * (1.0 - lam_init)
    o_ref[0] = o.T.astype(o_ref.dtype)


def diff_attention(proj3, vt, lam_vecs, out_gain, *, lam_init, tq=512):
    B, S, _ = proj3.shape
    tq = min(tq, S)
    H = N_DIFF_HEADS
    dv = 2 * HEAD_DIM
    vec = pl.BlockSpec((1, HEAD_DIM), lambda b, h, i: (0, 0))
    return pl.pallas_call(
        functools.partial(_diff_kernel, tq=tq, lam_init=lam_init),
        out_shape=jax.ShapeDtypeStruct((B, S, ATTN_WIDTH), BF16),
        grid=(B, H, S // tq),
        in_specs=[
            pl.BlockSpec((1, tq, dv), lambda b, h, i: (b, i, SEC_DQ * H + h)),
            pl.BlockSpec((1, S, dv), lambda b, h, i: (b, 0, SEC_DK * H + h)),
            pl.BlockSpec((dv, S), lambda b, h, i: (H + h, b)),
            vec, vec, vec, vec,
            pl.BlockSpec((dv, 1), lambda b, h, i: (0, 0)),
        ],
        out_specs=pl.BlockSpec((1, tq, dv), lambda b, h, i: (b, i, h)),
        scratch_shapes=[pltpu.VMEM((1, tq), F32), pltpu.VMEM((dv + ONES_ROWS, tq), F32)] * 2,
        compiler_params=_cparams(("arbitrary", "arbitrary", "arbitrary")),
        name="diff_attn",
    )(proj3, proj3, vt, *lam_vecs, out_gain)


def _post_kernel(of_ref, od_ref, ga_ref, gb_ref, wb0_ref, wb1_ref, wo_ref, x_ref, g1_ref, o_ref):
    y0 = jnp.dot(of_ref[...], wb0_ref[...], preferred_element_type=F32)
    y1 = jnp.dot(od_ref[...], wb1_ref[...], preferred_element_type=F32)
    merged = ga_ref[...].astype(F32) * y0 + gb_ref[...].astype(F32) * y1
    mix = jnp.dot(merged.astype(BF16), wo_ref[...], preferred_element_type=F32)
    o_ref[...] = x_ref[...] + g1_ref[0] * mix


def _resident(shape):
    nd = len(shape)
    return pl.BlockSpec(shape, lambda *_: (0,) * nd, pipeline_mode=pl.Buffered(1))


def post_attention(o_fox, o_diff, proj, wb0, wb1, wo, x2, g1, *, seq, tm=256):
    T, D = x2.shape
    tm = min(tm, seq)
    bpb = seq // tm
    gate_blk = SEC_GATE * ATTN_WIDTH // D
    return pl.pallas_call(
        _post_kernel,
        out_shape=jax.ShapeDtypeStruct((T, D), F32),
        grid=(T // tm,),
        in_specs=[
            pl.BlockSpec((tm, ATTN_WIDTH), lambda i: (i, 0)),
            pl.BlockSpec((tm, ATTN_WIDTH), lambda i: (i, 0)),
            pl.BlockSpec((tm, D), lambda i: (i, gate_blk)),
            pl.BlockSpec((tm, D), lambda i: (i, gate_blk + 1)),
            _resident(wb0.shape), _resident(wb1.shape), _resident(wo.shape),
            pl.BlockSpec((tm, D), lambda i: (i, 0)),
            pl.BlockSpec((1, 1, D), lambda i: (i // bpb, 0, 0)),
        ],
        out_specs=pl.BlockSpec((tm, D), lambda i: (i, 0)),
        compiler_params=_cparams(("arbitrary",)),
        name="post_attn",
    )(o_fox, o_diff, proj, proj, wb0, wb1, wo, x2, g1)


def _pack_halves(y):
    half = y.shape[1] // 2
    bits = lax.bitcast_convert_type(y.astype(BF16).astype(F32), jnp.uint32)
    return (bits[:, half:] & jnp.uint32(0xFFFF0000)) | (bits[:, :half] >> 16)


def _unpack_halves(w):
    lo = lax.bitcast_convert_type(w << 16, F32)
    hi = lax.bitcast_convert_type(w & jnp.uint32(0xFFFF0000), F32)
    return lo, hi


def _store_token_major(ref, words, s8):
    m = words.shape[0]
    for s in range(s8):
        ref[pl.ds(s, m, stride=s8), :] = words[:, s * LANES:(s + 1) * LANES]


def _load_token_major(ref, m, s8):
    parts = [_unpack_halves(ref[pl.ds(s, m, stride=s8), :]) for s in range(s8)]
    return jnp.concatenate([p[0].astype(BF16) for p in parts] + [p[1].astype(BF16) for p in parts],
                           axis=1)


def _router_kernel(x_ref, g_ref, sh_ref, sc_ref, wr_ref, br_ref, h_ref, e_ref, w_ref, *, s8):
    y = _rms(x_ref[...]) * g_ref[...]
    h = y * (1.0 + sc_ref[0]) + sh_ref[0]
    hb = h.astype(BF16)
    _store_token_major(h_ref, _pack_halves(h), s8)
    logits = lax.dot_general(wr_ref[...], hb, _NT, preferred_element_type=F32)
    scores = _sigmoid(logits)
    tm = scores.shape[1]
    sc3 = scores.reshape(N_GROUPS, GROUP_SIZE, tm)
    sel = sc3 + br_ref[...]
    neg = -jnp.inf
    member = lax.broadcasted_iota(jnp.int32, sel.shape, 1)
    group = lax.broadcasted_iota(jnp.int32, sel.shape, 0)

    m1 = jnp.max(sel, axis=1, keepdims=True)
    i1 = jnp.min(jnp.where(sel == m1, member, GROUP_SIZE), axis=1, keepdims=True)
    m2 = jnp.max(jnp.where(member == i1, neg, sel), axis=1, keepdims=True)
    gscore = m1 + m2

    gidx = lax.broadcasted_iota(jnp.int32, gscore.shape, 0)
    gkeep = jnp.zeros(gscore.shape, jnp.int32)
    for _ in range(TOPK_GROUPS):
        best = jnp.max(gscore, axis=0, keepdims=True)
        bi = jnp.min(jnp.where(gscore == best, gidx, N_GROUPS), axis=0, keepdims=True)
        hit = gidx == bi
        gkeep = jnp.where(hit, 1, gkeep)
        gscore = jnp.where(hit, neg, gscore)

    cand = jnp.where(gkeep > 0, sel, neg)
    eidx = group * GROUP_SIZE + member
    weights = []
    for k in range(TOP_K):
        best = jnp.max(jnp.max(cand, axis=1, keepdims=True), axis=0, keepdims=True)
        bi = jnp.min(jnp.min(jnp.where(cand == best, eidx, N_EXPERTS), axis=1, keepdims=True),
                     axis=0, keepdims=True)
        hit = eidx == bi
        e_ref[k:k + 1, :] = bi[0]
        weights.append(jnp.sum(jnp.sum(jnp.where(hit, sc3, 0.0), axis=1, keepdims=True),
                               axis=0, keepdims=True)[0])
        cand = jnp.where(hit, neg, cand)
    wsum = functools.reduce(lambda a, b: a + b, weights)
    row = lax.broadcasted_iota(jnp.int32, (LANES, tm), 0)
    wmat = jnp.zeros((LANES, tm), F32)
    for k in range(TOP_K):
        wmat = jnp.where(row == k, weights[k] / wsum * ROUTED_SCALE, wmat)
    w_ref[...] = wmat.T


def router(x2, g, shift, scale, wr_t, b_router, *, seq, tm=512):
    T, D = x2.shape
    tm = min(tm, seq)
    bpb = seq // tm
    s8 = D // (2 * LANES)
    return pl.pallas_call(
        functools.partial(_router_kernel, s8=s8),
        out_shape=(jax.ShapeDtypeStruct((T * s8, LANES), jnp.uint32),
                   jax.ShapeDtypeStruct((TOP_K, T), jnp.int32),
                   jax.ShapeDtypeStruct((T, LANES), F32)),
        grid=(T // tm,),
        in_specs=[
            pl.BlockSpec((tm, D), lambda i: (i, 0)),
            pl.BlockSpec((1, D), lambda i: (0, 0)),
            pl.BlockSpec((1, 1, D), lambda i: (i // bpb, 0, 0)),
            pl.BlockSpec((1, 1, D), lambda i: (i // bpb, 0, 0)),
            pl.BlockSpec((N_EXPERTS, D), lambda i: (0, 0)),
            pl.BlockSpec((N_GROUPS, GROUP_SIZE, 1), lambda i: (0, 0, 0)),
        ],
        out_specs=(pl.BlockSpec((tm * s8, LANES), lambda i: (i, 0)),
                   pl.BlockSpec((TOP_K, tm), lambda i: (0, i)),
                   pl.BlockSpec((tm, LANES), lambda i: (i, 0))),
        compiler_params=_cparams(("arbitrary",)),
        name="router",
    )(x2, g, shift, scale, wr_t, b_router.reshape(N_GROUPS, GROUP_SIZE, 1).astype(F32))


def _slot_rank_kernel(e_ref, rank_ref, cnt_ref, base_scr, tri_scr):
    tm = e_ref.shape[1]

    @pl.when(pl.program_id(0) == 0)
    def _():
        base_scr[...] = jnp.zeros_like(base_scr)
        r = lax.broadcasted_iota(jnp.int32, (tm, tm), 0)
        c = lax.broadcasted_iota(jnp.int32, (tm, tm), 1)
        tri_scr[...] = jnp.where(r <= c, 1.0, 0.0).astype(BF16)

    expert = lax.broadcasted_iota(jnp.int32, (N_EXPERTS, tm), 0)
    base = base_scr[...]
    for k in range(TOP_K):
        hot = expert == e_ref[k:k + 1, :]
        onehot = jnp.where(hot, 1.0, 0.0)
        incl = jnp.dot(onehot.astype(BF16), tri_scr[...], preferred_element_type=F32)
        rank = jnp.sum(jnp.where(hot, base + incl - 1.0, 0.0), axis=0, keepdims=True)
        rank_ref[k:k + 1, :] = rank.astype(jnp.int32)
        base = base + jnp.sum(onehot, axis=1, keepdims=True)
    base_scr[...] = base
    cnt_ref[...] = jnp.broadcast_to(base, cnt_ref.shape).astype(jnp.int32)


def slot_rank(top_e, *, tm=512):
    K, T = top_e.shape
    tm = min(tm, T)
    return pl.pallas_call(
        _slot_rank_kernel,
        out_shape=(jax.ShapeDtypeStruct((K, T), jnp.int32),
                   jax.ShapeDtypeStruct((N_EXPERTS, LANES), jnp.int32)),
        grid=(T // tm,),
        in_specs=[pl.BlockSpec((K, tm), lambda i: (0, i))],
        out_specs=(pl.BlockSpec((K, tm), lambda i: (0, i)),
                   pl.BlockSpec((N_EXPERTS, LANES), lambda i: (0, 0))),
        scratch_shapes=[pltpu.VMEM((N_EXPERTS, 1), F32), pltpu.VMEM((tm, tm), BF16)],
        compiler_params=_cparams(("arbitrary",)),
        name="slot_rank",
    )(top_e)


def _slot_dest_kernel(pstart_ref, e_ref, rank_ref, dest_ref):
    e = e_ref[...]
    dest = rank_ref[...]
    for x in range(N_EXPERTS):
        dest = dest + jnp.where(e == x, pstart_ref[x], 0)
    dest_ref[...] = dest


def slot_dest(pstarts, top_e, rank, *, tm=4096):
    K, T = top_e.shape
    tm = min(tm, T)
    blockspec = pl.BlockSpec((K, tm), lambda i, ps: (0, i))
    return pl.pallas_call(
        _slot_dest_kernel,
        out_shape=jax.ShapeDtypeStruct((K, T), jnp.int32),
        grid_spec=pltpu.PrefetchScalarGridSpec(
            num_scalar_prefetch=1, grid=(T // tm,),
            in_specs=[blockspec, blockspec], out_specs=blockspec),
        compiler_params=_cparams(("arbitrary",)),
        name="slot_dest",
    )(pstarts, top_e, rank)


def _dispatch_kernel(zrow_ref, zon_ref, dest_ref, h_ref, xs_ref, zero_scr, sem, zsem, *, tm, s8, blk):
    def zero_copy(x):
        row = pl.multiple_of(zrow_ref[x] * s8, 8)
        return pltpu.make_async_copy(zero_scr, xs_ref.at[pl.ds(row, blk * s8)], zsem)

    @pl.when(pl.program_id(0) == 0)
    def _():
        zero_scr[...] = jnp.zeros_like(zero_scr)

        def start(x, c):
            @pl.when(zon_ref[x] > 0)
            def _():
                zero_copy(x).start()
            return c

        def wait(x, c):
            @pl.when(zon_ref[x] > 0)
            def _():
                zero_copy(x).wait()
            return c

        lax.fori_loop(0, 2 * N_EXPERTS, start, 0)
        lax.fori_loop(0, 2 * N_EXPERTS, wait, 0)

    def issue(t, c):
        src = h_ref.at[pl.ds(pl.multiple_of(t * s8, s8), s8)]
        for k in range(TOP_K):
            row = pl.multiple_of(dest_ref[k, t] * s8, s8)
            pltpu.make_async_copy(src, xs_ref.at[pl.ds(row, s8)], sem).start(priority=k % 2)
        return c

    lax.fori_loop(0, tm, issue, 0)
    for _ in range(TOP_K):
        pltpu.make_async_copy(h_ref, xs_ref.at[pl.ds(0, tm * s8)], sem).wait()


def dispatch(hp, dest, zrow, zon, *, n_rows, blk, tm=512):
    K, T = dest.shape
    s8 = hp.shape[0] // T
    tm = min(tm, T)
    return pl.pallas_call(
        functools.partial(_dispatch_kernel, tm=tm, s8=s8, blk=blk),
        out_shape=jax.ShapeDtypeStruct((n_rows * s8, LANES), jnp.uint32),
        grid_spec=pltpu.PrefetchScalarGridSpec(
            num_scalar_prefetch=2, grid=(T // tm,),
            in_specs=[
                pl.BlockSpec((K, tm), lambda i, zr, zc: (0, i), memory_space=pltpu.SMEM),
                pl.BlockSpec((tm * s8, LANES), lambda i, zr, zc: (i, 0)),
            ],
            out_specs=pl.BlockSpec(memory_space=pl.ANY),
            scratch_shapes=[pltpu.VMEM((blk * s8, LANES), jnp.uint32),
                            pltpu.SemaphoreType.DMA, pltpu.SemaphoreType.DMA],
        ),
        compiler_params=_cparams(("arbitrary",)),
        name="dispatch",
    )(zrow, zon, dest, hp)


def _expert_kernel(be_ref, nb_ref, x_ref, wg_ref, wu_ref, wd_ref, o_ref, wg_s, wu_s, wd_s, *, blk, s8):
    b = pl.program_id(0)

    @pl.when((b == 0) | (be_ref[b] != be_ref[jnp.maximum(b - 1, 0)]))
    def _():
        wg_s[...] = wg_ref[0, 0].astype(BF16)
        wu_s[...] = wu_ref[0, 0].astype(BF16)
        wd_s[...] = wd_ref[0, 0].astype(BF16)

    @pl.when(b < nb_ref[0])
    def _():
        x = _load_token_major(x_ref, blk, s8)
        a = jnp.dot(x, wg_s[...], preferred_element_type=F32)
        u = jnp.dot(x, wu_s[...], preferred_element_type=F32)
        hmid = (a * _sigmoid(a) * u).astype(BF16)
        y = jnp.dot(hmid, wd_s[...], preferred_element_type=F32)
        _store_token_major(o_ref, _pack_halves(y), s8)

    @pl.when(b >= nb_ref[0])
    def _():
        o_ref[...] = jnp.zeros_like(o_ref)


def experts(xs, block_e, n_used, wg, wu, wd, *, layer, blk):
    D, E = wg.shape[2], wg.shape[3]
    s8 = D // (2 * LANES)
    nblk = xs.shape[0] // (blk * s8)

    def used(b, nb):
        return jnp.minimum(b, nb[0] - 1)

    def weight(shape):
        return pl.BlockSpec((1, 1) + shape, lambda b, be, nb: (layer, be[used(b, nb)], 0, 0))

    grid_spec = pltpu.PrefetchScalarGridSpec(
        num_scalar_prefetch=2,
        grid=(nblk,),
        in_specs=[
            pl.BlockSpec((blk * s8, LANES), lambda b, be, nb: (used(b, nb), 0)),
            weight((D, E)), weight((D, E)), weight((E, D)),
        ],
        out_specs=pl.BlockSpec((blk * s8, LANES), lambda b, be, nb: (b, 0)),
        scratch_shapes=[pltpu.VMEM((D, E), BF16), pltpu.VMEM((D, E), BF16), pltpu.VMEM((E, D), BF16)],
    )
    return pl.pallas_call(
        functools.partial(_expert_kernel, blk=blk, s8=s8),
        out_shape=jax.ShapeDtypeStruct(xs.shape, jnp.uint32),
        grid_spec=grid_spec,
        compiler_params=_cparams(("arbitrary",)),
        name="experts",
    )(block_e, n_used, xs, wg, wu, wd)


def _ffn_out_kernel(dest_ref, h_ref, wt_ref, wg_ref, wu_ref, wd_ref, x_ref, g2_ref, ys_ref, o_ref,
                    buf, sem, *, tm, s8):
    def issue(t, c):
        for k in range(TOP_K):
            row = pl.multiple_of(dest_ref[k, t] * s8, s8)
            pltpu.make_async_copy(ys_ref.at[pl.ds(row, s8)],
                                  buf.at[k, pl.ds(pl.multiple_of(t * s8, s8), s8)],
                                  sem).start(priority=k % 2)
        return c

    lax.fori_loop(0, tm, issue, 0)

    h = _load_token_major(h_ref, tm, s8)
    a = jnp.dot(h, wg_ref[...], preferred_element_type=F32)
    u = jnp.dot(h, wu_ref[...], preferred_element_type=F32)
    hmid = (a * _sigmoid(a) * u).astype(BF16)
    shared = jnp.dot(hmid, wd_ref[...], preferred_element_type=F32)

    for k in range(TOP_K):
        pltpu.make_async_copy(ys_ref.at[pl.ds(0, tm * s8)], buf.at[k], sem).wait()

    half = s8 * LANES
    wts = [jnp.broadcast_to(wt_ref[:, k:k + 1], (tm, LANES)) for k in range(TOP_K)]
    g2 = g2_ref[0]
    for s in range(s8):
        acc_lo = jnp.zeros((tm, LANES), F32)
        acc_hi = jnp.zeros((tm, LANES), F32)
        for k in range(TOP_K):
            lo, hi = _unpack_halves(buf.at[k][pl.ds(s, tm, stride=s8), :])
            acc_lo = acc_lo + wts[k] * lo
            acc_hi = acc_hi + wts[k] * hi
        for off, acc in ((s * LANES, acc_lo), (half + s * LANES, acc_hi)):
            cols = slice(off, off + LANES)
            o_ref[:, cols] = x_ref[:, cols] + g2[:, cols] * (shared[:, cols] + acc)


def ffn_out(hp, ys, dest, w_tok, wsg, wsu, wsd, x2, g2, *, seq, tm=256):
    T, D = x2.shape
    tm = min(tm, seq)
    bpb = seq // tm
    s8 = D // (2 * LANES)
    return pl.pallas_call(
        functools.partial(_ffn_out_kernel, tm=tm, s8=s8),
        out_shape=jax.ShapeDtypeStruct((T, D), F32),
        grid=(T // tm,),
        in_specs=[
            pl.BlockSpec((TOP_K, tm), lambda i: (0, i), memory_space=pltpu.SMEM),
            pl.BlockSpec((tm * s8, LANES), lambda i: (i, 0)),
            pl.BlockSpec((tm, LANES), lambda i: (i, 0)),
            _resident(wsg.shape), _resident(wsu.shape), _resident(wsd.shape),
            pl.BlockSpec((tm, D), lambda i: (i, 0)),
            pl.BlockSpec((1, 1, D), lambda i: (i // bpb, 0, 0)),
            pl.BlockSpec(memory_space=pl.ANY),
        ],
        out_specs=pl.BlockSpec((tm, D), lambda i: (i, 0)),
        scratch_shapes=[pltpu.VMEM((TOP_K, tm * s8, LANES), jnp.uint32), pltpu.SemaphoreType.DMA],
        compiler_params=_cparams(("arbitrary",)),
        name="ffn_out",
    )(dest, hp, w_tok, wsg, wsu, wsd, x2, g2, ys)


def block_plan(counts, *, blk, n_slots):
    padded = ((counts + blk - 1) // blk) * blk
    pends = jnp.cumsum(padded)
    pstarts = (pends - padded).astype(jnp.int32)
    nblk = -(-n_slots // blk) + N_EXPERTS
    first_row = jnp.arange(nblk, dtype=pends.dtype) * blk
    block_e = jnp.sum(pends[None, :] <= first_row[:, None], axis=1)
    block_e = jnp.minimum(block_e, N_EXPERTS - 1).astype(jnp.int32)
    n_used = (pends[-1] // blk).astype(jnp.int32)
    tail = n_used + jnp.arange(N_EXPERTS, dtype=jnp.int32)
    zrow = jnp.concatenate([pends - blk, jnp.minimum(tail, nblk - 1) * blk]).astype(jnp.int32)
    zon = jnp.concatenate([counts > 0, tail < nblk]).astype(jnp.int32)
    return pstarts, block_e, n_used.reshape(1), zrow, zon, nblk


def _layer_weights(l, D, norm1_g, norm2_g, w_in, b_fgt, fox_qn_g, fox_kn_g, diff_qn_g, diff_kn_g,
                   diff_out_g, w_branch, w_out, w_router, w_exp_gate, w_exp_up, w_exp_down,
                   w_sh_gate, w_sh_up, w_sh_down):
    W = ATTN_WIDTH
    fcol = 3 * W
    w = w_in[l]
    dcol = fcol + N_FOX_HEADS
    w_cat = jnp.concatenate([w[:, :2 * W], w[:, dcol:dcol + 2 * W], w[:, dcol + 3 * W:],
                             w[:, 2 * W:fcol], w[:, dcol + 2 * W:dcol + 3 * W]], axis=1).astype(BF16)
    w_ff = jnp.zeros((D, LANES), BF16).at[:, :N_FOX_HEADS].set(
        w[:, fcol:fcol + N_FOX_HEADS].astype(BF16))
    qscale = HEAD_DIM ** -0.5 * LOG2E
    colgain = jnp.concatenate([
        jnp.tile(fox_qn_g[l].astype(F32) * qscale, N_FOX_HEADS),
        jnp.tile(fox_kn_g[l].astype(F32), N_FOX_HEADS),
        jnp.tile(diff_qn_g[l].astype(F32) * qscale, 2 * N_DIFF_HEADS),
        jnp.tile(diff_kn_g[l].astype(F32), 2 * N_DIFF_HEADS),
        jnp.ones((2 * D + 2 * W,), F32),
    ]).reshape(1, -1)
    return dict(
        w_cat=w_cat, w_ff=w_ff, colgain=colgain,
        wb0=w_branch[l, 0].astype(BF16), wb1=w_branch[l, 1].astype(BF16), wo=w_out[l].astype(BF16),
        wr_t=w_router[l].T.astype(BF16),
        wsg=w_sh_gate[l].astype(BF16), wsu=w_sh_up[l].astype(BF16), wsd=w_sh_down[l].astype(BF16),
    )


def kernel(x, c, positions, norm1_g, norm2_g, w_ada, b_ada, w_in, b_fgt, fox_qn_g, fox_kn_g,
           diff_qn_g, diff_kn_g, lam_q1, lam_k1, lam_q2, lam_k2, diff_out_g, w_branch, w_out,
           w_router, b_router, w_exp_gate, w_exp_up, w_exp_down, w_sh_gate, w_sh_up, w_sh_down):
    B, S, D = x.shape
    L = w_ada.shape[0]
    T = B * S
    blk = 512
    cos_t, sin_t = rope_tables(positions)
    mod = adaln(c, w_ada, b_ada)
    x2 = x.reshape(T, D)
    for l in range(L):
        lw = _layer_weights(l, D, norm1_g, norm2_g, w_in, b_fgt, fox_qn_g, fox_kn_g, diff_qn_g,
                            diff_kn_g, diff_out_g, w_branch, w_out, w_router, w_exp_gate,
                            w_exp_up, w_exp_down, w_sh_gate, w_sh_up, w_sh_down)
        sh1, sc1, g1, sh2, sc2, g2 = [mod[l, :, k * D:(k + 1) * D].reshape(B, 1, D) for k in range(6)]
        lam_init = 0.8 - 0.6 * math.exp(-0.3 * l)

        proj, vt, ff = inproj(x2, norm1_g[l].reshape(1, D), sh1, sc1, lw['w_cat'], lw['w_ff'],
                              lw['colgain'], cos_t, sin_t, seq=S)
        k_aug = fgate_keys(ff, b_fgt[l], proj, batch=B, seq=S)
        proj3 = proj.reshape(B, S, -1)
        o_fox = fox_attention(proj3, k_aug, vt).reshape(T, ATTN_WIDTH)
        lam_vecs = [v[l].reshape(1, HEAD_DIM).astype(F32) for v in (lam_q1, lam_k1, lam_q2, lam_k2)]
        o_diff = diff_attention(proj3, vt, lam_vecs, diff_out_g[l].reshape(-1, 1).astype(F32),
                                lam_init=lam_init).reshape(T, ATTN_WIDTH)
        x2 = post_attention(o_fox, o_diff, proj, lw['wb0'], lw['wb1'], lw['wo'], x2, g1, seq=S)

        hp, top_e, w_tok = router(x2, norm2_g[l].reshape(1, D), sh2, sc2, lw['wr_t'], b_router[l], seq=S)
        rank, counts = slot_rank(top_e)
        counts = counts[:, 0]
        pstarts, block_e, n_used, zrow, zon, nblk = block_plan(counts, blk=blk, n_slots=T * TOP_K)
        dest = slot_dest(pstarts, top_e, rank)
        xs = dispatch(hp, dest, zrow, zon, n_rows=nblk * blk, blk=blk)
        ys = experts(xs, block_e, n_used, w_exp_gate, w_exp_up, w_exp_down, layer=l, blk=blk)
        x2 = ffn_out(hp, ys, dest, w_tok, lw['wsg'], lw['wsu'], lw['wsd'], x2, g2, seq=S)
    return x2.reshape(B, S, D)
```

```python
import functools
import math

import jax
import jax.numpy as jnp
import numpy as np
from jax import lax
from jax.experimental import pallas as pl
from jax.experimental.pallas import tpu as pltpu

F32 = jnp.float32
BF16 = jnp.bfloat16

HEAD_DIM = 128
N_FOX_HEADS = 8
N_DIFF_HEADS = 4
ATTN_WIDTH = N_FOX_HEADS * HEAD_DIM
CHUNK = 64
ROT_DIM = HEAD_DIM // 4
ROPE_THETA = 500000.0
N_EXPERTS = 64
N_GROUPS = 8
GROUP_SIZE = N_EXPERTS // N_GROUPS
TOPK_GROUPS = 4
TOP_K = 8
ROUTED_SCALE = 2.5
RMS_EPS = 1e-6
LANES = 128
V7X_VMEM_BYTES = 64 * 1024 * 1024
VMEM_LIMIT = V7X_VMEM_BYTES - 8 * 1024 * 1024

SEC_FQ, SEC_FK, SEC_DQ, SEC_DK, SEC_GATE = range(5)
LOG2E = math.log2(math.e)
BIAS_LANES = 3


def _cparams(sem):
    return pltpu.CompilerParams(dimension_semantics=sem, vmem_limit_bytes=VMEM_LIMIT)


def _rms(y, axis=-1):
    return y * lax.rsqrt(jnp.mean(y * y, axis=axis, keepdims=True) + RMS_EPS)


def _sigmoid(z):
    return 1.0 / (1.0 + jnp.exp(-z))


def _adaln_kernel(c_ref, w_ref, b_ref, o_ref):
    c = c_ref[...]
    cond = (c * _sigmoid(c)).astype(BF16)
    o_ref[0] = jnp.dot(cond, w_ref[0].astype(BF16), preferred_element_type=F32) + b_ref[0]


def adaln(c, w_ada, b_ada, *, tn=1024):
    L, D, N = w_ada.shape
    B = c.shape[0]
    tn = min(tn, N)
    return pl.pallas_call(
        _adaln_kernel,
        out_shape=jax.ShapeDtypeStruct((L, B, N), F32),
        grid=(L, N // tn),
        in_specs=[
            pl.BlockSpec((B, D), lambda l, j: (0, 0)),
            pl.BlockSpec((1, D, tn), lambda l, j: (l, 0, j)),
            pl.BlockSpec((1, 1, tn), lambda l, j: (l, 0, j)),
        ],
        out_specs=pl.BlockSpec((1, B, tn), lambda l, j: (l, 0, j)),
        compiler_params=_cparams(("arbitrary", "arbitrary")),
        name="adaln",
    )(c, w_ada, b_ada.reshape(L, 1, N))


def _rope_kernel(pos_ref, invf_ref, cos_ref, sin_ref):
    ang = pos_ref[...].astype(F32) * invf_ref[...]
    lane = lax.broadcasted_iota(jnp.int32, ang.shape, 1)
    c = jnp.cos(ang)
    s = jnp.sin(ang)
    cos_ref[...] = jnp.where(lane < ROT_DIM, c, 1.0)
    sin_ref[...] = jnp.where(lane < ROT_DIM // 2, -s, jnp.where(lane < ROT_DIM, s, 0.0))


def rope_tables(positions, *, tm=1024):
    T = positions.size
    tm = min(tm, T)
    half = ROT_DIM // 2
    inv_freq = ROPE_THETA ** (-jnp.arange(0, ROT_DIM, 2, dtype=F32) / ROT_DIM)
    invf = jnp.zeros((1, LANES), F32).at[0, :half].set(inv_freq).at[0, half:ROT_DIM].set(inv_freq)
    return pl.pallas_call(
        _rope_kernel,
        out_shape=(jax.ShapeDtypeStruct((T, LANES), F32),) * 2,
        grid=(T // tm,),
        in_specs=[
            pl.BlockSpec((tm, 1), lambda i: (i, 0)),
            pl.BlockSpec((1, LANES), lambda i: (0, 0)),
        ],
        out_specs=(pl.BlockSpec((tm, LANES), lambda i: (i, 0)),) * 2,
        compiler_params=_cparams(("arbitrary",)),
        name="rope_tables",
    )(positions.reshape(T, 1), invf)


def _inproj_kernel(x_ref, g_ref, sh_ref, sc_ref, w_ref, wff_ref, cg_ref, cos_ref, sin_ref,
                   o_ref, vt_ref, ff_ref, h_scr, *, sec_v):
    j = pl.program_id(1)

    @pl.when(j == 0)
    def _():
        y = _rms(x_ref[...]) * g_ref[...]
        h = (y * (1.0 + sc_ref[0]) + sh_ref[0]).astype(BF16)
        h_scr[...] = h
        ff_ref[...] = jnp.dot(h, wff_ref[...], preferred_element_type=F32)

    W = ATTN_WIDTH
    n_heads = W // HEAD_DIM

    def dots():
        h = h_scr[...]
        return [jnp.dot(h, w_ref[:, s * W:(s + 1) * W], preferred_element_type=F32) for s in range(2)]

    def normed_heads(acc, s):
        for hd in range(n_heads):
            cols = slice(s * W + hd * HEAD_DIM, s * W + (hd + 1) * HEAD_DIM)
            yield cols, _rms(acc[:, hd * HEAD_DIM:(hd + 1) * HEAD_DIM]) * cg_ref[:, cols]

    @pl.when(j == 0)
    def _():
        for s, acc in enumerate(dots()):
            for cols, y in normed_heads(acc, s):
                o_ref[:, cols] = y.astype(o_ref.dtype)

    @pl.when(j == 1)
    def _():
        cos_t = cos_ref[...]
        sin_t = sin_ref[...]
        low = lax.broadcasted_iota(jnp.int32, cos_t.shape, 1) < ROT_DIM // 2
        for s, acc in enumerate(dots()):
            for cols, y in normed_heads(acc, s):
                rot = jnp.where(low, pltpu.roll(y, HEAD_DIM - ROT_DIM // 2, 1),
                                pltpu.roll(y, ROT_DIM // 2, 1))
                o_ref[:, cols] = (y * cos_t + rot * sin_t).astype(o_ref.dtype)

    @pl.when((j >= 2) & (j < sec_v))
    def _():
        for s, acc in enumerate(dots()):
            o_ref[:, s * W:(s + 1) * W] = _sigmoid(acc).astype(o_ref.dtype)

    @pl.when(j >= sec_v)
    def _():
        for s, acc in enumerate(dots()):
            vt_ref[s * W:(s + 1) * W, :] = acc.T.astype(vt_ref.dtype)


def inproj(x2, g, shift, scale, w_cat, w_ff, colgain, cos_t, sin_t, *, seq, tm=512):
    T, D = x2.shape
    NC = w_cat.shape[1]
    tn = 2 * ATTN_WIDTH
    tm = min(tm, seq)
    bpb = seq // tm
    sec_v = NC // tn - 1
    return pl.pallas_call(
        functools.partial(_inproj_kernel, sec_v=sec_v),
        out_shape=(jax.ShapeDtypeStruct((T, sec_v * tn), BF16),
                   jax.ShapeDtypeStruct((tn, T), BF16),
                   jax.ShapeDtypeStruct((T, LANES), F32)),
        grid=(T // tm, NC // tn),
        in_specs=[
            pl.BlockSpec((tm, D), lambda i, j: (i, 0)),
            pl.BlockSpec((1, D), lambda i, j: (0, 0)),
            pl.BlockSpec((1, 1, D), lambda i, j: (i // bpb, 0, 0)),
            pl.BlockSpec((1, 1, D), lambda i, j: (i // bpb, 0, 0)),
            pl.BlockSpec((D, tn), lambda i, j: (0, j)),
            pl.BlockSpec((D, LANES), lambda i, j: (0, 0)),
            pl.BlockSpec((1, tn), lambda i, j: (0, j)),
            pl.BlockSpec((tm, LANES), lambda i, j: (i, 0)),
            pl.BlockSpec((tm, LANES), lambda i, j: (i, 0)),
        ],
        out_specs=(
            pl.BlockSpec((tm, tn), lambda i, j: (i, jnp.minimum(j, sec_v - 1))),
            pl.BlockSpec((tn, tm), lambda i, j: (0, i)),
            pl.BlockSpec((tm, LANES), lambda i, j: (i, 0)),
        ),
        scratch_shapes=[pltpu.VMEM((tm, D), BF16)],
        compiler_params=_cparams(("arbitrary", "arbitrary")),
        name="inproj",
    )(x2, g, shift, scale, w_cat, w_ff, colgain, cos_t, sin_t)


def _split3(v):
    hi = v.astype(BF16)
    r = v - hi.astype(F32)
    mid = r.astype(BF16)
    lo = (r - mid.astype(F32)).astype(BF16)
    return hi, mid, lo


def _fgate_kernel(ff_ref, b_ref, k_ref, o_ref, carry_scr):
    @pl.when(pl.program_id(1) == 0)
    def _():
        carry_scr[...] = jnp.zeros_like(carry_scr)

    z = ff_ref[...] + b_ref[...]
    logf = jnp.minimum(z, 0.0) - jnp.log(1.0 + jnp.exp(-jnp.abs(z)))
    tm = z.shape[0]
    row = lax.broadcasted_iota(jnp.int32, (tm, tm), 0)
    col = lax.broadcasted_iota(jnp.int32, (tm, tm), 1)
    tri = jnp.where(row >= col, 1.0, 0.0).astype(BF16)
    cum = carry_scr[...]
    for part in _split3(logf):
        cum = cum + jnp.dot(tri, part, preferred_element_type=F32)
    carry_scr[...] = cum[tm - 1:tm, :]
    bias = cum * (-LOG2E)
    lane = lax.broadcasted_iota(jnp.int32, (tm, HEAD_DIM), 1)
    for hd in range(N_FOX_HEADS):
        hi, mid, lo = [p.astype(F32) for p in _split3(bias[:, hd:hd + 1])]
        ext = jnp.where(lane == 0, hi, jnp.where(lane == 1, mid, jnp.where(lane == 2, lo, 0.0)))
        o_ref[0, hd, :, :HEAD_DIM] = k_ref[:, hd * HEAD_DIM:(hd + 1) * HEAD_DIM]
        o_ref[0, hd, :, HEAD_DIM:] = ext.astype(o_ref.dtype)


def fgate_keys(ff, b_fgt, proj, *, batch, seq, tm=512):
    tm = min(tm, seq)
    b_pad = jnp.zeros((1, LANES), F32).at[0, :N_FOX_HEADS].set(b_fgt.astype(F32))
    nb = seq // tm
    return pl.pallas_call(
        _fgate_kernel,
        out_shape=jax.ShapeDtypeStruct((batch, N_FOX_HEADS, seq, 2 * HEAD_DIM), BF16),
        grid=(batch, nb),
        in_specs=[
            pl.BlockSpec((tm, LANES), lambda b, i: (b * nb + i, 0)),
            pl.BlockSpec((1, LANES), lambda b, i: (0, 0)),
            pl.BlockSpec((tm, ATTN_WIDTH), lambda b, i: (b * nb + i, SEC_FK)),
        ],
        out_specs=pl.BlockSpec((1, N_FOX_HEADS, tm, 2 * HEAD_DIM), lambda b, i: (b, 0, i, 0)),
        scratch_shapes=[pltpu.VMEM((1, LANES), F32)],
        compiler_params=_cparams(("arbitrary", "arbitrary")),
        name="fgate",
    )(ff, b_pad, proj)


_NT = (((1,), (1,)), ((), ()))
ONES_ROWS = 16


def _flash_step(sT, vT, m_ref, acc_ref):
    m_old = m_ref[...]
    m_new = jnp.maximum(m_old, jnp.max(sT, axis=0, keepdims=True))
    alpha = jnp.exp2(m_old - m_new)
    pT = jnp.exp2(sT - m_new).astype(vT.dtype)
    acc_ref[...] = alpha * acc_ref[...] + jnp.dot(vT, pT, preferred_element_type=F32)
    m_ref[...] = m_new


def _kv_schedule(i, qk, spv):
    @pl.when(i == 0)
    def _():
        qk(0, True, 0)
        spv(0, 0)

    @pl.when(i > 0)
    def _():
        qk(0, False, 0)

        def pair(jj, carry):
            qk(2 * jj + 1, False, 1)
            spv(0, 2 * jj)
            qk(2 * jj + 2, False, 0)
            spv(1, 2 * jj + 1)
            return carry

        lax.fori_loop(0, (i - 1) // 2, pair, 0)

        @pl.when(i % 2 == 1)
        def _():
            qk(i, True, 1)
            spv(0, i - 1)
            spv(1, i)

        @pl.when(i % 2 == 0)
        def _():
            qk(i - 1, False, 1)
            spv(0, i - 2)
            qk(i, True, 0)
            spv(1, i - 1)
            spv(0, i)


def _with_ones(vT):
    return jnp.concatenate([vT, jnp.ones((ONES_ROWS, vT.shape[1]), vT.dtype)], axis=0)


def _fox_kernel(q_ref, k_ref, vt_ref, o_ref, m_scr, acc_scr, s0, s1, *, tq):
    i = pl.program_id(2)
    lane = lax.broadcasted_iota(jnp.int32, (tq, HEAD_DIM), 1)
    q = jnp.concatenate([q_ref[0], jnp.where(lane < BIAS_LANES, 1.0, 0.0).astype(q_ref.dtype)], axis=1)
    m_scr[...] = jnp.full_like(m_scr, -jnp.inf)
    acc_scr[...] = jnp.zeros_like(acc_scr)
    bufs = (s0, s1)

    def qk(j, masked, slot):
        k = k_ref[0, 0, pl.ds(pl.multiple_of(j * tq, tq), tq), :]
        sT = lax.dot_general(k, q, _NT, preferred_element_type=F32)
        if masked:
            key = lax.broadcasted_iota(jnp.int32, sT.shape, 0)
            qry = lax.broadcasted_iota(jnp.int32, sT.shape, 1)
            sT = jnp.where(key <= qry, sT, -jnp.inf)
        bufs[slot][...] = sT

    def spv(slot, j):
        vT = _with_ones(vt_ref[:, pl.ds(pl.multiple_of(j * tq, tq), tq)])
        _flash_step(bufs[slot][...], vT, m_scr, acc_scr)

    _kv_schedule(i, qk, spv)
    o = acc_scr[:HEAD_DIM, :] / acc_scr[HEAD_DIM:HEAD_DIM + 1, :]
    o_ref[0] = o.T.astype(o_ref.dtype)


def fox_attention(proj3, k_aug, vt, *, tq=512):
    B, S, _ = proj3.shape
    tq = min(tq, S)
    H = N_FOX_HEADS
    return pl.pallas_call(
        functools.partial(_fox_kernel, tq=tq),
        out_shape=jax.ShapeDtypeStruct((B, S, ATTN_WIDTH), BF16),
        grid=(B, H, S // tq),
        in_specs=[
            pl.BlockSpec((1, tq, HEAD_DIM), lambda b, h, i: (b, i, SEC_FQ * H + h)),
            pl.BlockSpec((1, 1, S, 2 * HEAD_DIM), lambda b, h, i: (b, h, 0, 0)),
            pl.BlockSpec((HEAD_DIM, S), lambda b, h, i: (h, b)),
        ],
        out_specs=pl.BlockSpec((1, tq, HEAD_DIM), lambda b, h, i: (b, i, h)),
        scratch_shapes=[pltpu.VMEM((1, tq), F32), pltpu.VMEM((HEAD_DIM + ONES_ROWS, tq), F32),
                        pltpu.VMEM((tq, tq), F32), pltpu.VMEM((tq, tq), F32)],
        compiler_params=_cparams(("arbitrary", "arbitrary", "arbitrary")),
        name="fox_attn",
    )(proj3, k_aug, vt)


def _diff_kernel(q_ref, k_ref, vt_ref, lq1_ref, lk1_ref, lq2_ref, lk2_ref, g_ref, o_ref,
                 m1, a1, m2, a2, s10, s11, s20, s21, *, tq, lam_init):
    i = pl.program_id(2)
    dv = 2 * HEAD_DIM
    for m, a in ((m1, a1), (m2, a2)):
        m[...] = jnp.full_like(m, -jnp.inf)
        a[...] = jnp.zeros_like(a)
    bufs = ((s10, s11), (s20, s21))

    def qk(j, masked, slot):
        ks = pl.multiple_of(j * tq, tq)
        for mp in range(2):
            q = q_ref[0, :, mp * HEAD_DIM:(mp + 1) * HEAD_DIM]
            k = k_ref[0, pl.ds(ks, tq), mp * HEAD_DIM:(mp + 1) * HEAD_DIM]
            sT = lax.dot_general(k, q, _NT, preferred_element_type=F32)
            if masked:
                key = lax.broadcasted_iota(jnp.int32, sT.shape, 0)
                qry = lax.broadcasted_iota(jnp.int32, sT.shape, 1)
                sT = jnp.where(key // CHUNK <= qry // CHUNK, sT, -jnp.inf)
            bufs[mp][slot][...] = sT

    def spv(slot, j):
        vT = _with_ones(vt_ref[:, pl.ds(pl.multiple_of(j * tq, tq), tq)])
        for mp, (m, a) in enumerate(((m1, a1), (m2, a2))):
            _flash_step(bufs[mp][slot][...], vT, m, a)

    _kv_schedule(i, qk, spv)
    lam = (jnp.exp(jnp.sum(lq1_ref[...] * lk1_ref[...], axis=-1, keepdims=True))
           - jnp.exp(jnp.sum(lq2_ref[...] * lk2_ref[...], axis=-1, keepdims=True)) + lam_init)
    o = a1[:dv, :] / a1[dv:dv + 1, :] - lam * (a2[:dv, :] / a2[dv:dv + 1, :])
    o = _rms(o, axis=0) * g_ref[...] * (1.0 - lam_init)
    o_ref[0] = o.T.astype(o_ref.dtype)


def diff_attention(proj3, vt, lam_vecs, out_gain, *, lam_init, tq=512):
    B, S, _ = proj3.shape
    tq = min(tq, S)
    H = N_DIFF_HEADS
    dv = 2 * HEAD_DIM
    vec = pl.BlockSpec((1, HEAD_DIM), lambda b, h, i: (0, 0))
    return pl.pallas_call(
        functools.partial(_diff_kernel, tq=tq, lam_init=lam_init),
        out_shape=jax.ShapeDtypeStruct((B, S, ATTN_WIDTH), BF16),
        grid=(B, H, S // tq),
        in_specs=[
            pl.BlockSpec((1, tq, dv), lambda b, h, i: (b, i, SEC_DQ * H + h)),
            pl.BlockSpec((1, S, dv), lambda b, h, i: (b, 0, SEC_DK * H + h)),
            pl.BlockSpec((dv, S), lambda b, h, i: (H + h, b)),
            vec, vec, vec, vec,
            pl.BlockSpec((dv, 1), lambda b, h, i: (0, 0)),
        ],
        out_specs=pl.BlockSpec((1, tq, dv), lambda b, h, i: (b, i, h)),
        scratch_shapes=([pltpu.VMEM((1, tq), F32), pltpu.VMEM((dv + ONES_ROWS, tq), F32)] * 2
                        + [pltpu.VMEM((tq, tq), F32)] * 4),
        compiler_params=_cparams(("arbitrary", "arbitrary", "arbitrary")),
        name="diff_attn",
    )(proj3, proj3, vt, *lam_vecs, out_gain)


def _post_kernel(of_ref, od_ref, ga_ref, gb_ref, wb0_ref, wb1_ref, wo_ref, x_ref, g1_ref, o_ref):
    y0 = jnp.dot(of_ref[...], wb0_ref[...], preferred_element_type=F32)
    y1 = jnp.dot(od_ref[...], wb1_ref[...], preferred_element_type=F32)
    merged = ga_ref[...].astype(F32) * y0 + gb_ref[...].astype(F32) * y1
    mix = jnp.dot(merged.astype(BF16), wo_ref[...], preferred_element_type=F32)
    o_ref[...] = x_ref[...] + g1_ref[0] * mix


def _resident(shape):
    nd = len(shape)
    return pl.BlockSpec(shape, lambda *_: (0,) * nd, pipeline_mode=pl.Buffered(1))


def post_attention(o_fox, o_diff, proj, wb0, wb1, wo, x2, g1, *, seq, tm=256):
    T, D = x2.shape
    tm = min(tm, seq)
    bpb = seq // tm
    gate_blk = SEC_GATE * ATTN_WIDTH // D
    return pl.pallas_call(
        _post_kernel,
        out_shape=jax.ShapeDtypeStruct((T, D), F32),
        grid=(T // tm,),
        in_specs=[
            pl.BlockSpec((tm, ATTN_WIDTH), lambda i: (i, 0)),
            pl.BlockSpec((tm, ATTN_WIDTH), lambda i: (i, 0)),
            pl.BlockSpec((tm, D), lambda i: (i, gate_blk)),
            pl.BlockSpec((tm, D), lambda i: (i, gate_blk + 1)),
            _resident(wb0.shape), _resident(wb1.shape), _resident(wo.shape),
            pl.BlockSpec((tm, D), lambda i: (i, 0)),
            pl.BlockSpec((1, 1, D), lambda i: (i // bpb, 0, 0)),
        ],
        out_specs=pl.BlockSpec((tm, D), lambda i: (i, 0)),
        compiler_params=_cparams(("arbitrary",)),
        name="post_attn",
    )(o_fox, o_diff, proj, proj, wb0, wb1, wo, x2, g1)


def _pack_halves(y):
    half = y.shape[1] // 2
    bits = lax.bitcast_convert_type(y.astype(BF16).astype(F32), jnp.uint32)
    return (bits[:, half:] & jnp.uint32(0xFFFF0000)) | (bits[:, :half] >> 16)


def _unpack_halves(w):
    lo = lax.bitcast_convert_type(w << 16, F32)
    hi = lax.bitcast_convert_type(w & jnp.uint32(0xFFFF0000), F32)
    return lo, hi


def _store_token_major(ref, words, s8):
    m = words.shape[0]
    for s in range(s8):
        ref[pl.ds(s, m, stride=s8), :] = words[:, s * LANES:(s + 1) * LANES]


def _load_token_major(ref, m, s8):
    parts = [_unpack_halves(ref[pl.ds(s, m, stride=s8), :]) for s in range(s8)]
    return jnp.concatenate([p[0].astype(BF16) for p in parts] + [p[1].astype(BF16) for p in parts],
                           axis=1)


def _router_kernel(x_ref, g_ref, sh_ref, sc_ref, wr_ref, br_ref, h_ref, e_ref, w_ref, *, s8):
    y = _rms(x_ref[...]) * g_ref[...]
    h = y * (1.0 + sc_ref[0]) + sh_ref[0]
    hb = h.astype(BF16)
    _store_token_major(h_ref, _pack_halves(h), s8)
    logits = lax.dot_general(wr_ref[...], hb, _NT, preferred_element_type=F32)
    scores = _sigmoid(logits)
    tm = scores.shape[1]
    sc3 = scores.reshape(N_GROUPS, GROUP_SIZE, tm)
    sel = sc3 + br_ref[...]
    neg = -jnp.inf
    member = lax.broadcasted_iota(jnp.int32, sel.shape, 1)
    group = lax.broadcasted_iota(jnp.int32, sel.shape, 0)

    m1 = jnp.max(sel, axis=1, keepdims=True)
    i1 = jnp.min(jnp.where(sel == m1, member, GROUP_SIZE), axis=1, keepdims=True)
    m2 = jnp.max(jnp.where(member == i1, neg, sel), axis=1, keepdims=True)
    gscore = m1 + m2

    gidx = lax.broadcasted_iota(jnp.int32, gscore.shape, 0)
    gkeep = jnp.zeros(gscore.shape, jnp.int32)
    for _ in range(TOPK_GROUPS):
        best = jnp.max(gscore, axis=0, keepdims=True)
        bi = jnp.min(jnp.where(gscore == best, gidx, N_GROUPS), axis=0, keepdims=True)
        hit = gidx == bi
        gkeep = jnp.where(hit, 1, gkeep)
        gscore = jnp.where(hit, neg, gscore)

    cand = jnp.where(gkeep > 0, sel, neg)
    eidx = group * GROUP_SIZE + member
    weights = []
    for k in range(TOP_K):
        best = jnp.max(jnp.max(cand, axis=1, keepdims=True), axis=0, keepdims=True)
        bi = jnp.min(jnp.min(jnp.where(cand == best, eidx, N_EXPERTS), axis=1, keepdims=True),
                     axis=0, keepdims=True)
        hit = eidx == bi
        e_ref[k:k + 1, :] = bi[0]
        weights.append(jnp.sum(jnp.sum(jnp.where(hit, sc3, 0.0), axis=1, keepdims=True),
                               axis=0, keepdims=True)[0])
        cand = jnp.where(hit, neg, cand)
    wsum = functools.reduce(lambda a, b: a + b, weights)
    row = lax.broadcasted_iota(jnp.int32, (LANES, tm), 0)
    wmat = jnp.zeros((LANES, tm), F32)
    for k in range(TOP_K):
        wmat = jnp.where(row == k, weights[k] / wsum * ROUTED_SCALE, wmat)
    w_ref[...] = wmat.T


def router(x2, g, shift, scale, wr_t, b_router, *, seq, tm=512):
    T, D = x2.shape
    tm = min(tm, seq)
    bpb = seq // tm
    s8 = D // (2 * LANES)
    return pl.pallas_call(
        functools.partial(_router_kernel, s8=s8),
        out_shape=(jax.ShapeDtypeStruct((T * s8, LANES), jnp.uint32),
                   jax.ShapeDtypeStruct((TOP_K, T), jnp.int32),
                   jax.ShapeDtypeStruct((T, LANES), F32)),
        grid=(T // tm,),
        in_specs=[
            pl.BlockSpec((tm, D), lambda i: (i, 0)),
            pl.BlockSpec((1, D), lambda i: (0, 0)),
            pl.BlockSpec((1, 1, D), lambda i: (i // bpb, 0, 0)),
            pl.BlockSpec((1, 1, D), lambda i: (i // bpb, 0, 0)),
            pl.BlockSpec((N_EXPERTS, D), lambda i: (0, 0)),
            pl.BlockSpec((N_GROUPS, GROUP_SIZE, 1), lambda i: (0, 0, 0)),
        ],
        out_specs=(pl.BlockSpec((tm * s8, LANES), lambda i: (i, 0)),
                   pl.BlockSpec((TOP_K, tm), lambda i: (0, i)),
                   pl.BlockSpec((tm, LANES), lambda i: (i, 0))),
        compiler_params=_cparams(("arbitrary",)),
        name="router",
    )(x2, g, shift, scale, wr_t, b_router.reshape(N_GROUPS, GROUP_SIZE, 1).astype(F32))


def _slot_rank_kernel(e_ref, rank_ref, cnt_ref, base_scr, tri_scr):
    tm = e_ref.shape[1]

    @pl.when(pl.program_id(0) == 0)
    def _():
        base_scr[...] = jnp.zeros_like(base_scr)
        r = lax.broadcasted_iota(jnp.int32, (tm, tm), 0)
        c = lax.broadcasted_iota(jnp.int32, (tm, tm), 1)
        tri_scr[...] = jnp.where(r <= c, 1.0, 0.0).astype(BF16)

    expert = lax.broadcasted_iota(jnp.int32, (N_EXPERTS, tm), 0)
    base = base_scr[...]
    for k in range(TOP_K):
        hot = expert == e_ref[k:k + 1, :]
        onehot = jnp.where(hot, 1.0, 0.0)
        incl = jnp.dot(onehot.astype(BF16), tri_scr[...], preferred_element_type=F32)
        rank = jnp.sum(jnp.where(hot, base + incl - 1.0, 0.0), axis=0, keepdims=True)
        rank_ref[k:k + 1, :] = rank.astype(jnp.int32)
        base = base + jnp.sum(onehot, axis=1, keepdims=True)
    base_scr[...] = base
    cnt_ref[...] = jnp.broadcast_to(base, cnt_ref.shape).astype(jnp.int32)


def slot_rank(top_e, *, tm=512):
    K, T = top_e.shape
    tm = min(tm, T)
    return pl.pallas_call(
        _slot_rank_kernel,
        out_shape=(jax.ShapeDtypeStruct((K, T), jnp.int32),
                   jax.ShapeDtypeStruct((N_EXPERTS, LANES), jnp.int32)),
        grid=(T // tm,),
        in_specs=[pl.BlockSpec((K, tm), lambda i: (0, i))],
        out_specs=(pl.BlockSpec((K, tm), lambda i: (0, i)),
                   pl.BlockSpec((N_EXPERTS, LANES), lambda i: (0, 0))),
        scratch_shapes=[pltpu.VMEM((N_EXPERTS, 1), F32), pltpu.VMEM((tm, tm), BF16)],
        compiler_params=_cparams(("arbitrary",)),
        name="slot_rank",
    )(top_e)


def _slot_dest_kernel(pstart_ref, e_ref, rank_ref, dest_ref):
    e = e_ref[...]
    dest = rank_ref[...]
    for x in range(N_EXPERTS):
        dest = dest + jnp.where(e == x, pstart_ref[x], 0)
    dest_ref[...] = dest


def slot_dest(pstarts, top_e, rank, *, tm=4096):
    K, T = top_e.shape
    tm = min(tm, T)
    blockspec = pl.BlockSpec((K, tm), lambda i, ps: (0, i))
    return pl.pallas_call(
        _slot_dest_kernel,
        out_shape=jax.ShapeDtypeStruct((K, T), jnp.int32),
        grid_spec=pltpu.PrefetchScalarGridSpec(
            num_scalar_prefetch=1, grid=(T // tm,),
            in_specs=[blockspec, blockspec], out_specs=blockspec),
        compiler_params=_cparams(("arbitrary",)),
        name="slot_dest",
    )(pstarts, top_e, rank)


def _dispatch_kernel(zrow_ref, zon_ref, dest_ref, h_ref, xs_ref, zero_scr, sem, zsem, *, tm, s8, blk):
    def zero_copy(x):
        row = pl.multiple_of(zrow_ref[x] * s8, 8)
        return pltpu.make_async_copy(zero_scr, xs_ref.at[pl.ds(row, blk * s8)], zsem)

    @pl.when(pl.program_id(0) == 0)
    def _():
        zero_scr[...] = jnp.zeros_like(zero_scr)

        def start(x, c):
            @pl.when(zon_ref[x] > 0)
            def _():
                zero_copy(x).start()
            return c

        def wait(x, c):
            @pl.when(zon_ref[x] > 0)
            def _():
                zero_copy(x).wait()
            return c

        lax.fori_loop(0, 2 * N_EXPERTS, start, 0)
        lax.fori_loop(0, 2 * N_EXPERTS, wait, 0)

    def issue(t, c):
        src = h_ref.at[pl.ds(pl.multiple_of(t * s8, s8), s8)]
        for k in range(TOP_K):
            row = pl.multiple_of(dest_ref[k, t] * s8, s8)
            pltpu.make_async_copy(src, xs_ref.at[pl.ds(row, s8)], sem).start(priority=k % 2)
        return c

    lax.fori_loop(0, tm, issue, 0)
    for _ in range(TOP_K):
        pltpu.make_async_copy(h_ref, xs_ref.at[pl.ds(0, tm * s8)], sem).wait()


def dispatch(hp, dest, zrow, zon, *, n_rows, blk, tm=512):
    K, T = dest.shape
    s8 = hp.shape[0] // T
    tm = min(tm, T)
    return pl.pallas_call(
        functools.partial(_dispatch_kernel, tm=tm, s8=s8, blk=blk),
        out_shape=jax.ShapeDtypeStruct((n_rows * s8, LANES), jnp.uint32),
        grid_spec=pltpu.PrefetchScalarGridSpec(
            num_scalar_prefetch=2, grid=(T // tm,),
            in_specs=[
                pl.BlockSpec((K, tm), lambda i, zr, zc: (0, i), memory_space=pltpu.SMEM),
                pl.BlockSpec((tm * s8, LANES), lambda i, zr, zc: (i, 0)),
            ],
            out_specs=pl.BlockSpec(memory_space=pl.ANY),
            scratch_shapes=[pltpu.VMEM((blk * s8, LANES), jnp.uint32),
                            pltpu.SemaphoreType.DMA, pltpu.SemaphoreType.DMA],
        ),
        compiler_params=_cparams(("arbitrary",)),
        name="dispatch",
    )(zrow, zon, dest, hp)


def _expert_kernel(be_ref, nb_ref, x_ref, wg_ref, wu_ref, wd_ref, o_ref, wg_s, wu_s, wd_s, *, blk, s8):
    b = pl.program_id(0)

    @pl.when((b == 0) | (be_ref[b] != be_ref[jnp.maximum(b - 1, 0)]))
    def _():
        wg_s[...] = wg_ref[0, 0].astype(BF16)
        wu_s[...] = wu_ref[0, 0].astype(BF16)
        wd_s[...] = wd_ref[0, 0].astype(BF16)

    @pl.when(b < nb_ref[0])
    def _():
        x = _load_token_major(x_ref, blk, s8)
        a = jnp.dot(x, wg_s[...], preferred_element_type=F32)
        u = jnp.dot(x, wu_s[...], preferred_element_type=F32)
        hmid = (a * _sigmoid(a) * u).astype(BF16)
        y = jnp.dot(hmid, wd_s[...], preferred_element_type=F32)
        _store_token_major(o_ref, _pack_halves(y), s8)

    @pl.when(b >= nb_ref[0])
    def _():
        o_ref[...] = jnp.zeros_like(o_ref)


def experts(xs, block_e, n_used, wg, wu, wd, *, layer, blk):
    D, E = wg.shape[2], wg.shape[3]
    s8 = D // (2 * LANES)
    nblk = xs.shape[0] // (blk * s8)

    def used(b, nb):
        return jnp.minimum(b, nb[0] - 1)

    def weight(shape):
        return pl.BlockSpec((1, 1) + shape, lambda b, be, nb: (layer, be[used(b, nb)], 0, 0))

    grid_spec = pltpu.PrefetchScalarGridSpec(
        num_scalar_prefetch=2,
        grid=(nblk,),
        in_specs=[
            pl.BlockSpec((blk * s8, LANES), lambda b, be, nb: (used(b, nb), 0)),
            weight((D, E)), weight((D, E)), weight((E, D)),
        ],
        out_specs=pl.BlockSpec((blk * s8, LANES), lambda b, be, nb: (b, 0)),
        scratch_shapes=[pltpu.VMEM((D, E), BF16), pltpu.VMEM((D, E), BF16), pltpu.VMEM((E, D), BF16)],
    )
    return pl.pallas_call(
        functools.partial(_expert_kernel, blk=blk, s8=s8),
        out_shape=jax.ShapeDtypeStruct(xs.shape, jnp.uint32),
        grid_spec=grid_spec,
        compiler_params=_cparams(("arbitrary",)),
        name="experts",
    )(block_e, n_used, xs, wg, wu, wd)


def _ffn_out_kernel(dest_ref, h_ref, wt_ref, wg_ref, wu_ref, wd_ref, x_ref, g2_ref, ys_ref, o_ref,
                    buf, sem, *, tm, s8):
    def issue(t, c):
        for k in range(TOP_K):
            row = pl.multiple_of(dest_ref[k, t] * s8, s8)
            pltpu.make_async_copy(ys_ref.at[pl.ds(row, s8)],
                                  buf.at[k, pl.ds(pl.multiple_of(t * s8, s8), s8)],
                                  sem).start(priority=k % 2)
        return c

    lax.fori_loop(0, tm, issue, 0)

    h = _load_token_major(h_ref, tm, s8)
    a = jnp.dot(h, wg_ref[...], preferred_element_type=F32)
    u = jnp.dot(h, wu_ref[...], preferred_element_type=F32)
    hmid = (a * _sigmoid(a) * u).astype(BF16)
    shared = jnp.dot(hmid, wd_ref[...], preferred_element_type=F32)

    for k in range(TOP_K):
        pltpu.make_async_copy(ys_ref.at[pl.ds(0, tm * s8)], buf.at[k], sem).wait()

    half = s8 * LANES
    wts = [jnp.broadcast_to(wt_ref[:, k:k + 1], (tm, LANES)) for k in range(TOP_K)]
    g2 = g2_ref[0]
    for s in range(s8):
        acc_lo = jnp.zeros((tm, LANES), F32)
        acc_hi = jnp.zeros((tm, LANES), F32)
        for k in range(TOP_K):
            lo, hi = _unpack_halves(buf.at[k][pl.ds(s, tm, stride=s8), :])
            acc_lo = acc_lo + wts[k] * lo
            acc_hi = acc_hi + wts[k] * hi
        for off, acc in ((s * LANES, acc_lo), (half + s * LANES, acc_hi)):
            cols = slice(off, off + LANES)
            o_ref[:, cols] = x_ref[:, cols] + g2[:, cols] * (shared[:, cols] + acc)


def ffn_out(hp, ys, dest, w_tok, wsg, wsu, wsd, x2, g2, *, seq, tm=256):
    T, D = x2.shape
    tm = min(tm, seq)
    bpb = seq // tm
    s8 = D // (2 * LANES)
    return pl.pallas_call(
        functools.partial(_ffn_out_kernel, tm=tm, s8=s8),
        out_shape=jax.ShapeDtypeStruct((T, D), F32),
        grid=(T // tm,),
        in_specs=[
            pl.BlockSpec((TOP_K, tm), lambda i: (0, i), memory_space=pltpu.SMEM),
            pl.BlockSpec((tm * s8, LANES), lambda i: (i, 0)),
            pl.BlockSpec((tm, LANES), lambda i: (i, 0)),
            _resident(wsg.shape), _resident(wsu.shape), _resident(wsd.shape),
            pl.BlockSpec((tm, D), lambda i: (i, 0)),
            pl.BlockSpec((1, 1, D), lambda i: (i // bpb, 0, 0)),
            pl.BlockSpec(memory_space=pl.ANY),
        ],
        out_specs=pl.BlockSpec((tm, D), lambda i: (i, 0)),
        scratch_shapes=[pltpu.VMEM((TOP_K, tm * s8, LANES), jnp.uint32), pltpu.SemaphoreType.DMA],
        compiler_params=_cparams(("arbitrary",)),
        name="ffn_out",
    )(dest, hp, w_tok, wsg, wsu, wsd, x2, g2, ys)


def block_plan(counts, *, blk, n_slots):
    padded = ((counts + blk - 1) // blk) * blk
    pends = jnp.cumsum(padded)
    pstarts = (pends - padded).astype(jnp.int32)
    nblk = -(-n_slots // blk) + N_EXPERTS
    first_row = jnp.arange(nblk, dtype=pends.dtype) * blk
    block_e = jnp.sum(pends[None, :] <= first_row[:, None], axis=1)
    block_e = jnp.minimum(block_e, N_EXPERTS - 1).astype(jnp.int32)
    n_used = (pends[-1] // blk).astype(jnp.int32)
    tail = n_used + jnp.arange(N_EXPERTS, dtype=jnp.int32)
    zrow = jnp.concatenate([pends - blk, jnp.minimum(tail, nblk - 1) * blk]).astype(jnp.int32)
    zon = jnp.concatenate([counts > 0, tail < nblk]).astype(jnp.int32)
    return pstarts, block_e, n_used.reshape(1), zrow, zon, nblk


def _layer_weights(l, D, norm1_g, norm2_g, w_in, b_fgt, fox_qn_g, fox_kn_g, diff_qn_g, diff_kn_g,
                   diff_out_g, w_branch, w_out, w_router, w_exp_gate, w_exp_up, w_exp_down,
                   w_sh_gate, w_sh_up, w_sh_down):
    W = ATTN_WIDTH
    fcol = 3 * W
    w = w_in[l]
    dcol = fcol + N_FOX_HEADS
    w_cat = jnp.concatenate([w[:, :2 * W], w[:, dcol:dcol + 2 * W], w[:, dcol + 3 * W:],
                             w[:, 2 * W:fcol], w[:, dcol + 2 * W:dcol + 3 * W]], axis=1).astype(BF16)
    w_ff = jnp.zeros((D, LANES), BF16).at[:, :N_FOX_HEADS].set(
        w[:, fcol:fcol + N_FOX_HEADS].astype(BF16))
    qscale = HEAD_DIM ** -0.5 * LOG2E
    colgain = jnp.concatenate([
        jnp.tile(fox_qn_g[l].astype(F32) * qscale, N_FOX_HEADS),
        jnp.tile(fox_kn_g[l].astype(F32), N_FOX_HEADS),
        jnp.tile(diff_qn_g[l].astype(F32) * qscale, 2 * N_DIFF_HEADS),
        jnp.tile(diff_kn_g[l].astype(F32), 2 * N_DIFF_HEADS),
        jnp.ones((2 * D + 2 * W,), F32),
    ]).reshape(1, -1)
    return dict(
        w_cat=w_cat, w_ff=w_ff, colgain=colgain,
        wb0=w_branch[l, 0].astype(BF16), wb1=w_branch[l, 1].astype(BF16), wo=w_out[l].astype(BF16),
        wr_t=w_router[l].T.astype(BF16),
        wsg=w_sh_gate[l].astype(BF16), wsu=w_sh_up[l].astype(BF16), wsd=w_sh_down[l].astype(BF16),
    )


def kernel(x, c, positions, norm1_g, norm2_g, w_ada, b_ada, w_in, b_fgt, fox_qn_g, fox_kn_g,
           diff_qn_g, diff_kn_g, lam_q1, lam_k1, lam_q2, lam_k2, diff_out_g, w_branch, w_out,
           w_router, b_router, w_exp_gate, w_exp_up, w_exp_down, w_sh_gate, w_sh_up, w_sh_down):
    B, S, D = x.shape
    L = w_ada.shape[0]
    T = B * S
    blk = 512
    cos_t, sin_t = rope_tables(positions)
    mod = adaln(c, w_ada, b_ada)
    x2 = x.reshape(T, D)
    for l in range(L):
        lw = _layer_weights(l, D, norm1_g, norm2_g, w_in, b_fgt, fox_qn_g, fox_kn_g, diff_qn_g,
                            diff_kn_g, diff_out_g, w_branch, w_out, w_router, w_exp_gate,
                            w_exp_up, w_exp_down, w_sh_gate, w_sh_up, w_sh_down)
        sh1, sc1, g1, sh2, sc2, g2 = [mod[l, :, k * D:(k + 1) * D].reshape(B, 1, D) for k in range(6)]
        lam_init = 0.8 - 0.6 * math.exp(-0.3 * l)

        proj, vt, ff = inproj(x2, norm1_g[l].reshape(1, D), sh1, sc1, lw['w_cat'], lw['w_ff'],
                              lw['colgain'], cos_t, sin_t, seq=S)
        k_aug = fgate_keys(ff, b_fgt[l], proj, batch=B, seq=S)
        proj3 = proj.reshape(B, S, -1)
        o_fox = fox_attention(proj3, k_aug, vt).reshape(T, ATTN_WIDTH)
        lam_vecs = [v[l].reshape(1, HEAD_DIM).astype(F32) for v in (lam_q1, lam_k1, lam_q2, lam_k2)]
        o_diff = diff_attention(proj3, vt, lam_vecs, diff_out_g[l].reshape(-1, 1).astype(F32),
                                lam_init=lam_init).reshape(T, ATTN_WIDTH)
        x2 = post_attention(o_fox, o_diff, proj, lw['wb0'], lw['wb1'], lw['wo'], x2, g1, seq=S)

        hp, top_e, w_tok = router(x2, norm2_g[l].reshape(1, D), sh2, sc2, lw['wr_t'], b_router[l], seq=S)
        rank, counts = slot_rank(top_e)
        counts = counts[:, 0]
        pstarts, block_e, n_used, zrow, zon, nblk = block_plan(counts, blk=blk, n_slots=T * TOP_K)
        dest = slot_dest(pstarts, top_e, rank)
        xs = dispatch(hp, dest, zrow, zon, n_rows=nblk * blk, blk=blk)
        ys = experts(xs, block_e, n_used, w_exp_gate, w_exp_up, w_exp_down, layer=l, blk=blk)
        x2 = ffn_out(hp, ys, dest, w_tok, lw['wsg'], lw['wsu'], lw['wsd'], x2, g2, seq=S)
    return x2.reshape(B, S, D)
```

```python
import functools
import math

import jax
import jax.numpy as jnp
import numpy as np
from jax import lax
from jax.experimental import pallas as pl
from jax.experimental.pallas import tpu as pltpu

F32 = jnp.float32
BF16 = jnp.bfloat16

HEAD_DIM = 128
N_FOX_HEADS = 8
N_DIFF_HEADS = 4
ATTN_WIDTH = N_FOX_HEADS * HEAD_DIM
CHUNK = 64
ROT_DIM = HEAD_DIM // 4
ROPE_THETA = 500000.0
N_EXPERTS = 64
N_GROUPS = 8
GROUP_SIZE = N_EXPERTS // N_GROUPS
TOPK_GROUPS = 4
TOP_K = 8
ROUTED_SCALE = 2.5
RMS_EPS = 1e-6
LANES = 128
V7X_VMEM_BYTES = 64 * 1024 * 1024
VMEM_LIMIT = V7X_VMEM_BYTES - 8 * 1024 * 1024

SEC_FQ, SEC_FK, SEC_DQ, SEC_DK, SEC_GATE = range(5)
LOG2E = math.log2(math.e)
BIAS_LANES = 3


def _cparams(sem):
    return pltpu.CompilerParams(dimension_semantics=sem, vmem_limit_bytes=VMEM_LIMIT)


def _rms(y, axis=-1):
    return y * lax.rsqrt(jnp.mean(y * y, axis=axis, keepdims=True) + RMS_EPS)


def _sigmoid(z):
    return 1.0 / (1.0 + jnp.exp(-z))


def _adaln_kernel(c_ref, w_ref, b_ref, o_ref):
    c = c_ref[...]
    cond = (c * _sigmoid(c)).astype(BF16)
    o_ref[0] = jnp.dot(cond, w_ref[0].astype(BF16), preferred_element_type=F32) + b_ref[0]


def adaln(c, w_ada, b_ada, *, tn=1024):
    L, D, N = w_ada.shape
    B = c.shape[0]
    tn = min(tn, N)
    return pl.pallas_call(
        _adaln_kernel,
        out_shape=jax.ShapeDtypeStruct((L, B, N), F32),
        grid=(L, N // tn),
        in_specs=[
            pl.BlockSpec((B, D), lambda l, j: (0, 0)),
            pl.BlockSpec((1, D, tn), lambda l, j: (l, 0, j)),
            pl.BlockSpec((1, 1, tn), lambda l, j: (l, 0, j)),
        ],
        out_specs=pl.BlockSpec((1, B, tn), lambda l, j: (l, 0, j)),
        compiler_params=_cparams(("arbitrary", "arbitrary")),
        name="adaln",
    )(c, w_ada, b_ada.reshape(L, 1, N))


def _rope_kernel(pos_ref, invf_ref, cos_ref, sin_ref):
    ang = pos_ref[...].astype(F32) * invf_ref[...]
    lane = lax.broadcasted_iota(jnp.int32, ang.shape, 1)
    c = jnp.cos(ang)
    s = jnp.sin(ang)
    cos_ref[...] = jnp.where(lane < ROT_DIM, c, 1.0)
    sin_ref[...] = jnp.where(lane < ROT_DIM // 2, -s, jnp.where(lane < ROT_DIM, s, 0.0))


def rope_tables(positions, *, tm=1024):
    T = positions.size
    tm = min(tm, T)
    half = ROT_DIM // 2
    inv_freq = ROPE_THETA ** (-jnp.arange(0, ROT_DIM, 2, dtype=F32) / ROT_DIM)
    invf = jnp.zeros((1, LANES), F32).at[0, :half].set(inv_freq).at[0, half:ROT_DIM].set(inv_freq)
    return pl.pallas_call(
        _rope_kernel,
        out_shape=(jax.ShapeDtypeStruct((T, LANES), F32),) * 2,
        grid=(T // tm,),
        in_specs=[
            pl.BlockSpec((tm, 1), lambda i: (i, 0)),
            pl.BlockSpec((1, LANES), lambda i: (0, 0)),
        ],
        out_specs=(pl.BlockSpec((tm, LANES), lambda i: (i, 0)),) * 2,
        compiler_params=_cparams(("arbitrary",)),
        name="rope_tables",
    )(positions.reshape(T, 1), invf)


def _inproj_kernel(x_ref, g_ref, sh_ref, sc_ref, w_ref, wff_ref, cg_ref, cos_ref, sin_ref,
                   o_ref, vt_ref, ff_ref, h_scr, *, sec_v):
    j = pl.program_id(1)

    @pl.when(j == 0)
    def _():
        y = _rms(x_ref[...]) * g_ref[...]
        h = (y * (1.0 + sc_ref[0]) + sh_ref[0]).astype(BF16)
        h_scr[...] = h
        ff_ref[...] = jnp.dot(h, wff_ref[...], preferred_element_type=F32)

    W = ATTN_WIDTH
    n_heads = W // HEAD_DIM

    def dots():
        h = h_scr[...]
        return [jnp.dot(h, w_ref[:, s * W:(s + 1) * W], preferred_element_type=F32) for s in range(2)]

    def normed_heads(acc, s):
        for hd in range(n_heads):
            cols = slice(s * W + hd * HEAD_DIM, s * W + (hd + 1) * HEAD_DIM)
            yield cols, _rms(acc[:, hd * HEAD_DIM:(hd + 1) * HEAD_DIM]) * cg_ref[:, cols]

    @pl.when(j == 0)
    def _():
        for s, acc in enumerate(dots()):
            for cols, y in normed_heads(acc, s):
                o_ref[:, cols] = y.astype(o_ref.dtype)

    @pl.when(j == 1)
    def _():
        cos_t = cos_ref[...]
        sin_t = sin_ref[...]
        low = lax.broadcasted_iota(jnp.int32, cos_t.shape, 1) < ROT_DIM // 2
        for s, acc in enumerate(dots()):
            for cols, y in normed_heads(acc, s):
                rot = jnp.where(low, pltpu.roll(y, HEAD_DIM - ROT_DIM // 2, 1),
                                pltpu.roll(y, ROT_DIM // 2, 1))
                o_ref[:, cols] = (y * cos_t + rot * sin_t).astype(o_ref.dtype)

    @pl.when((j >= 2) & (j < sec_v))
    def _():
        for s, acc in enumerate(dots()):
            o_ref[:, s * W:(s + 1) * W] = _sigmoid(acc).astype(o_ref.dtype)

    @pl.when(j >= sec_v)
    def _():
        for s, acc in enumerate(dots()):
            vt_ref[s * W:(s + 1) * W, :] = acc.T.astype(vt_ref.dtype)


def inproj(x2, g, shift, scale, w_cat, w_ff, colgain, cos_t, sin_t, *, seq, tm=512):
    T, D = x2.shape
    NC = w_cat.shape[1]
    tn = 2 * ATTN_WIDTH
    tm = min(tm, seq)
    bpb = seq // tm
    sec_v = NC // tn - 1
    return pl.pallas_call(
        functools.partial(_inproj_kernel, sec_v=sec_v),
        out_shape=(jax.ShapeDtypeStruct((T, sec_v * tn), BF16),
                   jax.ShapeDtypeStruct((tn, T), BF16),
                   jax.ShapeDtypeStruct((T, LANES), F32)),
        grid=(T // tm, NC // tn),
        in_specs=[
            pl.BlockSpec((tm, D), lambda i, j: (i, 0)),
            pl.BlockSpec((1, D), lambda i, j: (0, 0)),
            pl.BlockSpec((1, 1, D), lambda i, j: (i // bpb, 0, 0)),
            pl.BlockSpec((1, 1, D), lambda i, j: (i // bpb, 0, 0)),
            pl.BlockSpec((D, tn), lambda i, j: (0, j)),
            pl.BlockSpec((D, LANES), lambda i, j: (0, 0)),
            pl.BlockSpec((1, tn), lambda i, j: (0, j)),
            pl.BlockSpec((tm, LANES), lambda i, j: (i, 0)),
            pl.BlockSpec((tm, LANES), lambda i, j: (i, 0)),
        ],
        out_specs=(
            pl.BlockSpec((tm, tn), lambda i, j: (i, jnp.minimum(j, sec_v - 1))),
            pl.BlockSpec((tn, tm), lambda i, j: (0, i)),
            pl.BlockSpec((tm, LANES), lambda i, j: (i, 0)),
        ),
        scratch_shapes=[pltpu.VMEM((tm, D), BF16)],
        compiler_params=_cparams(("arbitrary", "arbitrary")),
        name="inproj",
    )(x2, g, shift, scale, w_cat, w_ff, colgain, cos_t, sin_t)


def _split3(v):
    hi = v.astype(BF16)
    r = v - hi.astype(F32)
    mid = r.astype(BF16)
    lo = (r - mid.astype(F32)).astype(BF16)
    return hi, mid, lo


def _fgate_kernel(ff_ref, b_ref, k_ref, o_ref, carry_scr):
    @pl.when(pl.program_id(1) == 0)
    def _():
        carry_scr[...] = jnp.zeros_like(carry_scr)

    z = ff_ref[...] + b_ref[...]
    logf = jnp.minimum(z, 0.0) - jnp.log(1.0 + jnp.exp(-jnp.abs(z)))
    tm = z.shape[0]
    row = lax.broadcasted_iota(jnp.int32, (tm, tm), 0)
    col = lax.broadcasted_iota(jnp.int32, (tm, tm), 1)
    tri = jnp.where(row >= col, 1.0, 0.0).astype(BF16)
    cum = carry_scr[...]
    for part in _split3(logf):
        cum = cum + jnp.dot(tri, part, preferred_element_type=F32)
    carry_scr[...] = cum[tm - 1:tm, :]
    bias = cum * (-LOG2E)
    lane = lax.broadcasted_iota(jnp.int32, (tm, HEAD_DIM), 1)
    for hd in range(N_FOX_HEADS):
        hi, mid, lo = [p.astype(F32) for p in _split3(bias[:, hd:hd + 1])]
        ext = jnp.where(lane == 0, hi, jnp.where(lane == 1, mid, jnp.where(lane == 2, lo, 0.0)))
        o_ref[0, hd, :, :HEAD_DIM] = k_ref[:, hd * HEAD_DIM:(hd + 1) * HEAD_DIM]
        o_ref[0, hd, :, HEAD_DIM:] = ext.astype(o_ref.dtype)


def fgate_keys(ff, b_fgt, proj, *, batch, seq, tm=512):
    tm = min(tm, seq)
    b_pad = jnp.zeros((1, LANES), F32).at[0, :N_FOX_HEADS].set(b_fgt.astype(F32))
    nb = seq // tm
    return pl.pallas_call(
        _fgate_kernel,
        out_shape=jax.ShapeDtypeStruct((batch, N_FOX_HEADS, seq, 2 * HEAD_DIM), BF16),
        grid=(batch, nb),
        in_specs=[
            pl.BlockSpec((tm, LANES), lambda b, i: (b * nb + i, 0)),
            pl.BlockSpec((1, LANES), lambda b, i: (0, 0)),
            pl.BlockSpec((tm, ATTN_WIDTH), lambda b, i: (b * nb + i, SEC_FK)),
        ],
        out_specs=pl.BlockSpec((1, N_FOX_HEADS, tm, 2 * HEAD_DIM), lambda b, i: (b, 0, i, 0)),
        scratch_shapes=[pltpu.VMEM((1, LANES), F32)],
        compiler_params=_cparams(("arbitrary", "arbitrary")),
        name="fgate",
    )(ff, b_pad, proj)


_NT = (((1,), (1,)), ((), ()))
ONES_ROWS = 16


def _flash_step(sT, vT, m_ref, acc_ref):
    m_old = m_ref[...]
    m_new = jnp.maximum(m_old, jnp.max(sT, axis=0, keepdims=True))
    alpha = jnp.exp2(m_old - m_new)
    pT = jnp.exp2(sT - m_new).astype(vT.dtype)
    acc_ref[...] = alpha * acc_ref[...] + jnp.dot(vT, pT, preferred_element_type=F32)
    m_ref[...] = m_new


def _kv_schedule(i, qk, spv):
    @pl.when(i == 0)
    def _():
        qk(0, True, 0)
        spv(0, 0)

    @pl.when(i > 0)
    def _():
        qk(0, False, 0)

        def pair(jj, carry):
            qk(2 * jj + 1, False, 1)
            spv(0, 2 * jj)
            qk(2 * jj + 2, False, 0)
            spv(1, 2 * jj + 1)
            return carry

        lax.fori_loop(0, (i - 1) // 2, pair, 0)

        @pl.when(i % 2 == 1)
        def _():
            qk(i, True, 1)
            spv(0, i - 1)
            spv(1, i)

        @pl.when(i % 2 == 0)
        def _():
            qk(i - 1, False, 1)
            spv(0, i - 2)
            qk(i, True, 0)
            spv(1, i - 1)
            spv(0, i)


def _with_ones(vT):
    return jnp.concatenate([vT, jnp.ones((ONES_ROWS, vT.shape[1]), vT.dtype)], axis=0)


def _fox_kernel(q_ref, k_ref, vt_ref, o_ref, m_scr, acc_scr, s0, s1, *, tq):
    i = pl.program_id(2)
    lane = lax.broadcasted_iota(jnp.int32, (tq, HEAD_DIM), 1)
    q = jnp.concatenate([q_ref[0], jnp.where(lane < BIAS_LANES, 1.0, 0.0).astype(q_ref.dtype)], axis=1)
    m_scr[...] = jnp.full_like(m_scr, -jnp.inf)
    acc_scr[...] = jnp.zeros_like(acc_scr)
    bufs = (s0, s1)

    def qk(j, masked, slot):
        k = k_ref[0, 0, pl.ds(pl.multiple_of(j * tq, tq), tq), :]
        sT = lax.dot_general(k, q, _NT, preferred_element_type=F32)
        if masked:
            key = lax.broadcasted_iota(jnp.int32, sT.shape, 0)
            qry = lax.broadcasted_iota(jnp.int32, sT.shape, 1)
            sT = jnp.where(key <= qry, sT, -jnp.inf)
        bufs[slot][...] = sT

    def spv(slot, j):
        vT = _with_ones(vt_ref[:, pl.ds(pl.multiple_of(j * tq, tq), tq)])
        _flash_step(bufs[slot][...], vT, m_scr, acc_scr)

    _kv_schedule(i, qk, spv)
    o = acc_scr[:HEAD_DIM, :] / acc_scr[HEAD_DIM:HEAD_DIM + 1, :]
    o_ref[0] = o.T.astype(o_ref.dtype)


def fox_attention(proj3, k_aug, vt, *, tq=512):
    B, S, _ = proj3.shape
    tq = min(tq, S)
    H = N_FOX_HEADS
    return pl.pallas_call(
        functools.partial(_fox_kernel, tq=tq),
        out_shape=jax.ShapeDtypeStruct((B, S, ATTN_WIDTH), BF16),
        grid=(B, H, S // tq),
        in_specs=[
            pl.BlockSpec((1, tq, HEAD_DIM), lambda b, h, i: (b, i, SEC_FQ * H + h)),
            pl.BlockSpec((1, 1, S, 2 * HEAD_DIM), lambda b, h, i: (b, h, 0, 0)),
            pl.BlockSpec((HEAD_DIM, S), lambda b, h, i: (h, b)),
        ],
        out_specs=pl.BlockSpec((1, tq, HEAD_DIM), lambda b, h, i: (b, i, h)),
        scratch_shapes=[pltpu.VMEM((1, tq), F32), pltpu.VMEM((HEAD_DIM + ONES_ROWS, tq), F32),
                        pltpu.VMEM((tq, tq), F32), pltpu.VMEM((tq, tq), F32)],
        compiler_params=_cparams(("arbitrary", "arbitrary", "arbitrary")),
        name="fox_attn",
    )(proj3, k_aug, vt)


def _diff_kernel(q_ref, k_ref, vt_ref, lq1_ref, lk1_ref, lq2_ref, lk2_ref, g_ref, o_ref,
                 m1, a1, m2, a2, s10, s11, s20, s21, *, tq, lam_init):
    i = pl.program_id(2)
    dv = 2 * HEAD_DIM
    for m, a in ((m1, a1), (m2, a2)):
        m[...] = jnp.full_like(m, -jnp.inf)
        a[...] = jnp.zeros_like(a)
    bufs = ((s10, s11), (s20, s21))

    def qk(j, masked, slot):
        ks = pl.multiple_of(j * tq, tq)
        for mp in range(2):
            q = q_ref[0, :, mp * HEAD_DIM:(mp + 1) * HEAD_DIM]
            k = k_ref[0, pl.ds(ks, tq), mp * HEAD_DIM:(mp + 1) * HEAD_DIM]
            sT = lax.dot_general(k, q, _NT, preferred_element_type=F32)
            if masked:
                key = lax.broadcasted_iota(jnp.int32, sT.shape, 0)
                qry = lax.broadcasted_iota(jnp.int32, sT.shape, 1)
                sT = jnp.where(key // CHUNK <= qry // CHUNK, sT, -jnp.inf)
            bufs[mp][slot][...] = sT

    def spv(slot, j):
        vT = _with_ones(vt_ref[:, pl.ds(pl.multiple_of(j * tq, tq), tq)])
        for mp, (m, a) in enumerate(((m1, a1), (m2, a2))):
            _flash_step(bufs[mp][slot][...], vT, m, a)

    _kv_schedule(i, qk, spv)
    lam = (jnp.exp(jnp.sum(lq1_ref[...] * lk1_ref[...], axis=-1, keepdims=True))
           - jnp.exp(jnp.sum(lq2_ref[...] * lk2_ref[...], axis=-1, keepdims=True)) + lam_init)
    o = a1[:dv, :] / a1[dv:dv + 1, :] - lam * (a2[:dv, :] / a2[dv:dv + 1, :])
    o = _rms(o, axis=0) * g_ref[...] * (1.0 - lam_init)
    o_ref[0] = o.T.astype(o_ref.dtype)


def diff_attention(proj3, vt, lam_vecs, out_gain, *, lam_init, tq=512):
    B, S, _ = proj3.shape
    tq = min(tq, S)
    H = N_DIFF_HEADS
    dv = 2 * HEAD_DIM
    vec = pl.BlockSpec((1, HEAD_DIM), lambda b, h, i: (0, 0))
    return pl.pallas_call(
        functools.partial(_diff_kernel, tq=tq, lam_init=lam_init),
        out_shape=jax.ShapeDtypeStruct((B, S, ATTN_WIDTH), BF16),
        grid=(B, H, S // tq),
        in_specs=[
            pl.BlockSpec((1, tq, dv), lambda b, h, i: (b, i, SEC_DQ * H + h)),
            pl.BlockSpec((1, S, dv), lambda b, h, i: (b, 0, SEC_DK * H + h)),
            pl.BlockSpec((dv, S), lambda b, h, i: (H + h, b)),
            vec, vec, vec, vec,
            pl.BlockSpec((dv, 1), lambda b, h, i: (0, 0)),
        ],
        out_specs=pl.BlockSpec((1, tq, dv), lambda b, h, i: (b, i, h)),
        scratch_shapes=([pltpu.VMEM((1, tq), F32), pltpu.VMEM((dv + ONES_ROWS, tq), F32)] * 2
                        + [pltpu.VMEM((tq, tq), F32)] * 4),
        compiler_params=_cparams(("arbitrary", "arbitrary", "arbitrary")),
        name="diff_attn",
    )(proj3, proj3, vt, *lam_vecs, out_gain)


def _post_kernel(of_ref, od_ref, ga_ref, gb_ref, wb0_ref, wb1_ref, wo_ref, x_ref, g1_ref, o_ref):
    y0 = jnp.dot(of_ref[...], wb0_ref[...], preferred_element_type=F32)
    y1 = jnp.dot(od_ref[...], wb1_ref[...], preferred_element_type=F32)
    merged = ga_ref[...].astype(F32) * y0 + gb_ref[...].astype(F32) * y1
    mix = jnp.dot(merged.astype(BF16), wo_ref[...], preferred_element_type=F32)
    o_ref[...] = x_ref[...] + g1_ref[0] * mix


def _resident(shape):
    nd = len(shape)
    return pl.BlockSpec(shape, lambda *_: (0,) * nd, pipeline_mode=pl.Buffered(1))


def post_attention(o_fox, o_diff, proj, wb0, wb1, wo, x2, g1, *, seq, tm=256):
    T, D = x2.shape
    tm = min(tm, seq)
    bpb = seq // tm
    gate_blk = SEC_GATE * ATTN_WIDTH // D
    return pl.pallas_call(
        _post_kernel,
        out_shape=jax.ShapeDtypeStruct((T, D), F32),
        grid=(T // tm,),
        in_specs=[
            pl.BlockSpec((tm, ATTN_WIDTH), lambda i: (i, 0)),
            pl.BlockSpec((tm, ATTN_WIDTH), lambda i: (i, 0)),
            pl.BlockSpec((tm, D), lambda i: (i, gate_blk)),
            pl.BlockSpec((tm, D), lambda i: (i, gate_blk + 1)),
            _resident(wb0.shape), _resident(wb1.shape), _resident(wo.shape),
            pl.BlockSpec((tm, D), lambda i: (i, 0)),
            pl.BlockSpec((1, 1, D), lambda i: (i // bpb, 0, 0)),
        ],
        out_specs=pl.BlockSpec((tm, D), lambda i: (i, 0)),
        compiler_params=_cparams(("arbitrary",)),
        name="post_attn",
    )(o_fox, o_diff, proj, proj, wb0, wb1, wo, x2, g1)


def _pack_halves(y):
    half = y.shape[1] // 2
    bits = lax.bitcast_convert_type(y.astype(BF16).astype(F32), jnp.uint32)
    return (bits[:, half:] & jnp.uint32(0xFFFF0000)) | (bits[:, :half] >> 16)


def _unpack_halves(w):
    lo = lax.bitcast_convert_type(w << 16, F32)
    hi = lax.bitcast_convert_type(w & jnp.uint32(0xFFFF0000), F32)
    return lo, hi


def _store_token_major(ref, words, s8):
    m = words.shape[0]
    for s in range(s8):
        ref[pl.ds(s, m, stride=s8), :] = words[:, s * LANES:(s + 1) * LANES]


def _load_token_major(ref, m, s8):
    parts = [_unpack_halves(ref[pl.ds(s, m, stride=s8), :]) for s in range(s8)]
    return jnp.concatenate([p[0].astype(BF16) for p in parts] + [p[1].astype(BF16) for p in parts],
                           axis=1)


def _router_kernel(x_ref, g_ref, sh_ref, sc_ref, wr_ref, br_ref, h_ref, e_ref, w_ref, *, s8):
    y = _rms(x_ref[...]) * g_ref[...]
    h = y * (1.0 + sc_ref[0]) + sh_ref[0]
    hb = h.astype(BF16)
    _store_token_major(h_ref, _pack_halves(h), s8)
    logits = lax.dot_general(wr_ref[...], hb, _NT, preferred_element_type=F32)
    scores = _sigmoid(logits)
    tm = scores.shape[1]
    sc3 = scores.reshape(N_GROUPS, GROUP_SIZE, tm)
    sel = sc3 + br_ref[...]
    neg = -jnp.inf
    member = lax.broadcasted_iota(jnp.int32, sel.shape, 1)
    group = lax.broadcasted_iota(jnp.int32, sel.shape, 0)

    m1 = jnp.max(sel, axis=1, keepdims=True)
    i1 = jnp.min(jnp.where(sel == m1, member, GROUP_SIZE), axis=1, keepdims=True)
    m2 = jnp.max(jnp.where(member == i1, neg, sel), axis=1, keepdims=True)
    gscore = m1 + m2

    gidx = lax.broadcasted_iota(jnp.int32, gscore.shape, 0)
    gkeep = jnp.zeros(gscore.shape, jnp.int32)
    for _ in range(TOPK_GROUPS):
        best = jnp.max(gscore, axis=0, keepdims=True)
        bi = jnp.min(jnp.where(gscore == best, gidx, N_GROUPS), axis=0, keepdims=True)
        hit = gidx == bi
        gkeep = jnp.where(hit, 1, gkeep)
        gscore = jnp.where(hit, neg, gscore)

    cand = jnp.where(gkeep > 0, sel, neg)
    eidx = group * GROUP_SIZE + member
    weights = []
    for k in range(TOP_K):
        best = jnp.max(jnp.max(cand, axis=1, keepdims=True), axis=0, keepdims=True)
        bi = jnp.min(jnp.min(jnp.where(cand == best, eidx, N_EXPERTS), axis=1, keepdims=True),
                     axis=0, keepdims=True)
        hit = eidx == bi
        e_ref[k:k + 1, :] = bi[0]
        weights.append(jnp.sum(jnp.sum(jnp.where(hit, sc3, 0.0), axis=1, keepdims=True),
                               axis=0, keepdims=True)[0])
        cand = jnp.where(hit, neg, cand)
    wsum = functools.reduce(lambda a, b: a + b, weights)
    row = lax.broadcasted_iota(jnp.int32, (LANES, tm), 0)
    wmat = jnp.zeros((LANES, tm), F32)
    for k in range(TOP_K):
        wmat = jnp.where(row == k, weights[k] / wsum * ROUTED_SCALE, wmat)
    w_ref[...] = wmat.T


def router(x2, g, shift, scale, wr_t, b_router, *, seq, tm=512):
    T, D = x2.shape
    tm = min(tm, seq)
    bpb = seq // tm
    s8 = D // (2 * LANES)
    return pl.pallas_call(
        functools.partial(_router_kernel, s8=s8),
        out_shape=(jax.ShapeDtypeStruct((T * s8, LANES), jnp.uint32),
                   jax.ShapeDtypeStruct((TOP_K, T), jnp.int32),
                   jax.ShapeDtypeStruct((T, LANES), F32)),
        grid=(T // tm,),
        in_specs=[
            pl.BlockSpec((tm, D), lambda i: (i, 0)),
            pl.BlockSpec((1, D), lambda i: (0, 0)),
            pl.BlockSpec((1, 1, D), lambda i: (i // bpb, 0, 0)),
            pl.BlockSpec((1, 1, D), lambda i: (i // bpb, 0, 0)),
            pl.BlockSpec((N_EXPERTS, D), lambda i: (0, 0)),
            pl.BlockSpec((N_GROUPS, GROUP_SIZE, 1), lambda i: (0, 0, 0)),
        ],
        out_specs=(pl.BlockSpec((tm * s8, LANES), lambda i: (i, 0)),
                   pl.BlockSpec((TOP_K, tm), lambda i: (0, i)),
                   pl.BlockSpec((tm, LANES), lambda i: (i, 0))),
        compiler_params=_cparams(("arbitrary",)),
        name="router",
    )(x2, g, shift, scale, wr_t, b_router.reshape(N_GROUPS, GROUP_SIZE, 1).astype(F32))


def _slot_rank_kernel(e_ref, rank_ref, cnt_ref, base_scr, tri_scr):
    tm = e_ref.shape[1]

    @pl.when(pl.program_id(0) == 0)
    def _():
        base_scr[...] = jnp.zeros_like(base_scr)
        r = lax.broadcasted_iota(jnp.int32, (tm, tm), 0)
        c = lax.broadcasted_iota(jnp.int32, (tm, tm), 1)
        tri_scr[...] = jnp.where(r <= c, 1.0, 0.0).astype(BF16)

    expert = lax.broadcasted_iota(jnp.int32, (N_EXPERTS, tm), 0)
    base = base_scr[...]
    for k in range(TOP_K):
        hot = expert == e_ref[k:k + 1, :]
        onehot = jnp.where(hot, 1.0, 0.0)
        incl = jnp.dot(onehot.astype(BF16), tri_scr[...], preferred_element_type=F32)
        rank = jnp.sum(jnp.where(hot, base + incl - 1.0, 0.0), axis=0, keepdims=True)
        rank_ref[k:k + 1, :] = rank.astype(jnp.int32)
        base = base + jnp.sum(onehot, axis=1, keepdims=True)
    base_scr[...] = base
    cnt_ref[...] = jnp.broadcast_to(base, cnt_ref.shape).astype(jnp.int32)


def slot_rank(top_e, *, tm=512):
    K, T = top_e.shape
    tm = min(tm, T)
    return pl.pallas_call(
        _slot_rank_kernel,
        out_shape=(jax.ShapeDtypeStruct((K, T), jnp.int32),
                   jax.ShapeDtypeStruct((N_EXPERTS, LANES), jnp.int32)),
        grid=(T // tm,),
        in_specs=[pl.BlockSpec((K, tm), lambda i: (0, i))],
        out_specs=(pl.BlockSpec((K, tm), lambda i: (0, i)),
                   pl.BlockSpec((N_EXPERTS, LANES), lambda i: (0, 0))),
        scratch_shapes=[pltpu.VMEM((N_EXPERTS, 1), F32), pltpu.VMEM((tm, tm), BF16)],
        compiler_params=_cparams(("arbitrary",)),
        name="slot_rank",
    )(top_e)


def _slot_dest_kernel(pstart_ref, e_ref, rank_ref, dest_ref):
    e = e_ref[...]
    dest = rank_ref[...]
    for x in range(N_EXPERTS):
        dest = dest + jnp.where(e == x, pstart_ref[x], 0)
    dest_ref[...] = dest


def slot_dest(pstarts, top_e, rank, *, tm=4096):
    K, T = top_e.shape
    tm = min(tm, T)
    blockspec = pl.BlockSpec((K, tm), lambda i, ps: (0, i))
    return pl.pallas_call(
        _slot_dest_kernel,
        out_shape=jax.ShapeDtypeStruct((K, T), jnp.int32),
        grid_spec=pltpu.PrefetchScalarGridSpec(
            num_scalar_prefetch=1, grid=(T // tm,),
            in_specs=[blockspec, blockspec], out_specs=blockspec),
        compiler_params=_cparams(("arbitrary",)),
        name="slot_dest",
    )(pstarts, top_e, rank)


def _dispatch_kernel(zrow_ref, zon_ref, dest_ref, h_ref, xs_ref, zero_scr, sem, zsem, *, tm, s8, blk):
    def zero_copy(x):
        row = pl.multiple_of(zrow_ref[x] * s8, 8)
        return pltpu.make_async_copy(zero_scr, xs_ref.at[pl.ds(row, blk * s8)], zsem)

    @pl.when(pl.program_id(0) == 0)
    def _():
        zero_scr[...] = jnp.zeros_like(zero_scr)

        def start(x, c):
            @pl.when(zon_ref[x] > 0)
            def _():
                zero_copy(x).start()
            return c

        def wait(x, c):
            @pl.when(zon_ref[x] > 0)
            def _():
                zero_copy(x).wait()
            return c

        lax.fori_loop(0, 2 * N_EXPERTS, start, 0)
        lax.fori_loop(0, 2 * N_EXPERTS, wait, 0)

    def issue(t, c):
        src = h_ref.at[pl.ds(pl.multiple_of(t * s8, s8), s8)]
        for k in range(TOP_K):
            row = pl.multiple_of(dest_ref[k, t] * s8, s8)
            pltpu.make_async_copy(src, xs_ref.at[pl.ds(row, s8)], sem).start(priority=k % 2)
        return c

    lax.fori_loop(0, tm, issue, 0)
    for _ in range(TOP_K):
        pltpu.make_async_copy(h_ref, xs_ref.at[pl.ds(0, tm * s8)], sem).wait()


def dispatch(hp, dest, zrow, zon, *, n_rows, blk, tm=512):
    K, T = dest.shape
    s8 = hp.shape[0] // T
    tm = min(tm, T)
    return pl.pallas_call(
        functools.partial(_dispatch_kernel, tm=tm, s8=s8, blk=blk),
        out_shape=jax.ShapeDtypeStruct((n_rows * s8, LANES), jnp.uint32),
        grid_spec=pltpu.PrefetchScalarGridSpec(
            num_scalar_prefetch=2, grid=(T // tm,),
            in_specs=[
                pl.BlockSpec((K, tm), lambda i, zr, zc: (0, i), memory_space=pltpu.SMEM),
                pl.BlockSpec((tm * s8, LANES), lambda i, zr, zc: (i, 0)),
            ],
            out_specs=pl.BlockSpec(memory_space=pl.ANY),
            scratch_shapes=[pltpu.VMEM((blk * s8, LANES), jnp.uint32),
                            pltpu.SemaphoreType.DMA, pltpu.SemaphoreType.DMA],
        ),
        compiler_params=_cparams(("arbitrary",)),
        name="dispatch",
    )(zrow, zon, dest, hp)


def _expert_kernel(be_ref, nb_ref, slot_ref, nxt_ref, x_ref, wg_hbm, wu_hbm, wd_hbm, o_ref,
                   fg, fu, fd, wg_s, wu_s, wd_s, sems, *, layer, blk, s8):
    b = pl.program_id(0)
    e = be_ref[b]
    live = b < nb_ref[0]
    first = (b == 0) | (e != be_ref[jnp.maximum(b - 1, 0)])

    def fetch(expert, slot):
        return [pltpu.make_async_copy(w.at[layer, expert], f.at[slot], sems.at[slot])
                for w, f in ((wg_hbm, fg), (wu_hbm, fu), (wd_hbm, fd))]

    @pl.when(live & (b == 0))
    def _():
        for c in fetch(e, 0):
            c.start()

    @pl.when(live & first)
    def _():
        slot = slot_ref[b]
        nxt = nxt_ref[e]

        @pl.when(nxt >= 0)
        def _():
            for c in fetch(nxt, 1 - slot):
                c.start()

        for c in fetch(e, slot):
            c.wait()
        wg_s[...] = fg[slot].astype(BF16)
        wu_s[...] = fu[slot].astype(BF16)
        wd_s[...] = fd[slot].astype(BF16)

    @pl.when(live)
    def _():
        x = _load_token_major(x_ref, blk, s8)
        a = jnp.dot(x, wg_s[...], preferred_element_type=F32)
        u = jnp.dot(x, wu_s[...], preferred_element_type=F32)
        hmid = (a * _sigmoid(a) * u).astype(BF16)
        y = jnp.dot(hmid, wd_s[...], preferred_element_type=F32)
        _store_token_major(o_ref, _pack_halves(y), s8)

    @pl.when(b >= nb_ref[0])
    def _():
        o_ref[...] = jnp.zeros_like(o_ref)


def experts(xs, block_e, n_used, run_slot, next_expert, wg, wu, wd, *, layer, blk):
    D, E = wg.shape[2], wg.shape[3]
    s8 = D // (2 * LANES)
    nblk = xs.shape[0] // (blk * s8)
    hbm = pl.BlockSpec(memory_space=pl.ANY)
    grid_spec = pltpu.PrefetchScalarGridSpec(
        num_scalar_prefetch=4,
        grid=(nblk,),
        in_specs=[
            pl.BlockSpec((blk * s8, LANES), lambda b, be, nb, sl, nx: (jnp.minimum(b, nb[0] - 1), 0)),
            hbm, hbm, hbm,
        ],
        out_specs=pl.BlockSpec((blk * s8, LANES), lambda b, be, nb, sl, nx: (b, 0)),
        scratch_shapes=[pltpu.VMEM((2, D, E), F32), pltpu.VMEM((2, D, E), F32), pltpu.VMEM((2, E, D), F32),
                        pltpu.VMEM((D, E), BF16), pltpu.VMEM((D, E), BF16), pltpu.VMEM((E, D), BF16),
                        pltpu.SemaphoreType.DMA((2,))],
    )
    return pl.pallas_call(
        functools.partial(_expert_kernel, layer=layer, blk=blk, s8=s8),
        out_shape=jax.ShapeDtypeStruct(xs.shape, jnp.uint32),
        grid_spec=grid_spec,
        compiler_params=_cparams(("arbitrary",)),
        name="experts",
    )(block_e, n_used, run_slot, next_expert, xs, wg, wu, wd)


def _ffn_out_kernel(dest_ref, dnext_ref, h_ref, wt_ref, wg_ref, wu_ref, wd_ref, x_ref, g2_ref, ys_ref,
                    o_ref, buf, sems, *, tm, s8, n_steps):
    i = pl.program_id(0)
    cur = i % 2

    def gather_tile(d_ref, slot):
        def issue(t, c):
            for k in range(TOP_K):
                row = pl.multiple_of(d_ref[k, t] * s8, s8)
                pltpu.make_async_copy(ys_ref.at[pl.ds(row, s8)],
                                      buf.at[slot, k, pl.ds(pl.multiple_of(t * s8, s8), s8)],
                                      sems.at[slot]).start(priority=k % 2)
            return c

        lax.fori_loop(0, tm, issue, 0)

    @pl.when(i == 0)
    def _():
        gather_tile(dest_ref, 0)

    @pl.when(i + 1 < n_steps)
    def _():
        gather_tile(dnext_ref, 1 - cur)

    h = _load_token_major(h_ref, tm, s8)
    a = jnp.dot(h, wg_ref[...], preferred_element_type=F32)
    u = jnp.dot(h, wu_ref[...], preferred_element_type=F32)
    hmid = (a * _sigmoid(a) * u).astype(BF16)
    shared = jnp.dot(hmid, wd_ref[...], preferred_element_type=F32)

    for k in range(TOP_K):
        pltpu.make_async_copy(ys_ref.at[pl.ds(0, tm * s8)], buf.at[cur, k], sems.at[cur]).wait()

    half = s8 * LANES
    wts = [jnp.broadcast_to(wt_ref[:, k:k + 1], (tm, LANES)) for k in range(TOP_K)]
    g2 = g2_ref[0]
    for s in range(s8):
        acc_lo = jnp.zeros((tm, LANES), F32)
        acc_hi = jnp.zeros((tm, LANES), F32)
        for k in range(TOP_K):
            lo, hi = _unpack_halves(buf.at[cur, k][pl.ds(s, tm, stride=s8), :])
            acc_lo = acc_lo + wts[k] * lo
            acc_hi = acc_hi + wts[k] * hi
        for off, acc in ((s * LANES, acc_lo), (half + s * LANES, acc_hi)):
            cols = slice(off, off + LANES)
            o_ref[:, cols] = x_ref[:, cols] + g2[:, cols] * (shared[:, cols] + acc)


def ffn_out(hp, ys, dest, w_tok, wsg, wsu, wsd, x2, g2, *, seq, tm=256):
    T, D = x2.shape
    tm = min(tm, seq)
    bpb = seq // tm
    s8 = D // (2 * LANES)
    n_steps = T // tm
    return pl.pallas_call(
        functools.partial(_ffn_out_kernel, tm=tm, s8=s8, n_steps=n_steps),
        out_shape=jax.ShapeDtypeStruct((T, D), F32),
        grid=(n_steps,),
        in_specs=[
            pl.BlockSpec((TOP_K, tm), lambda i: (0, i), memory_space=pltpu.SMEM),
            pl.BlockSpec((TOP_K, tm), lambda i: (0, jnp.minimum(i + 1, n_steps - 1)),
                         memory_space=pltpu.SMEM),
            pl.BlockSpec((tm * s8, LANES), lambda i: (i, 0)),
            pl.BlockSpec((tm, LANES), lambda i: (i, 0)),
            _resident(wsg.shape), _resident(wsu.shape), _resident(wsd.shape),
            pl.BlockSpec((tm, D), lambda i: (i, 0)),
            pl.BlockSpec((1, 1, D), lambda i: (i // bpb, 0, 0)),
            pl.BlockSpec(memory_space=pl.ANY),
        ],
        out_specs=pl.BlockSpec((tm, D), lambda i: (i, 0)),
        scratch_shapes=[pltpu.VMEM((2, TOP_K, tm * s8, LANES), jnp.uint32),
                        pltpu.SemaphoreType.DMA((2,))],
        compiler_params=_cparams(("arbitrary",)),
        name="ffn_out",
    )(dest, dest, hp, w_tok, wsg, wsu, wsd, x2, g2, ys)


def block_plan(counts, *, blk, n_slots):
    padded = ((counts + blk - 1) // blk) * blk
    pends = jnp.cumsum(padded)
    pstarts = (pends - padded).astype(jnp.int32)
    nblk = -(-n_slots // blk) + N_EXPERTS
    first_row = jnp.arange(nblk, dtype=pends.dtype) * blk
    block_e = jnp.sum(pends[None, :] <= first_row[:, None], axis=1)
    block_e = jnp.minimum(block_e, N_EXPERTS - 1).astype(jnp.int32)
    n_used = (pends[-1] // blk).astype(jnp.int32)
    tail = n_used + jnp.arange(N_EXPERTS, dtype=jnp.int32)
    zrow = jnp.concatenate([pends - blk, jnp.minimum(tail, nblk - 1) * blk]).astype(jnp.int32)
    zon = jnp.concatenate([counts > 0, tail < nblk]).astype(jnp.int32)
    new_run = jnp.concatenate([jnp.ones((1,), jnp.int32),
                               (block_e[1:] != block_e[:-1]).astype(jnp.int32)])
    run_slot = ((jnp.cumsum(new_run) - 1) % 2).astype(jnp.int32)
    ids = jnp.arange(N_EXPERTS, dtype=jnp.int32)
    later = (ids[None, :] > ids[:, None]) & (counts[None, :] > 0)
    next_expert = jnp.min(jnp.where(later, ids[None, :], N_EXPERTS), axis=1)
    next_expert = jnp.where(next_expert < N_EXPERTS, next_expert, -1).astype(jnp.int32)
    return pstarts, block_e, n_used.reshape(1), zrow, zon, run_slot, next_expert, nblk


def _layer_weights(l, D, norm1_g, norm2_g, w_in, b_fgt, fox_qn_g, fox_kn_g, diff_qn_g, diff_kn_g,
                   diff_out_g, w_branch, w_out, w_router, w_exp_gate, w_exp_up, w_exp_down,
                   w_sh_gate, w_sh_up, w_sh_down):
    W = ATTN_WIDTH
    fcol = 3 * W
    w = w_in[l]
    dcol = fcol + N_FOX_HEADS
    w_cat = jnp.concatenate([w[:, :2 * W], w[:, dcol:dcol + 2 * W], w[:, dcol + 3 * W:],
                             w[:, 2 * W:fcol], w[:, dcol + 2 * W:dcol + 3 * W]], axis=1).astype(BF16)
    w_ff = jnp.zeros((D, LANES), BF16).at[:, :N_FOX_HEADS].set(
        w[:, fcol:fcol + N_FOX_HEADS].astype(BF16))
    qscale = HEAD_DIM ** -0.5 * LOG2E
    colgain = jnp.concatenate([
        jnp.tile(fox_qn_g[l].astype(F32) * qscale, N_FOX_HEADS),
        jnp.tile(fox_kn_g[l].astype(F32), N_FOX_HEADS),
        jnp.tile(diff_qn_g[l].astype(F32) * qscale, 2 * N_DIFF_HEADS),
        jnp.tile(diff_kn_g[l].astype(F32), 2 * N_DIFF_HEADS),
        jnp.ones((2 * D + 2 * W,), F32),
    ]).reshape(1, -1)
    return dict(
        w_cat=w_cat, w_ff=w_ff, colgain=colgain,
        wb0=w_branch[l, 0].astype(BF16), wb1=w_branch[l, 1].astype(BF16), wo=w_out[l].astype(BF16),
        wr_t=w_router[l].T.astype(BF16),
        wsg=w_sh_gate[l].astype(BF16), wsu=w_sh_up[l].astype(BF16), wsd=w_sh_down[l].astype(BF16),
    )


def kernel(x, c, positions, norm1_g, norm2_g, w_ada, b_ada, w_in, b_fgt, fox_qn_g, fox_kn_g,
           diff_qn_g, diff_kn_g, lam_q1, lam_k1, lam_q2, lam_k2, diff_out_g, w_branch, w_out,
           w_router, b_router, w_exp_gate, w_exp_up, w_exp_down, w_sh_gate, w_sh_up, w_sh_down):
    B, S, D = x.shape
    L = w_ada.shape[0]
    T = B * S
    blk = 512
    cos_t, sin_t = rope_tables(positions)
    mod = adaln(c, w_ada, b_ada)
    x2 = x.reshape(T, D)
    for l in range(L):
        lw = _layer_weights(l, D, norm1_g, norm2_g, w_in, b_fgt, fox_qn_g, fox_kn_g, diff_qn_g,
                            diff_kn_g, diff_out_g, w_branch, w_out, w_router, w_exp_gate,
                            w_exp_up, w_exp_down, w_sh_gate, w_sh_up, w_sh_down)
        sh1, sc1, g1, sh2, sc2, g2 = [mod[l, :, k * D:(k + 1) * D].reshape(B, 1, D) for k in range(6)]
        lam_init = 0.8 - 0.6 * math.exp(-0.3 * l)

        proj, vt, ff = inproj(x2, norm1_g[l].reshape(1, D), sh1, sc1, lw['w_cat'], lw['w_ff'],
                              lw['colgain'], cos_t, sin_t, seq=S)
        k_aug = fgate_keys(ff, b_fgt[l], proj, batch=B, seq=S)
        proj3 = proj.reshape(B, S, -1)
        o_fox = fox_attention(proj3, k_aug, vt).reshape(T, ATTN_WIDTH)
        lam_vecs = [v[l].reshape(1, HEAD_DIM).astype(F32) for v in (lam_q1, lam_k1, lam_q2, lam_k2)]
        o_diff = diff_attention(proj3, vt, lam_vecs, diff_out_g[l].reshape(-1, 1).astype(F32),
                                lam_init=lam_init).reshape(T, ATTN_WIDTH)
        x2 = post_attention(o_fox, o_diff, proj, lw['wb0'], lw['wb1'], lw['wo'], x2, g1, seq=S)

        hp, top_e, w_tok = router(x2, norm2_g[l].reshape(1, D), sh2, sc2, lw['wr_t'], b_router[l], seq=S)
        rank, counts = slot_rank(top_e)
        counts = counts[:, 0]
        pstarts, block_e, n_used, zrow, zon, run_slot, next_expert, nblk = block_plan(
            counts, blk=blk, n_slots=T * TOP_K)
        dest = slot_dest(pstarts, top_e, rank)
        xs = dispatch(hp, dest, zrow, zon, n_rows=nblk * blk, blk=blk)
        ys = experts(xs, block_e, n_used, run_slot, next_expert, w_exp_gate, w_exp_up, w_exp_down,
                     layer=l, blk=blk)
        x2 = ffn_out(hp, ys, dest, w_tok, lw['wsg'], lw['wsu'], lw['wsd'], x2, g2, seq=S)
    return x2.reshape(B, S, D)
```

```python
import functools
import math

import jax
import jax.numpy as jnp
import numpy as np
from jax import lax
from jax.experimental import pallas as pl
from jax.experimental.pallas import tpu as pltpu

F32 = jnp.float32
BF16 = jnp.bfloat16

HEAD_DIM = 128
N_FOX_HEADS = 8
N_DIFF_HEADS = 4
ATTN_WIDTH = N_FOX_HEADS * HEAD_DIM
CHUNK = 64
ROT_DIM = HEAD_DIM // 4
ROPE_THETA = 500000.0
N_EXPERTS = 64
N_GROUPS = 8
GROUP_SIZE = N_EXPERTS // N_GROUPS
TOPK_GROUPS = 4
TOP_K = 8
ROUTED_SCALE = 2.5
RMS_EPS = 1e-6
LANES = 128
V7X_VMEM_BYTES = 64 * 1024 * 1024
VMEM_LIMIT = V7X_VMEM_BYTES - 8 * 1024 * 1024

SEC_FQ, SEC_FK, SEC_DQ, SEC_DK, SEC_GATE = range(5)
LOG2E = math.log2(math.e)
BIAS_LANES = 3


def _cparams(sem):
    return pltpu.CompilerParams(dimension_semantics=sem, vmem_limit_bytes=VMEM_LIMIT)


def _rms(y, axis=-1):
    return y * lax.rsqrt(jnp.mean(y * y, axis=axis, keepdims=True) + RMS_EPS)


def _sigmoid(z):
    return 1.0 / (1.0 + jnp.exp(-z))


def _adaln_kernel(c_ref, w_ref, b_ref, o_ref):
    c = c_ref[...]
    cond = (c * _sigmoid(c)).astype(BF16)
    o_ref[0] = jnp.dot(cond, w_ref[0].astype(BF16), preferred_element_type=F32) + b_ref[0]


def adaln(c, w_ada, b_ada, *, tn=1024):
    L, D, N = w_ada.shape
    B = c.shape[0]
    tn = min(tn, N)
    return pl.pallas_call(
        _adaln_kernel,
        out_shape=jax.ShapeDtypeStruct((L, B, N), F32),
        grid=(L, N // tn),
        in_specs=[
            pl.BlockSpec((B, D), lambda l, j: (0, 0)),
            pl.BlockSpec((1, D, tn), lambda l, j: (l, 0, j)),
            pl.BlockSpec((1, 1, tn), lambda l, j: (l, 0, j)),
        ],
        out_specs=pl.BlockSpec((1, B, tn), lambda l, j: (l, 0, j)),
        compiler_params=_cparams(("arbitrary", "arbitrary")),
        name="adaln",
    )(c, w_ada, b_ada.reshape(L, 1, N))


def _rope_kernel(pos_ref, invf_ref, cos_ref, sin_ref):
    ang = pos_ref[...].astype(F32) * invf_ref[...]
    lane = lax.broadcasted_iota(jnp.int32, ang.shape, 1)
    c = jnp.cos(ang)
    s = jnp.sin(ang)
    cos_ref[...] = jnp.where(lane < ROT_DIM, c, 1.0)
    sin_ref[...] = jnp.where(lane < ROT_DIM // 2, -s, jnp.where(lane < ROT_DIM, s, 0.0))


def rope_tables(positions, *, tm=1024):
    T = positions.size
    tm = min(tm, T)
    half = ROT_DIM // 2
    inv_freq = ROPE_THETA ** (-jnp.arange(0, ROT_DIM, 2, dtype=F32) / ROT_DIM)
    invf = jnp.zeros((1, LANES), F32).at[0, :half].set(inv_freq).at[0, half:ROT_DIM].set(inv_freq)
    return pl.pallas_call(
        _rope_kernel,
        out_shape=(jax.ShapeDtypeStruct((T, LANES), F32),) * 2,
        grid=(T // tm,),
        in_specs=[
            pl.BlockSpec((tm, 1), lambda i: (i, 0)),
            pl.BlockSpec((1, LANES), lambda i: (0, 0)),
        ],
        out_specs=(pl.BlockSpec((tm, LANES), lambda i: (i, 0)),) * 2,
        compiler_params=_cparams(("arbitrary",)),
        name="rope_tables",
    )(positions.reshape(T, 1), invf)


def _inproj_kernel(x_ref, g_ref, sh_ref, sc_ref, w_ref, wff_ref, cg_ref, cos_ref, sin_ref,
                   o_ref, vt_ref, ff_ref, h_scr, *, sec_v):
    j = pl.program_id(1)

    @pl.when(j == 0)
    def _():
        y = _rms(x_ref[...]) * g_ref[...]
        h = (y * (1.0 + sc_ref[0]) + sh_ref[0]).astype(BF16)
        h_scr[...] = h
        ff_ref[...] = jnp.dot(h, wff_ref[...], preferred_element_type=F32)

    W = ATTN_WIDTH
    n_heads = W // HEAD_DIM

    def dots():
        h = h_scr[...]
        return [jnp.dot(h, w_ref[:, s * W:(s + 1) * W], preferred_element_type=F32) for s in range(2)]

    def normed_heads(acc, s):
        for hd in range(n_heads):
            cols = slice(s * W + hd * HEAD_DIM, s * W + (hd + 1) * HEAD_DIM)
            yield cols, _rms(acc[:, hd * HEAD_DIM:(hd + 1) * HEAD_DIM]) * cg_ref[:, cols]

    @pl.when(j == 0)
    def _():
        for s, acc in enumerate(dots()):
            for cols, y in normed_heads(acc, s):
                o_ref[:, cols] = y.astype(o_ref.dtype)

    @pl.when(j == 1)
    def _():
        cos_t = cos_ref[...]
        sin_t = sin_ref[...]
        low = lax.broadcasted_iota(jnp.int32, cos_t.shape, 1) < ROT_DIM // 2
        for s, acc in enumerate(dots()):
            for cols, y in normed_heads(acc, s):
                rot = jnp.where(low, pltpu.roll(y, HEAD_DIM - ROT_DIM // 2, 1),
                                pltpu.roll(y, ROT_DIM // 2, 1))
                o_ref[:, cols] = (y * cos_t + rot * sin_t).astype(o_ref.dtype)

    @pl.when((j >= 2) & (j < sec_v))
    def _():
        for s, acc in enumerate(dots()):
            o_ref[:, s * W:(s + 1) * W] = _sigmoid(acc).astype(o_ref.dtype)

    @pl.when(j >= sec_v)
    def _():
        for s, acc in enumerate(dots()):
            vt_ref[s * W:(s + 1) * W, :] = acc.T.astype(vt_ref.dtype)


def inproj(x2, g, shift, scale, w_cat, w_ff, colgain, cos_t, sin_t, *, seq, tm=512):
    T, D = x2.shape
    NC = w_cat.shape[1]
    tn = 2 * ATTN_WIDTH
    tm = min(tm, seq)
    bpb = seq // tm
    sec_v = NC // tn - 1
    return pl.pallas_call(
        functools.partial(_inproj_kernel, sec_v=sec_v),
        out_shape=(jax.ShapeDtypeStruct((T, sec_v * tn), BF16),
                   jax.ShapeDtypeStruct((tn, T), BF16),
                   jax.ShapeDtypeStruct((T, LANES), F32)),
        grid=(T // tm, NC // tn),
        in_specs=[
            pl.BlockSpec((tm, D), lambda i, j: (i, 0)),
            pl.BlockSpec((1, D), lambda i, j: (0, 0)),
            pl.BlockSpec((1, 1, D), lambda i, j: (i // bpb, 0, 0)),
            pl.BlockSpec((1, 1, D), lambda i, j: (i // bpb, 0, 0)),
            pl.BlockSpec((D, tn), lambda i, j: (0, j)),
            pl.BlockSpec((D, LANES), lambda i, j: (0, 0)),
            pl.BlockSpec((1, tn), lambda i, j: (0, j)),
            pl.BlockSpec((tm, LANES), lambda i, j: (i, 0)),
            pl.BlockSpec((tm, LANES), lambda i, j: (i, 0)),
        ],
        out_specs=(
            pl.BlockSpec((tm, tn), lambda i, j: (i, jnp.minimum(j, sec_v - 1))),
            pl.BlockSpec((tn, tm), lambda i, j: (0, i)),
            pl.BlockSpec((tm, LANES), lambda i, j: (i, 0)),
        ),
        scratch_shapes=[pltpu.VMEM((tm, D), BF16)],
        compiler_params=_cparams(("arbitrary", "arbitrary")),
        name="inproj",
    )(x2, g, shift, scale, w_cat, w_ff, colgain, cos_t, sin_t)


def _split3(v):
    hi = v.astype(BF16)
    r = v - hi.astype(F32)
    mid = r.astype(BF16)
    lo = (r - mid.astype(F32)).astype(BF16)
    return hi, mid, lo


def _fgate_kernel(ff_ref, b_ref, k_ref, o_ref, carry_scr):
    @pl.when(pl.program_id(1) == 0)
    def _():
        carry_scr[...] = jnp.zeros_like(carry_scr)

    z = ff_ref[...] + b_ref[...]
    logf = jnp.minimum(z, 0.0) - jnp.log(1.0 + jnp.exp(-jnp.abs(z)))
    tm = z.shape[0]
    row = lax.broadcasted_iota(jnp.int32, (tm, tm), 0)
    col = lax.broadcasted_iota(jnp.int32, (tm, tm), 1)
    tri = jnp.where(row >= col, 1.0, 0.0).astype(BF16)
    cum = carry_scr[...]
    for part in _split3(logf):
        cum = cum + jnp.dot(tri, part, preferred_element_type=F32)
    carry_scr[...] = cum[tm - 1:tm, :]
    bias = cum * (-LOG2E)
    lane = lax.broadcasted_iota(jnp.int32, (tm, HEAD_DIM), 1)
    for hd in range(N_FOX_HEADS):
        hi, mid, lo = [p.astype(F32) for p in _split3(bias[:, hd:hd + 1])]
        ext = jnp.where(lane == 0, hi, jnp.where(lane == 1, mid, jnp.where(lane == 2, lo, 0.0)))
        o_ref[0, hd, :, :HEAD_DIM] = k_ref[:, hd * HEAD_DIM:(hd + 1) * HEAD_DIM]
        o_ref[0, hd, :, HEAD_DIM:] = ext.astype(o_ref.dtype)


def fgate_keys(ff, b_fgt, proj, *, batch, seq, tm=512):
    tm = min(tm, seq)
    b_pad = jnp.zeros((1, LANES), F32).at[0, :N_FOX_HEADS].set(b_fgt.astype(F32))
    nb = seq // tm
    return pl.pallas_call(
        _fgate_kernel,
        out_shape=jax.ShapeDtypeStruct((batch, N_FOX_HEADS, seq, 2 * HEAD_DIM), BF16),
        grid=(batch, nb),
        in_specs=[
            pl.BlockSpec((tm, LANES), lambda b, i: (b * nb + i, 0)),
            pl.BlockSpec((1, LANES), lambda b, i: (0, 0)),
            pl.BlockSpec((tm, ATTN_WIDTH), lambda b, i: (b * nb + i, SEC_FK)),
        ],
        out_specs=pl.BlockSpec((1, N_FOX_HEADS, tm, 2 * HEAD_DIM), lambda b, i: (b, 0, i, 0)),
        scratch_shapes=[pltpu.VMEM((1, LANES), F32)],
        compiler_params=_cparams(("arbitrary", "arbitrary")),
        name="fgate",
    )(ff, b_pad, proj)


_NT = (((1,), (1,)), ((), ()))
ONES_ROWS = 16


def _flash_step(sT, vT, m_ref, acc_ref):
    m_old = m_ref[...]
    m_new = jnp.maximum(m_old, jnp.max(sT, axis=0, keepdims=True))
    alpha = jnp.exp2(m_old - m_new)
    pT = jnp.exp2(sT - m_new).astype(vT.dtype)
    acc_ref[...] = alpha * acc_ref[...] + jnp.dot(vT, pT, preferred_element_type=F32)
    m_ref[...] = m_new


def _kv_schedule(i, qk, spv):
    @pl.when(i == 0)
    def _():
        qk(0, True, 0)
        spv(0, 0)

    @pl.when(i > 0)
    def _():
        qk(0, False, 0)

        def pair(jj, carry):
            qk(2 * jj + 1, False, 1)
            spv(0, 2 * jj)
            qk(2 * jj + 2, False, 0)
            spv(1, 2 * jj + 1)
            return carry

        lax.fori_loop(0, (i - 1) // 2, pair, 0)

        @pl.when(i % 2 == 1)
        def _():
            qk(i, True, 1)
            spv(0, i - 1)
            spv(1, i)

        @pl.when(i % 2 == 0)
        def _():
            qk(i - 1, False, 1)
            spv(0, i - 2)
            qk(i, True, 0)
            spv(1, i - 1)
            spv(0, i)


def _with_ones(vT):
    return jnp.concatenate([vT, jnp.ones((ONES_ROWS, vT.shape[1]), vT.dtype)], axis=0)


FOX_HEADS_PER_STEP = 2


def _fox_kernel(q_ref, k_ref, vt_ref, o_ref, *scratch, tq):
    i = pl.program_id(2)
    hps = FOX_HEADS_PER_STEP
    heads = [scratch[4 * hh:4 * hh + 4] for hh in range(hps)]
    lane = lax.broadcasted_iota(jnp.int32, (tq, HEAD_DIM), 1)
    ones_ext = jnp.where(lane < BIAS_LANES, 1.0, 0.0).astype(q_ref.dtype)
    qs = []
    for hh, (m_scr, acc_scr, _, _) in enumerate(heads):
        qs.append(jnp.concatenate([q_ref[0, :, hh * HEAD_DIM:(hh + 1) * HEAD_DIM], ones_ext], axis=1))
        m_scr[...] = jnp.full_like(m_scr, -jnp.inf)
        acc_scr[...] = jnp.zeros_like(acc_scr)

    def qk(j, masked, slot):
        ks = pl.multiple_of(j * tq, tq)
        for hh in range(hps):
            k = k_ref[0, hh, pl.ds(ks, tq), :]
            sT = lax.dot_general(k, qs[hh], _NT, preferred_element_type=F32)
            if masked:
                key = lax.broadcasted_iota(jnp.int32, sT.shape, 0)
                qry = lax.broadcasted_iota(jnp.int32, sT.shape, 1)
                sT = jnp.where(key <= qry, sT, -jnp.inf)
            heads[hh][2 + slot][...] = sT

    def spv(slot, j):
        ks = pl.multiple_of(j * tq, tq)
        for hh, (m_scr, acc_scr, _, _) in enumerate(heads):
            vT = _with_ones(vt_ref[hh * HEAD_DIM:(hh + 1) * HEAD_DIM, pl.ds(ks, tq)])
            _flash_step(heads[hh][2 + slot][...], vT, m_scr, acc_scr)

    _kv_schedule(i, qk, spv)
    for hh, (_, acc_scr, _, _) in enumerate(heads):
        o = acc_scr[:HEAD_DIM, :] / acc_scr[HEAD_DIM:HEAD_DIM + 1, :]
        o_ref[0, :, hh * HEAD_DIM:(hh + 1) * HEAD_DIM] = o.T.astype(o_ref.dtype)


def fox_attention(proj3, k_aug, vt, *, tq=512):
    B, S, _ = proj3.shape
    tq = min(tq, S)
    hps = FOX_HEADS_PER_STEP
    G = N_FOX_HEADS // hps
    per_head = [pltpu.VMEM((1, tq), F32), pltpu.VMEM((HEAD_DIM + ONES_ROWS, tq), F32),
                pltpu.VMEM((tq, tq), F32), pltpu.VMEM((tq, tq), F32)]
    return pl.pallas_call(
        functools.partial(_fox_kernel, tq=tq),
        out_shape=jax.ShapeDtypeStruct((B, S, ATTN_WIDTH), BF16),
        grid=(B, G, S // tq),
        in_specs=[
            pl.BlockSpec((1, tq, hps * HEAD_DIM), lambda b, g, i: (b, i, SEC_FQ * G + g)),
            pl.BlockSpec((1, hps, S, 2 * HEAD_DIM), lambda b, g, i: (b, g, 0, 0)),
            pl.BlockSpec((hps * HEAD_DIM, S), lambda b, g, i: (g, b)),
        ],
        out_specs=pl.BlockSpec((1, tq, hps * HEAD_DIM), lambda b, g, i: (b, i, g)),
        scratch_shapes=per_head * hps,
        compiler_params=_cparams(("arbitrary", "arbitrary", "arbitrary")),
        name="fox_attn",
    )(proj3, k_aug, vt)


def _diff_kernel(q_ref, k_ref, vt_ref, lq1_ref, lk1_ref, lq2_ref, lk2_ref, g_ref, o_ref,
                 m1, a1, m2, a2, s10, s11, s20, s21, *, tq, lam_init):
    i = pl.program_id(2)
    dv = 2 * HEAD_DIM
    for m, a in ((m1, a1), (m2, a2)):
        m[...] = jnp.full_like(m, -jnp.inf)
        a[...] = jnp.zeros_like(a)
    bufs = ((s10, s11), (s20, s21))

    def qk(j, masked, slot):
        ks = pl.multiple_of(j * tq, tq)
        for mp in range(2):
            q = q_ref[0, :, mp * HEAD_DIM:(mp + 1) * HEAD_DIM]
            k = k_ref[0, pl.ds(ks, tq), mp * HEAD_DIM:(mp + 1) * HEAD_DIM]
            sT = lax.dot_general(k, q, _NT, preferred_element_type=F32)
            if masked:
                key = lax.broadcasted_iota(jnp.int32, sT.shape, 0)
                qry = lax.broadcasted_iota(jnp.int32, sT.shape, 1)
                sT = jnp.where(key // CHUNK <= qry // CHUNK, sT, -jnp.inf)
            bufs[mp][slot][...] = sT

    def spv(slot, j):
        vT = _with_ones(vt_ref[:, pl.ds(pl.multiple_of(j * tq, tq), tq)])
        for mp, (m, a) in enumerate(((m1, a1), (m2, a2))):
            _flash_step(bufs[mp][slot][...], vT, m, a)

    _kv_schedule(i, qk, spv)
    lam = (jnp.exp(jnp.sum(lq1_ref[...] * lk1_ref[...], axis=-1, keepdims=True))
           - jnp.exp(jnp.sum(lq2_ref[...] * lk2_ref[...], axis=-1, keepdims=True)) + lam_init)
    o = a1[:dv, :] / a1[dv:dv + 1, :] - lam * (a2[:dv, :] / a2[dv:dv + 1, :])
    o = _rms(o, axis=0) * g_ref[...] * (1.0 - lam_init)
    o_ref[0] = o.T.astype(o_ref.dtype)


def diff_attention(proj3, vt, lam_vecs, out_gain, *, lam_init, tq=512):
    B, S, _ = proj3.shape
    tq = min(tq, S)
    H = N_DIFF_HEADS
    dv = 2 * HEAD_DIM
    vec = pl.BlockSpec((1, HEAD_DIM), lambda b, h, i: (0, 0))
    return pl.pallas_call(
        functools.partial(_diff_kernel, tq=tq, lam_init=lam_init),
        out_shape=jax.ShapeDtypeStruct((B, S, ATTN_WIDTH), BF16),
        grid=(B, H, S // tq),
        in_specs=[
            pl.BlockSpec((1, tq, dv), lambda b, h, i: (b, i, SEC_DQ * H + h)),
            pl.BlockSpec((1, S, dv), lambda b, h, i: (b, 0, SEC_DK * H + h)),
            pl.BlockSpec((dv, S), lambda b, h, i: (H + h, b)),
            vec, vec, vec, vec,
            pl.BlockSpec((dv, 1), lambda b, h, i: (0, 0)),
        ],
        out_specs=pl.BlockSpec((1, tq, dv), lambda b, h, i: (b, i, h)),
        scratch_shapes=([pltpu.VMEM((1, tq), F32), pltpu.VMEM((dv + ONES_ROWS, tq), F32)] * 2
                        + [pltpu.VMEM((tq, tq), F32)] * 4),
        compiler_params=_cparams(("arbitrary", "arbitrary", "arbitrary")),
        name="diff_attn",
    )(proj3, proj3, vt, *lam_vecs, out_gain)


def _post_kernel(of_ref, od_ref, ga_ref, gb_ref, wb0_ref, wb1_ref, wo_ref, x_ref, g1_ref, o_ref):
    y0 = jnp.dot(of_ref[...], wb0_ref[...], preferred_element_type=F32)
    y1 = jnp.dot(od_ref[...], wb1_ref[...], preferred_element_type=F32)
    merged = ga_ref[...].astype(F32) * y0 + gb_ref[...].astype(F32) * y1
    mix = jnp.dot(merged.astype(BF16), wo_ref[...], preferred_element_type=F32)
    o_ref[...] = x_ref[...] + g1_ref[0] * mix


def _resident(shape):
    nd = len(shape)
    return pl.BlockSpec(shape, lambda *_: (0,) * nd, pipeline_mode=pl.Buffered(1))


def post_attention(o_fox, o_diff, proj, wb0, wb1, wo, x2, g1, *, seq, tm=256):
    T, D = x2.shape
    tm = min(tm, seq)
    bpb = seq // tm
    gate_blk = SEC_GATE * ATTN_WIDTH // D
    return pl.pallas_call(
        _post_kernel,
        out_shape=jax.ShapeDtypeStruct((T, D), F32),
        grid=(T // tm,),
        in_specs=[
            pl.BlockSpec((tm, ATTN_WIDTH), lambda i: (i, 0)),
            pl.BlockSpec((tm, ATTN_WIDTH), lambda i: (i, 0)),
            pl.BlockSpec((tm, D), lambda i: (i, gate_blk)),
            pl.BlockSpec((tm, D), lambda i: (i, gate_blk + 1)),
            _resident(wb0.shape), _resident(wb1.shape), _resident(wo.shape),
            pl.BlockSpec((tm, D), lambda i: (i, 0)),
            pl.BlockSpec((1, 1, D), lambda i: (i // bpb, 0, 0)),
        ],
        out_specs=pl.BlockSpec((tm, D), lambda i: (i, 0)),
        compiler_params=_cparams(("arbitrary",)),
        name="post_attn",
    )(o_fox, o_diff, proj, proj, wb0, wb1, wo, x2, g1)


def _pack_halves(y):
    half = y.shape[1] // 2
    bits = lax.bitcast_convert_type(y.astype(BF16).astype(F32), jnp.uint32)
    return (bits[:, half:] & jnp.uint32(0xFFFF0000)) | (bits[:, :half] >> 16)


def _unpack_halves(w):
    lo = lax.bitcast_convert_type(w << 16, F32)
    hi = lax.bitcast_convert_type(w & jnp.uint32(0xFFFF0000), F32)
    return lo, hi


def _store_token_major(ref, words, s8):
    m = words.shape[0]
    for s in range(s8):
        ref[pl.ds(s, m, stride=s8), :] = words[:, s * LANES:(s + 1) * LANES]


def _load_token_major(ref, m, s8):
    parts = [_unpack_halves(ref[pl.ds(s, m, stride=s8), :]) for s in range(s8)]
    return jnp.concatenate([p[0].astype(BF16) for p in parts] + [p[1].astype(BF16) for p in parts],
                           axis=1)


def _router_kernel(x_ref, g_ref, sh_ref, sc_ref, wr_ref, br_ref, h_ref, e_ref, w_ref, *, s8):
    y = _rms(x_ref[...]) * g_ref[...]
    h = y * (1.0 + sc_ref[0]) + sh_ref[0]
    hb = h.astype(BF16)
    _store_token_major(h_ref, _pack_halves(h), s8)
    logits = lax.dot_general(wr_ref[...], hb, _NT, preferred_element_type=F32)
    scores = _sigmoid(logits)
    tm = scores.shape[1]
    sc3 = scores.reshape(N_GROUPS, GROUP_SIZE, tm)
    sel = sc3 + br_ref[...]
    neg = -jnp.inf
    member = lax.broadcasted_iota(jnp.int32, sel.shape, 1)
    group = lax.broadcasted_iota(jnp.int32, sel.shape, 0)

    m1 = jnp.max(sel, axis=1, keepdims=True)
    i1 = jnp.min(jnp.where(sel == m1, member, GROUP_SIZE), axis=1, keepdims=True)
    m2 = jnp.max(jnp.where(member == i1, neg, sel), axis=1, keepdims=True)
    gscore = m1 + m2

    gidx = lax.broadcasted_iota(jnp.int32, gscore.shape, 0)
    gkeep = jnp.zeros(gscore.shape, jnp.int32)
    for _ in range(TOPK_GROUPS):
        best = jnp.max(gscore, axis=0, keepdims=True)
        bi = jnp.min(jnp.where(gscore == best, gidx, N_GROUPS), axis=0, keepdims=True)
        hit = gidx == bi
        gkeep = jnp.where(hit, 1, gkeep)
        gscore = jnp.where(hit, neg, gscore)

    cand = jnp.where(gkeep > 0, sel, neg)
    eidx = group * GROUP_SIZE + member
    weights = []
    for k in range(TOP_K):
        best = jnp.max(jnp.max(cand, axis=1, keepdims=True), axis=0, keepdims=True)
        bi = jnp.min(jnp.min(jnp.where(cand == best, eidx, N_EXPERTS), axis=1, keepdims=True),
                     axis=0, keepdims=True)
        hit = eidx == bi
        e_ref[k:k + 1, :] = bi[0]
        weights.append(jnp.sum(jnp.sum(jnp.where(hit, sc3, 0.0), axis=1, keepdims=True),
                               axis=0, keepdims=True)[0])
        cand = jnp.where(hit, neg, cand)
    wsum = functools.reduce(lambda a, b: a + b, weights)
    row = lax.broadcasted_iota(jnp.int32, (LANES, tm), 0)
    wmat = jnp.zeros((LANES, tm), F32)
    for k in range(TOP_K):
        wmat = jnp.where(row == k, weights[k] / wsum * ROUTED_SCALE, wmat)
    w_ref[...] = wmat.T


def router(x2, g, shift, scale, wr_t, b_router, *, seq, tm=512):
    T, D = x2.shape
    tm = min(tm, seq)
    bpb = seq // tm
    s8 = D // (2 * LANES)
    return pl.pallas_call(
        functools.partial(_router_kernel, s8=s8),
        out_shape=(jax.ShapeDtypeStruct((T * s8, LANES), jnp.uint32),
                   jax.ShapeDtypeStruct((TOP_K, T), jnp.int32),
                   jax.ShapeDtypeStruct((T, LANES), F32)),
        grid=(T // tm,),
        in_specs=[
            pl.BlockSpec((tm, D), lambda i: (i, 0)),
            pl.BlockSpec((1, D), lambda i: (0, 0)),
            pl.BlockSpec((1, 1, D), lambda i: (i // bpb, 0, 0)),
            pl.BlockSpec((1, 1, D), lambda i: (i // bpb, 0, 0)),
            pl.BlockSpec((N_EXPERTS, D), lambda i: (0, 0)),
            pl.BlockSpec((N_GROUPS, GROUP_SIZE, 1), lambda i: (0, 0, 0)),
        ],
        out_specs=(pl.BlockSpec((tm * s8, LANES), lambda i: (i, 0)),
                   pl.BlockSpec((TOP_K, tm), lambda i: (0, i)),
                   pl.BlockSpec((tm, LANES), lambda i: (i, 0))),
        compiler_params=_cparams(("arbitrary",)),
        name="router",
    )(x2, g, shift, scale, wr_t, b_router.reshape(N_GROUPS, GROUP_SIZE, 1).astype(F32))


def _slot_rank_kernel(e_ref, rank_ref, cnt_ref, base_scr, tri_scr):
    tm = e_ref.shape[1]

    @pl.when(pl.program_id(0) == 0)
    def _():
        base_scr[...] = jnp.zeros_like(base_scr)
        r = lax.broadcasted_iota(jnp.int32, (tm, tm), 0)
        c = lax.broadcasted_iota(jnp.int32, (tm, tm), 1)
        tri_scr[...] = jnp.where(r <= c, 1.0, 0.0).astype(BF16)

    expert = lax.broadcasted_iota(jnp.int32, (N_EXPERTS, tm), 0)
    base = base_scr[...]
    for k in range(TOP_K):
        hot = expert == e_ref[k:k + 1, :]
        onehot = jnp.where(hot, 1.0, 0.0)
        incl = jnp.dot(onehot.astype(BF16), tri_scr[...], preferred_element_type=F32)
        rank = jnp.sum(jnp.where(hot, base + incl - 1.0, 0.0), axis=0, keepdims=True)
        rank_ref[k:k + 1, :] = rank.astype(jnp.int32)
        base = base + jnp.sum(onehot, axis=1, keepdims=True)
    base_scr[...] = base
    cnt_ref[...] = jnp.broadcast_to(base, cnt_ref.shape).astype(jnp.int32)


def slot_rank(top_e, *, tm=512):
    K, T = top_e.shape
    tm = min(tm, T)
    return pl.pallas_call(
        _slot_rank_kernel,
        out_shape=(jax.ShapeDtypeStruct((K, T), jnp.int32),
                   jax.ShapeDtypeStruct((N_EXPERTS, LANES), jnp.int32)),
        grid=(T // tm,),
        in_specs=[pl.BlockSpec((K, tm), lambda i: (0, i))],
        out_specs=(pl.BlockSpec((K, tm), lambda i: (0, i)),
                   pl.BlockSpec((N_EXPERTS, LANES), lambda i: (0, 0))),
        scratch_shapes=[pltpu.VMEM((N_EXPERTS, 1), F32), pltpu.VMEM((tm, tm), BF16)],
        compiler_params=_cparams(("arbitrary",)),
        name="slot_rank",
    )(top_e)


def _slot_dest_kernel(pstart_ref, e_ref, rank_ref, dest_ref):
    e = e_ref[...]
    dest = rank_ref[...]
    for x in range(N_EXPERTS):
        dest = dest + jnp.where(e == x, pstart_ref[x], 0)
    dest_ref[...] = dest


def slot_dest(pstarts, top_e, rank, *, tm=4096):
    K, T = top_e.shape
    tm = min(tm, T)
    blockspec = pl.BlockSpec((K, tm), lambda i, ps: (0, i))
    return pl.pallas_call(
        _slot_dest_kernel,
        out_shape=jax.ShapeDtypeStruct((K, T), jnp.int32),
        grid_spec=pltpu.PrefetchScalarGridSpec(
            num_scalar_prefetch=1, grid=(T // tm,),
            in_specs=[blockspec, blockspec], out_specs=blockspec),
        compiler_params=_cparams(("arbitrary",)),
        name="slot_dest",
    )(pstarts, top_e, rank)


def _dispatch_kernel(zrow_ref, zon_ref, dest_ref, h_ref, xs_ref, zero_scr, sem, zsem, *, tm, s8, blk):
    def zero_copy(x):
        row = pl.multiple_of(zrow_ref[x] * s8, 8)
        return pltpu.make_async_copy(zero_scr, xs_ref.at[pl.ds(row, blk * s8)], zsem)

    @pl.when(pl.program_id(0) == 0)
    def _():
        zero_scr[...] = jnp.zeros_like(zero_scr)

        def start(x, c):
            @pl.when(zon_ref[x] > 0)
            def _():
                zero_copy(x).start()
            return c

        def wait(x, c):
            @pl.when(zon_ref[x] > 0)
            def _():
                zero_copy(x).wait()
            return c

        lax.fori_loop(0, 2 * N_EXPERTS, start, 0)
        lax.fori_loop(0, 2 * N_EXPERTS, wait, 0)

    def issue(t, c):
        src = h_ref.at[pl.ds(pl.multiple_of(t * s8, s8), s8)]
        for k in range(TOP_K):
            row = pl.multiple_of(dest_ref[k, t] * s8, s8)
            pltpu.make_async_copy(src, xs_ref.at[pl.ds(row, s8)], sem).start(priority=k % 2)
        return c

    lax.fori_loop(0, tm, issue, 0)
    for _ in range(TOP_K):
        pltpu.make_async_copy(h_ref, xs_ref.at[pl.ds(0, tm * s8)], sem).wait()


def dispatch(hp, dest, zrow, zon, *, n_rows, blk, tm=512):
    K, T = dest.shape
    s8 = hp.shape[0] // T
    tm = min(tm, T)
    return pl.pallas_call(
        functools.partial(_dispatch_kernel, tm=tm, s8=s8, blk=blk),
        out_shape=jax.ShapeDtypeStruct((n_rows * s8, LANES), jnp.uint32),
        grid_spec=pltpu.PrefetchScalarGridSpec(
            num_scalar_prefetch=2, grid=(T // tm,),
            in_specs=[
                pl.BlockSpec((K, tm), lambda i, zr, zc: (0, i), memory_space=pltpu.SMEM),
                pl.BlockSpec((tm * s8, LANES), lambda i, zr, zc: (i, 0)),
            ],
            out_specs=pl.BlockSpec(memory_space=pl.ANY),
            scratch_shapes=[pltpu.VMEM((blk * s8, LANES), jnp.uint32),
                            pltpu.SemaphoreType.DMA, pltpu.SemaphoreType.DMA],
        ),
        compiler_params=_cparams(("arbitrary",)),
        name="dispatch",
    )(zrow, zon, dest, hp)


def _expert_kernel(be_ref, nb_ref, slot_ref, nxt_ref, x_ref, wg_hbm, wu_hbm, wd_hbm, o_ref,
                   fg, fu, fd, wg_s, wu_s, wd_s, sems, *, layer, blk, s8):
    b = pl.program_id(0)
    e = be_ref[b]
    live = b < nb_ref[0]
    first = (b == 0) | (e != be_ref[jnp.maximum(b - 1, 0)])

    def fetch(expert, slot):
        return [pltpu.make_async_copy(w.at[layer, expert], f.at[slot], sems.at[slot])
                for w, f in ((wg_hbm, fg), (wu_hbm, fu), (wd_hbm, fd))]

    @pl.when(live & (b == 0))
    def _():
        for c in fetch(e, 0):
            c.start()

    @pl.when(live & first)
    def _():
        slot = slot_ref[b]
        nxt = nxt_ref[e]

        @pl.when(nxt >= 0)
        def _():
            for c in fetch(nxt, 1 - slot):
                c.start()

        for c in fetch(e, slot):
            c.wait()
        wg_s[...] = fg[slot].astype(BF16)
        wu_s[...] = fu[slot].astype(BF16)
        wd_s[...] = fd[slot].astype(BF16)

    @pl.when(live)
    def _():
        x = _load_token_major(x_ref, blk, s8)
        a = jnp.dot(x, wg_s[...], preferred_element_type=F32)
        u = jnp.dot(x, wu_s[...], preferred_element_type=F32)
        hmid = (a * _sigmoid(a) * u).astype(BF16)
        y = jnp.dot(hmid, wd_s[...], preferred_element_type=F32)
        _store_token_major(o_ref, _pack_halves(y), s8)

    @pl.when(b >= nb_ref[0])
    def _():
        o_ref[...] = jnp.zeros_like(o_ref)


def experts(xs, block_e, n_used, run_slot, next_expert, wg, wu, wd, *, layer, blk):
    D, E = wg.shape[2], wg.shape[3]
    s8 = D // (2 * LANES)
    nblk = xs.shape[0] // (blk * s8)
    hbm = pl.BlockSpec(memory_space=pl.ANY)
    grid_spec = pltpu.PrefetchScalarGridSpec(
        num_scalar_prefetch=4,
        grid=(nblk,),
        in_specs=[
            pl.BlockSpec((blk * s8, LANES), lambda b, be, nb, sl, nx: (jnp.minimum(b, nb[0] - 1), 0)),
            hbm, hbm, hbm,
        ],
        out_specs=pl.BlockSpec((blk * s8, LANES), lambda b, be, nb, sl, nx: (b, 0)),
        scratch_shapes=[pltpu.VMEM((2, D, E), F32), pltpu.VMEM((2, D, E), F32), pltpu.VMEM((2, E, D), F32),
                        pltpu.VMEM((D, E), BF16), pltpu.VMEM((D, E), BF16), pltpu.VMEM((E, D), BF16),
                        pltpu.SemaphoreType.DMA((2,))],
    )
    return pl.pallas_call(
        functools.partial(_expert_kernel, layer=layer, blk=blk, s8=s8),
        out_shape=jax.ShapeDtypeStruct(xs.shape, jnp.uint32),
        grid_spec=grid_spec,
        compiler_params=_cparams(("arbitrary",)),
        name="experts",
    )(block_e, n_used, run_slot, next_expert, xs, wg, wu, wd)


def _ffn_out_kernel(dest_ref, h_ref, wt_ref, wg_ref, wu_ref, wd_ref, x_ref, g2_ref, ys_ref, o_ref,
                    buf, sem, *, tm, s8):
    def issue(t, c):
        for k in range(TOP_K):
            row = pl.multiple_of(dest_ref[k, t] * s8, s8)
            pltpu.make_async_copy(ys_ref.at[pl.ds(row, s8)],
                                  buf.at[k, pl.ds(pl.multiple_of(t * s8, s8), s8)],
                                  sem).start(priority=k % 2)
        return c

    lax.fori_loop(0, tm, issue, 0)

    h = _load_token_major(h_ref, tm, s8)
    a = jnp.dot(h, wg_ref[...], preferred_element_type=F32)
    u = jnp.dot(h, wu_ref[...], preferred_element_type=F32)
    hmid = (a * _sigmoid(a) * u).astype(BF16)
    shared = jnp.dot(hmid, wd_ref[...], preferred_element_type=F32)

    for k in range(TOP_K):
        pltpu.make_async_copy(ys_ref.at[pl.ds(0, tm * s8)], buf.at[k], sem).wait()

    half = s8 * LANES
    wts = [jnp.broadcast_to(wt_ref[:, k:k + 1], (tm, LANES)) for k in range(TOP_K)]
    g2 = g2_ref[0]
    for s in range(s8):
        acc_lo = jnp.zeros((tm, LANES), F32)
        acc_hi = jnp.zeros((tm, LANES), F32)
        for k in range(TOP_K):
            lo, hi = _unpack_halves(buf.at[k][pl.ds(s, tm, stride=s8), :])
            acc_lo = acc_lo + wts[k] * lo
            acc_hi = acc_hi + wts[k] * hi
        for off, acc in ((s * LANES, acc_lo), (half + s * LANES, acc_hi)):
            cols = slice(off, off + LANES)
            o_ref[:, cols] = x_ref[:, cols] + g2[:, cols] * (shared[:, cols] + acc)


def ffn_out(hp, ys, dest, w_tok, wsg, wsu, wsd, x2, g2, *, seq, tm=256):
    T, D = x2.shape
    tm = min(tm, seq)
    bpb = seq // tm
    s8 = D // (2 * LANES)
    return pl.pallas_call(
        functools.partial(_ffn_out_kernel, tm=tm, s8=s8),
        out_shape=jax.ShapeDtypeStruct((T, D), F32),
        grid=(T // tm,),
        in_specs=[
            pl.BlockSpec((TOP_K, tm), lambda i: (0, i), memory_space=pltpu.SMEM),
            pl.BlockSpec((tm * s8, LANES), lambda i: (i, 0)),
            pl.BlockSpec((tm, LANES), lambda i: (i, 0)),
            _resident(wsg.shape), _resident(wsu.shape), _resident(wsd.shape),
            pl.BlockSpec((tm, D), lambda i: (i, 0)),
            pl.BlockSpec((1, 1, D), lambda i: (i // bpb, 0, 0)),
            pl.BlockSpec(memory_space=pl.ANY),
        ],
        out_specs=pl.BlockSpec((tm, D), lambda i: (i, 0)),
        scratch_shapes=[pltpu.VMEM((TOP_K, tm * s8, LANES), jnp.uint32), pltpu.SemaphoreType.DMA],
        compiler_params=_cparams(("arbitrary",)),
        name="ffn_out",
    )(dest, hp, w_tok, wsg, wsu, wsd, x2, g2, ys)


def block_plan(counts, *, blk, n_slots):
    padded = ((counts + blk - 1) // blk) * blk
    pends = jnp.cumsum(padded)
    pstarts = (pends - padded).astype(jnp.int32)
    nblk = -(-n_slots // blk) + N_EXPERTS
    first_row = jnp.arange(nblk, dtype=pends.dtype) * blk
    block_e = jnp.sum(pends[None, :] <= first_row[:, None], axis=1)
    block_e = jnp.minimum(block_e, N_EXPERTS - 1).astype(jnp.int32)
    n_used = (pends[-1] // blk).astype(jnp.int32)
    tail = n_used + jnp.arange(N_EXPERTS, dtype=jnp.int32)
    zrow = jnp.concatenate([pends - blk, jnp.minimum(tail, nblk - 1) * blk]).astype(jnp.int32)
    zon = jnp.concatenate([counts > 0, tail < nblk]).astype(jnp.int32)
    new_run = jnp.concatenate([jnp.ones((1,), jnp.int32),
                               (block_e[1:] != block_e[:-1]).astype(jnp.int32)])
    run_slot = ((jnp.cumsum(new_run) - 1) % 2).astype(jnp.int32)
    ids = jnp.arange(N_EXPERTS, dtype=jnp.int32)
    later = (ids[None, :] > ids[:, None]) & (counts[None, :] > 0)
    next_expert = jnp.min(jnp.where(later, ids[None, :], N_EXPERTS), axis=1)
    next_expert = jnp.where(next_expert < N_EXPERTS, next_expert, -1).astype(jnp.int32)
    return pstarts, block_e, n_used.reshape(1), zrow, zon, run_slot, next_expert, nblk


def _layer_weights(l, D, norm1_g, norm2_g, w_in, b_fgt, fox_qn_g, fox_kn_g, diff_qn_g, diff_kn_g,
                   diff_out_g, w_branch, w_out, w_router, w_exp_gate, w_exp_up, w_exp_down,
                   w_sh_gate, w_sh_up, w_sh_down):
    W = ATTN_WIDTH
    fcol = 3 * W
    w = w_in[l]
    dcol = fcol + N_FOX_HEADS
    w_cat = jnp.concatenate([w[:, :2 * W], w[:, dcol:dcol + 2 * W], w[:, dcol + 3 * W:],
                             w[:, 2 * W:fcol], w[:, dcol + 2 * W:dcol + 3 * W]], axis=1).astype(BF16)
    w_ff = jnp.zeros((D, LANES), BF16).at[:, :N_FOX_HEADS].set(
        w[:, fcol:fcol + N_FOX_HEADS].astype(BF16))
    qscale = HEAD_DIM ** -0.5 * LOG2E
    colgain = jnp.concatenate([
        jnp.tile(fox_qn_g[l].astype(F32) * qscale, N_FOX_HEADS),
        jnp.tile(fox_kn_g[l].astype(F32), N_FOX_HEADS),
        jnp.tile(diff_qn_g[l].astype(F32) * qscale, 2 * N_DIFF_HEADS),
        jnp.tile(diff_kn_g[l].astype(F32), 2 * N_DIFF_HEADS),
        jnp.ones((2 * D + 2 * W,), F32),
    ]).reshape(1, -1)
    return dict(
        w_cat=w_cat, w_ff=w_ff, colgain=colgain,
        wb0=w_branch[l, 0].astype(BF16), wb1=w_branch[l, 1].astype(BF16), wo=w_out[l].astype(BF16),
        wr_t=w_router[l].T.astype(BF16),
        wsg=w_sh_gate[l].astype(BF16), wsu=w_sh_up[l].astype(BF16), wsd=w_sh_down[l].astype(BF16),
    )


def kernel(x, c, positions, norm1_g, norm2_g, w_ada, b_ada, w_in, b_fgt, fox_qn_g, fox_kn_g,
           diff_qn_g, diff_kn_g, lam_q1, lam_k1, lam_q2, lam_k2, diff_out_g, w_branch, w_out,
           w_router, b_router, w_exp_gate, w_exp_up, w_exp_down, w_sh_gate, w_sh_up, w_sh_down):
    B, S, D = x.shape
    L = w_ada.shape[0]
    T = B * S
    blk = 512
    cos_t, sin_t = rope_tables(positions)
    mod = adaln(c, w_ada, b_ada)
    x2 = x.reshape(T, D)
    for l in range(L):
        lw = _layer_weights(l, D, norm1_g, norm2_g, w_in, b_fgt, fox_qn_g, fox_kn_g, diff_qn_g,
                            diff_kn_g, diff_out_g, w_branch, w_out, w_router, w_exp_gate,
                            w_exp_up, w_exp_down, w_sh_gate, w_sh_up, w_sh_down)
        sh1, sc1, g1, sh2, sc2, g2 = [mod[l, :, k * D:(k + 1) * D].reshape(B, 1, D) for k in range(6)]
        lam_init = 0.8 - 0.6 * math.exp(-0.3 * l)

        proj, vt, ff = inproj(x2, norm1_g[l].reshape(1, D), sh1, sc1, lw['w_cat'], lw['w_ff'],
                              lw['colgain'], cos_t, sin_t, seq=S)
        k_aug = fgate_keys(ff, b_fgt[l], proj, batch=B, seq=S)
        proj3 = proj.reshape(B, S, -1)
        o_fox = fox_attention(proj3, k_aug, vt).reshape(T, ATTN_WIDTH)
        lam_vecs = [v[l].reshape(1, HEAD_DIM).astype(F32) for v in (lam_q1, lam_k1, lam_q2, lam_k2)]
        o_diff = diff_attention(proj3, vt, lam_vecs, diff_out_g[l].reshape(-1, 1).astype(F32),
                                lam_init=lam_init).reshape(T, ATTN_WIDTH)
        x2 = post_attention(o_fox, o_diff, proj, lw['wb0'], lw['wb1'], lw['wo'], x2, g1, seq=S)

        hp, top_e, w_tok = router(x2, norm2_g[l].reshape(1, D), sh2, sc2, lw['wr_t'], b_router[l], seq=S)
        rank, counts = slot_rank(top_e)
        counts = counts[:, 0]
        pstarts, block_e, n_used, zrow, zon, run_slot, next_expert, nblk = block_plan(
            counts, blk=blk, n_slots=T * TOP_K)
        dest = slot_dest(pstarts, top_e, rank)
        xs = dispatch(hp, dest, zrow, zon, n_rows=nblk * blk, blk=blk)
        ys = experts(xs, block_e, n_used, run_slot, next_expert, w_exp_gate, w_exp_up, w_exp_down,
                     layer=l, blk=blk)
        x2 = ffn_out(hp, ys, dest, w_tok, lw['wsg'], lw['wsu'], lw['wsd'], x2, g2, seq=S)
    return x2.reshape(B, S, D)
```

```python
import functools
import math

import jax
import jax.numpy as jnp
import numpy as np
from jax import lax
from jax.experimental import pallas as pl
from jax.experimental.pallas import tpu as pltpu

F32 = jnp.float32
BF16 = jnp.bfloat16

HEAD_DIM = 128
N_FOX_HEADS = 8
N_DIFF_HEADS = 4
ATTN_WIDTH = N_FOX_HEADS * HEAD_DIM
CHUNK = 64
ROT_DIM = HEAD_DIM // 4
ROPE_THETA = 500000.0
N_EXPERTS = 64
N_GROUPS = 8
GROUP_SIZE = N_EXPERTS // N_GROUPS
TOPK_GROUPS = 4
TOP_K = 8
ROUTED_SCALE = 2.5
RMS_EPS = 1e-6
LANES = 128
V7X_VMEM_BYTES = 64 * 1024 * 1024
VMEM_LIMIT = V7X_VMEM_BYTES - 8 * 1024 * 1024

SEC_FQ, SEC_FK, SEC_DQ, SEC_DK, SEC_GATE = range(5)
LOG2E = math.log2(math.e)
BIAS_LANES = 3


def _cparams(sem):
    return pltpu.CompilerParams(dimension_semantics=sem, vmem_limit_bytes=VMEM_LIMIT)


def _rms(y, axis=-1):
    return y * lax.rsqrt(jnp.mean(y * y, axis=axis, keepdims=True) + RMS_EPS)


def _sigmoid(z):
    return 1.0 / (1.0 + jnp.exp(-z))


def _adaln_kernel(c_ref, w_ref, b_ref, o_ref):
    c = c_ref[...]
    cond = (c * _sigmoid(c)).astype(BF16)
    o_ref[0] = jnp.dot(cond, w_ref[0].astype(BF16), preferred_element_type=F32) + b_ref[0]


def adaln(c, w_ada, b_ada, *, tn=1024):
    L, D, N = w_ada.shape
    B = c.shape[0]
    tn = min(tn, N)
    return pl.pallas_call(
        _adaln_kernel,
        out_shape=jax.ShapeDtypeStruct((L, B, N), F32),
        grid=(L, N // tn),
        in_specs=[
            pl.BlockSpec((B, D), lambda l, j: (0, 0)),
            pl.BlockSpec((1, D, tn), lambda l, j: (l, 0, j)),
            pl.BlockSpec((1, 1, tn), lambda l, j: (l, 0, j)),
        ],
        out_specs=pl.BlockSpec((1, B, tn), lambda l, j: (l, 0, j)),
        compiler_params=_cparams(("arbitrary", "arbitrary")),
        name="adaln",
    )(c, w_ada, b_ada.reshape(L, 1, N))


def _rope_kernel(pos_ref, invf_ref, cos_ref, sin_ref):
    ang = pos_ref[...].astype(F32) * invf_ref[...]
    lane = lax.broadcasted_iota(jnp.int32, ang.shape, 1)
    c = jnp.cos(ang)
    s = jnp.sin(ang)
    cos_ref[...] = jnp.where(lane < ROT_DIM, c, 1.0)
    sin_ref[...] = jnp.where(lane < ROT_DIM // 2, -s, jnp.where(lane < ROT_DIM, s, 0.0))


def rope_tables(positions, *, tm=1024):
    T = positions.size
    tm = min(tm, T)
    half = ROT_DIM // 2
    inv_freq = ROPE_THETA ** (-jnp.arange(0, ROT_DIM, 2, dtype=F32) / ROT_DIM)
    invf = jnp.zeros((1, LANES), F32).at[0, :half].set(inv_freq).at[0, half:ROT_DIM].set(inv_freq)
    return pl.pallas_call(
        _rope_kernel,
        out_shape=(jax.ShapeDtypeStruct((T, LANES), F32),) * 2,
        grid=(T // tm,),
        in_specs=[
            pl.BlockSpec((tm, 1), lambda i: (i, 0)),
            pl.BlockSpec((1, LANES), lambda i: (0, 0)),
        ],
        out_specs=(pl.BlockSpec((tm, LANES), lambda i: (i, 0)),) * 2,
        compiler_params=_cparams(("arbitrary",)),
        name="rope_tables",
    )(positions.reshape(T, 1), invf)


def _inproj_kernel(x_ref, g_ref, sh_ref, sc_ref, w_ref, wff_ref, cg_ref, cos_ref, sin_ref,
                   o_ref, vt_ref, ff_ref, h_scr, *, sec_v):
    j = pl.program_id(1)

    @pl.when(j == 0)
    def _():
        y = _rms(x_ref[...]) * g_ref[...]
        h = (y * (1.0 + sc_ref[0]) + sh_ref[0]).astype(BF16)
        h_scr[...] = h
        ff_ref[...] = jnp.dot(h, wff_ref[...], preferred_element_type=F32)

    W = ATTN_WIDTH
    n_heads = W // HEAD_DIM

    def dots():
        h = h_scr[...]
        return [jnp.dot(h, w_ref[:, s * W:(s + 1) * W], preferred_element_type=F32) for s in range(2)]

    def normed_heads(acc, s):
        for hd in range(n_heads):
            cols = slice(s * W + hd * HEAD_DIM, s * W + (hd + 1) * HEAD_DIM)
            yield cols, _rms(acc[:, hd * HEAD_DIM:(hd + 1) * HEAD_DIM]) * cg_ref[:, cols]

    @pl.when(j == 0)
    def _():
        for s, acc in enumerate(dots()):
            for cols, y in normed_heads(acc, s):
                o_ref[:, cols] = y.astype(o_ref.dtype)

    @pl.when(j == 1)
    def _():
        cos_t = cos_ref[...]
        sin_t = sin_ref[...]
        low = lax.broadcasted_iota(jnp.int32, cos_t.shape, 1) < ROT_DIM // 2
        for s, acc in enumerate(dots()):
            for cols, y in normed_heads(acc, s):
                rot = jnp.where(low, pltpu.roll(y, HEAD_DIM - ROT_DIM // 2, 1),
                                pltpu.roll(y, ROT_DIM // 2, 1))
                o_ref[:, cols] = (y * cos_t + rot * sin_t).astype(o_ref.dtype)

    @pl.when((j >= 2) & (j < sec_v))
    def _():
        for s, acc in enumerate(dots()):
            o_ref[:, s * W:(s + 1) * W] = _sigmoid(acc).astype(o_ref.dtype)

    @pl.when(j >= sec_v)
    def _():
        for s, acc in enumerate(dots()):
            vt_ref[s * W:(s + 1) * W, :] = acc.T.astype(vt_ref.dtype)


def inproj(x2, g, shift, scale, w_cat, w_ff, colgain, cos_t, sin_t, *, seq, tm=512):
    T, D = x2.shape
    NC = w_cat.shape[1]
    tn = 2 * ATTN_WIDTH
    tm = min(tm, seq)
    bpb = seq // tm
    sec_v = NC // tn - 1
    return pl.pallas_call(
        functools.partial(_inproj_kernel, sec_v=sec_v),
        out_shape=(jax.ShapeDtypeStruct((T, sec_v * tn), BF16),
                   jax.ShapeDtypeStruct((tn, T), BF16),
                   jax.ShapeDtypeStruct((T, LANES), F32)),
        grid=(T // tm, NC // tn),
        in_specs=[
            pl.BlockSpec((tm, D), lambda i, j: (i, 0)),
            pl.BlockSpec((1, D), lambda i, j: (0, 0)),
            pl.BlockSpec((1, 1, D), lambda i, j: (i // bpb, 0, 0)),
            pl.BlockSpec((1, 1, D), lambda i, j: (i // bpb, 0, 0)),
            pl.BlockSpec((D, tn), lambda i, j: (0, j)),
            pl.BlockSpec((D, LANES), lambda i, j: (0, 0)),
            pl.BlockSpec((1, tn), lambda i, j: (0, j)),
            pl.BlockSpec((tm, LANES), lambda i, j: (i, 0)),
            pl.BlockSpec((tm, LANES), lambda i, j: (i, 0)),
        ],
        out_specs=(
            pl.BlockSpec((tm, tn), lambda i, j: (i, jnp.minimum(j, sec_v - 1))),
            pl.BlockSpec((tn, tm), lambda i, j: (0, i)),
            pl.BlockSpec((tm, LANES), lambda i, j: (i, 0)),
        ),
        scratch_shapes=[pltpu.VMEM((tm, D), BF16)],
        compiler_params=_cparams(("arbitrary", "arbitrary")),
        name="inproj",
    )(x2, g, shift, scale, w_cat, w_ff, colgain, cos_t, sin_t)


def _split3(v):
    hi = v.astype(BF16)
    r = v - hi.astype(F32)
    mid = r.astype(BF16)
    lo = (r - mid.astype(F32)).astype(BF16)
    return hi, mid, lo


def _fgate_kernel(ff_ref, b_ref, k_ref, o_ref, carry_scr):
    @pl.when(pl.program_id(1) == 0)
    def _():
        carry_scr[...] = jnp.zeros_like(carry_scr)

    z = ff_ref[...] + b_ref[...]
    logf = jnp.minimum(z, 0.0) - jnp.log(1.0 + jnp.exp(-jnp.abs(z)))
    tm = z.shape[0]
    row = lax.broadcasted_iota(jnp.int32, (tm, tm), 0)
    col = lax.broadcasted_iota(jnp.int32, (tm, tm), 1)
    tri = jnp.where(row >= col, 1.0, 0.0).astype(BF16)
    cum = carry_scr[...]
    for part in _split3(logf):
        cum = cum + jnp.dot(tri, part, preferred_element_type=F32)
    carry_scr[...] = cum[tm - 1:tm, :]
    bias = cum * (-LOG2E)
    lane = lax.broadcasted_iota(jnp.int32, (tm, HEAD_DIM), 1)
    for hd in range(N_FOX_HEADS):
        hi, mid, lo = [p.astype(F32) for p in _split3(bias[:, hd:hd + 1])]
        ext = jnp.where(lane == 0, hi, jnp.where(lane == 1, mid, jnp.where(lane == 2, lo, 0.0)))
        o_ref[0, hd, :, :HEAD_DIM] = k_ref[:, hd * HEAD_DIM:(hd + 1) * HEAD_DIM]
        o_ref[0, hd, :, HEAD_DIM:] = ext.astype(o_ref.dtype)


def fgate_keys(ff, b_fgt, proj, *, batch, seq, tm=512):
    tm = min(tm, seq)
    b_pad = jnp.zeros((1, LANES), F32).at[0, :N_FOX_HEADS].set(b_fgt.astype(F32))
    nb = seq // tm
    return pl.pallas_call(
        _fgate_kernel,
        out_shape=jax.ShapeDtypeStruct((batch, N_FOX_HEADS, seq, 2 * HEAD_DIM), BF16),
        grid=(batch, nb),
        in_specs=[
            pl.BlockSpec((tm, LANES), lambda b, i: (b * nb + i, 0)),
            pl.BlockSpec((1, LANES), lambda b, i: (0, 0)),
            pl.BlockSpec((tm, ATTN_WIDTH), lambda b, i: (b * nb + i, SEC_FK)),
        ],
        out_specs=pl.BlockSpec((1, N_FOX_HEADS, tm, 2 * HEAD_DIM), lambda b, i: (b, 0, i, 0)),
        scratch_shapes=[pltpu.VMEM((1, LANES), F32)],
        compiler_params=_cparams(("arbitrary", "arbitrary")),
        name="fgate",
    )(ff, b_pad, proj)


_NT = (((1,), (1,)), ((), ()))
ONES_ROWS = 16


def _flash_step(sT, vT, m_ref, acc_ref):
    m_old = m_ref[...]
    m_new = jnp.maximum(m_old, jnp.max(sT, axis=0, keepdims=True))
    alpha = jnp.exp2(m_old - m_new)
    pT = jnp.exp2(sT - m_new).astype(vT.dtype)
    acc_ref[...] = alpha * acc_ref[...] + jnp.dot(vT, pT, preferred_element_type=F32)
    m_ref[...] = m_new


def _kv_schedule(i, qk, spv):
    @pl.when(i == 0)
    def _():
        qk(0, True, 0)
        spv(0, 0)

    @pl.when(i > 0)
    def _():
        qk(0, False, 0)

        def pair(jj, carry):
            qk(2 * jj + 1, False, 1)
            spv(0, 2 * jj)
            qk(2 * jj + 2, False, 0)
            spv(1, 2 * jj + 1)
            return carry

        lax.fori_loop(0, (i - 1) // 2, pair, 0)

        @pl.when(i % 2 == 1)
        def _():
            qk(i, True, 1)
            spv(0, i - 1)
            spv(1, i)

        @pl.when(i % 2 == 0)
        def _():
            qk(i - 1, False, 1)
            spv(0, i - 2)
            qk(i, True, 0)
            spv(1, i - 1)
            spv(0, i)


def _with_ones(vT):
    return jnp.concatenate([vT, jnp.ones((ONES_ROWS, vT.shape[1]), vT.dtype)], axis=0)


FOX_HEADS_PER_STEP = 2


def _fox_kernel(q_ref, k_ref, vt_ref, o_ref, *scratch, tq):
    i = pl.program_id(2)
    hps = FOX_HEADS_PER_STEP
    heads = [scratch[4 * hh:4 * hh + 4] for hh in range(hps)]
    lane = lax.broadcasted_iota(jnp.int32, (tq, HEAD_DIM), 1)
    ones_ext = jnp.where(lane < BIAS_LANES, 1.0, 0.0).astype(q_ref.dtype)
    qs = []
    for hh, (m_scr, acc_scr, _, _) in enumerate(heads):
        qs.append(jnp.concatenate([q_ref[0, :, hh * HEAD_DIM:(hh + 1) * HEAD_DIM], ones_ext], axis=1))
        m_scr[...] = jnp.full_like(m_scr, -jnp.inf)
        acc_scr[...] = jnp.zeros_like(acc_scr)

    def qk(j, masked, slot):
        ks = pl.multiple_of(j * tq, tq)
        for hh in range(hps):
            k = k_ref[0, hh, pl.ds(ks, tq), :]
            sT = lax.dot_general(k, qs[hh], _NT, preferred_element_type=F32)
            if masked:
                key = lax.broadcasted_iota(jnp.int32, sT.shape, 0)
                qry = lax.broadcasted_iota(jnp.int32, sT.shape, 1)
                sT = jnp.where(key <= qry, sT, -jnp.inf)
            heads[hh][2 + slot][...] = sT

    def spv(slot, j):
        ks = pl.multiple_of(j * tq, tq)
        for hh, (m_scr, acc_scr, _, _) in enumerate(heads):
            vT = _with_ones(vt_ref[hh * HEAD_DIM:(hh + 1) * HEAD_DIM, pl.ds(ks, tq)])
            _flash_step(heads[hh][2 + slot][...], vT, m_scr, acc_scr)

    _kv_schedule(i, qk, spv)
    for hh, (_, acc_scr, _, _) in enumerate(heads):
        o = acc_scr[:HEAD_DIM, :] / acc_scr[HEAD_DIM:HEAD_DIM + 1, :]
        o_ref[0, :, hh * HEAD_DIM:(hh + 1) * HEAD_DIM] = o.T.astype(o_ref.dtype)


def fox_attention(proj3, k_aug, vt, *, tq=512):
    B, S, _ = proj3.shape
    tq = min(tq, S)
    hps = FOX_HEADS_PER_STEP
    G = N_FOX_HEADS // hps
    per_head = [pltpu.VMEM((1, tq), F32), pltpu.VMEM((HEAD_DIM + ONES_ROWS, tq), F32),
                pltpu.VMEM((tq, tq), F32), pltpu.VMEM((tq, tq), F32)]
    return pl.pallas_call(
        functools.partial(_fox_kernel, tq=tq),
        out_shape=jax.ShapeDtypeStruct((B, S, ATTN_WIDTH), BF16),
        grid=(B, G, S // tq),
        in_specs=[
            pl.BlockSpec((1, tq, hps * HEAD_DIM), lambda b, g, i: (b, i, SEC_FQ * G + g)),
            pl.BlockSpec((1, hps, S, 2 * HEAD_DIM), lambda b, g, i: (b, g, 0, 0)),
            pl.BlockSpec((hps * HEAD_DIM, S), lambda b, g, i: (g, b)),
        ],
        out_specs=pl.BlockSpec((1, tq, hps * HEAD_DIM), lambda b, g, i: (b, i, g)),
        scratch_shapes=per_head * hps,
        compiler_params=_cparams(("arbitrary", "arbitrary", "arbitrary")),
        name="fox_attn",
    )(proj3, k_aug, vt)


DIFF_HEADS_PER_STEP = 2


def _diff_kernel(q_ref, k_ref, vt_ref, lq1_ref, lk1_ref, lq2_ref, lk2_ref, g_ref, o_ref,
                 *scratch, tq, lam_init):
    i = pl.program_id(2)
    dv = 2 * HEAD_DIM
    hps = DIFF_HEADS_PER_STEP
    chains = [[scratch[4 * (2 * hh + mp):4 * (2 * hh + mp) + 4] for mp in range(2)] for hh in range(hps)]
    for hh in range(hps):
        for m, a, _, _ in chains[hh]:
            m[...] = jnp.full_like(m, -jnp.inf)
            a[...] = jnp.zeros_like(a)

    def qk(j, masked, slot):
        ks = pl.multiple_of(j * tq, tq)
        for hh in range(hps):
            for mp in range(2):
                cols = slice((2 * hh + mp) * HEAD_DIM, (2 * hh + mp + 1) * HEAD_DIM)
                sT = lax.dot_general(k_ref[0, pl.ds(ks, tq), cols], q_ref[0, :, cols], _NT,
                                     preferred_element_type=F32)
                if masked:
                    key = lax.broadcasted_iota(jnp.int32, sT.shape, 0)
                    qry = lax.broadcasted_iota(jnp.int32, sT.shape, 1)
                    sT = jnp.where(key // CHUNK <= qry // CHUNK, sT, -jnp.inf)
                chains[hh][mp][2 + slot][...] = sT

    def spv(slot, j):
        ks = pl.multiple_of(j * tq, tq)
        for hh in range(hps):
            vT = _with_ones(vt_ref[hh * dv:(hh + 1) * dv, pl.ds(ks, tq)])
            for m, a, *bufs in chains[hh]:
                _flash_step(bufs[slot][...], vT, m, a)

    _kv_schedule(i, qk, spv)
    lam = (jnp.exp(jnp.sum(lq1_ref[...] * lk1_ref[...], axis=-1, keepdims=True))
           - jnp.exp(jnp.sum(lq2_ref[...] * lk2_ref[...], axis=-1, keepdims=True)) + lam_init)
    for hh in range(hps):
        (_, a1, _, _), (_, a2, _, _) = chains[hh]
        o = a1[:dv, :] / a1[dv:dv + 1, :] - lam * (a2[:dv, :] / a2[dv:dv + 1, :])
        o = _rms(o, axis=0) * g_ref[...] * (1.0 - lam_init)
        o_ref[0, :, hh * dv:(hh + 1) * dv] = o.T.astype(o_ref.dtype)


def diff_attention(proj3, vt, lam_vecs, out_gain, *, lam_init, tq=512):
    B, S, _ = proj3.shape
    tq = min(tq, S)
    hps = DIFF_HEADS_PER_STEP
    G = N_DIFF_HEADS // hps
    dv = 2 * HEAD_DIM
    vec = pl.BlockSpec((1, HEAD_DIM), lambda b, g, i: (0, 0))
    per_chain = [pltpu.VMEM((1, tq), F32), pltpu.VMEM((dv + ONES_ROWS, tq), F32),
                 pltpu.VMEM((tq, tq), F32), pltpu.VMEM((tq, tq), F32)]
    return pl.pallas_call(
        functools.partial(_diff_kernel, tq=tq, lam_init=lam_init),
        out_shape=jax.ShapeDtypeStruct((B, S, ATTN_WIDTH), BF16),
        grid=(B, G, S // tq),
        in_specs=[
            pl.BlockSpec((1, tq, hps * dv), lambda b, g, i: (b, i, SEC_DQ * G + g)),
            pl.BlockSpec((1, S, hps * dv), lambda b, g, i: (b, 0, SEC_DK * G + g)),
            pl.BlockSpec((hps * dv, S), lambda b, g, i: (G + g, b)),
            vec, vec, vec, vec,
            pl.BlockSpec((dv, 1), lambda b, g, i: (0, 0)),
        ],
        out_specs=pl.BlockSpec((1, tq, hps * dv), lambda b, g, i: (b, i, g)),
        scratch_shapes=per_chain * (2 * hps),
        compiler_params=_cparams(("arbitrary", "arbitrary", "arbitrary")),
        name="diff_attn",
    )(proj3, proj3, vt, *lam_vecs, out_gain)


def _post_kernel(of_ref, od_ref, ga_ref, gb_ref, wb0_ref, wb1_ref, wo_ref, x_ref, g1_ref, o_ref):
    y0 = jnp.dot(of_ref[...], wb0_ref[...], preferred_element_type=F32)
    y1 = jnp.dot(od_ref[...], wb1_ref[...], preferred_element_type=F32)
    merged = ga_ref[...].astype(F32) * y0 + gb_ref[...].astype(F32) * y1
    mix = jnp.dot(merged.astype(BF16), wo_ref[...], preferred_element_type=F32)
    o_ref[...] = x_ref[...] + g1_ref[0] * mix


def _resident(shape):
    nd = len(shape)
    return pl.BlockSpec(shape, lambda *_: (0,) * nd, pipeline_mode=pl.Buffered(1))


def post_attention(o_fox, o_diff, proj, wb0, wb1, wo, x2, g1, *, seq, tm=256):
    T, D = x2.shape
    tm = min(tm, seq)
    bpb = seq // tm
    gate_blk = SEC_GATE * ATTN_WIDTH // D
    return pl.pallas_call(
        _post_kernel,
        out_shape=jax.ShapeDtypeStruct((T, D), F32),
        grid=(T // tm,),
        in_specs=[
            pl.BlockSpec((tm, ATTN_WIDTH), lambda i: (i, 0)),
            pl.BlockSpec((tm, ATTN_WIDTH), lambda i: (i, 0)),
            pl.BlockSpec((tm, D), lambda i: (i, gate_blk)),
            pl.BlockSpec((tm, D), lambda i: (i, gate_blk + 1)),
            _resident(wb0.shape), _resident(wb1.shape), _resident(wo.shape),
            pl.BlockSpec((tm, D), lambda i: (i, 0)),
            pl.BlockSpec((1, 1, D), lambda i: (i // bpb, 0, 0)),
        ],
        out_specs=pl.BlockSpec((tm, D), lambda i: (i, 0)),
        compiler_params=_cparams(("arbitrary",)),
        name="post_attn",
    )(o_fox, o_diff, proj, proj, wb0, wb1, wo, x2, g1)


def _pack_halves(y):
    half = y.shape[1] // 2
    bits = lax.bitcast_convert_type(y.astype(BF16).astype(F32), jnp.uint32)
    return (bits[:, half:] & jnp.uint32(0xFFFF0000)) | (bits[:, :half] >> 16)


def _unpack_halves(w):
    lo = lax.bitcast_convert_type(w << 16, F32)
    hi = lax.bitcast_convert_type(w & jnp.uint32(0xFFFF0000), F32)
    return lo, hi


def _store_token_major(ref, words, s8):
    m = words.shape[0]
    for s in range(s8):
        ref[pl.ds(s, m, stride=s8), :] = words[:, s * LANES:(s + 1) * LANES]


def _load_token_major(ref, m, s8):
    parts = [_unpack_halves(ref[pl.ds(s, m, stride=s8), :]) for s in range(s8)]
    return jnp.concatenate([p[0].astype(BF16) for p in parts] + [p[1].astype(BF16) for p in parts],
                           axis=1)


def _router_kernel(x_ref, g_ref, sh_ref, sc_ref, wr_ref, br_ref, h_ref, e_ref, w_ref, *, s8):
    y = _rms(x_ref[...]) * g_ref[...]
    h = y * (1.0 + sc_ref[0]) + sh_ref[0]
    hb = h.astype(BF16)
    _store_token_major(h_ref, _pack_halves(h), s8)
    logits = lax.dot_general(wr_ref[...], hb, _NT, preferred_element_type=F32)
    scores = _sigmoid(logits)
    tm = scores.shape[1]
    sc3 = scores.reshape(N_GROUPS, GROUP_SIZE, tm)
    sel = sc3 + br_ref[...]
    neg = -jnp.inf
    member = lax.broadcasted_iota(jnp.int32, sel.shape, 1)
    group = lax.broadcasted_iota(jnp.int32, sel.shape, 0)

    m1 = jnp.max(sel, axis=1, keepdims=True)
    i1 = jnp.min(jnp.where(sel == m1, member, GROUP_SIZE), axis=1, keepdims=True)
    m2 = jnp.max(jnp.where(member == i1, neg, sel), axis=1, keepdims=True)
    gscore = m1 + m2

    gidx = lax.broadcasted_iota(jnp.int32, gscore.shape, 0)
    gkeep = jnp.zeros(gscore.shape, jnp.int32)
    for _ in range(TOPK_GROUPS):
        best = jnp.max(gscore, axis=0, keepdims=True)
        bi = jnp.min(jnp.where(gscore == best, gidx, N_GROUPS), axis=0, keepdims=True)
        hit = gidx == bi
        gkeep = jnp.where(hit, 1, gkeep)
        gscore = jnp.where(hit, neg, gscore)

    cand = jnp.where(gkeep > 0, sel, neg)
    eidx = group * GROUP_SIZE + member
    weights = []
    for k in range(TOP_K):
        best = jnp.max(jnp.max(cand, axis=1, keepdims=True), axis=0, keepdims=True)
        bi = jnp.min(jnp.min(jnp.where(cand == best, eidx, N_EXPERTS), axis=1, keepdims=True),
                     axis=0, keepdims=True)
        hit = eidx == bi
        e_ref[k:k + 1, :] = bi[0]
        weights.append(jnp.sum(jnp.sum(jnp.where(hit, sc3, 0.0), axis=1, keepdims=True),
                               axis=0, keepdims=True)[0])
        cand = jnp.where(hit, neg, cand)
    wsum = functools.reduce(lambda a, b: a + b, weights)
    row = lax.broadcasted_iota(jnp.int32, (LANES, tm), 0)
    wmat = jnp.zeros((LANES, tm), F32)
    for k in range(TOP_K):
        wmat = jnp.where(row == k, weights[k] / wsum * ROUTED_SCALE, wmat)
    w_ref[...] = wmat.T


def router(x2, g, shift, scale, wr_t, b_router, *, seq, tm=512):
    T, D = x2.shape
    tm = min(tm, seq)
    bpb = seq // tm
    s8 = D // (2 * LANES)
    return pl.pallas_call(
        functools.partial(_router_kernel, s8=s8),
        out_shape=(jax.ShapeDtypeStruct((T * s8, LANES), jnp.uint32),
                   jax.ShapeDtypeStruct((TOP_K, T), jnp.int32),
                   jax.ShapeDtypeStruct((T, LANES), F32)),
        grid=(T // tm,),
        in_specs=[
            pl.BlockSpec((tm, D), lambda i: (i, 0)),
            pl.BlockSpec((1, D), lambda i: (0, 0)),
            pl.BlockSpec((1, 1, D), lambda i: (i // bpb, 0, 0)),
            pl.BlockSpec((1, 1, D), lambda i: (i // bpb, 0, 0)),
            pl.BlockSpec((N_EXPERTS, D), lambda i: (0, 0)),
            pl.BlockSpec((N_GROUPS, GROUP_SIZE, 1), lambda i: (0, 0, 0)),
        ],
        out_specs=(pl.BlockSpec((tm * s8, LANES), lambda i: (i, 0)),
                   pl.BlockSpec((TOP_K, tm), lambda i: (0, i)),
                   pl.BlockSpec((tm, LANES), lambda i: (i, 0))),
        compiler_params=_cparams(("arbitrary",)),
        name="router",
    )(x2, g, shift, scale, wr_t, b_router.reshape(N_GROUPS, GROUP_SIZE, 1).astype(F32))


def _slot_rank_kernel(e_ref, rank_ref, cnt_ref, base_scr, tri_scr):
    tm = e_ref.shape[1]

    @pl.when(pl.program_id(0) == 0)
    def _():
        base_scr[...] = jnp.zeros_like(base_scr)
        r = lax.broadcasted_iota(jnp.int32, (tm, tm), 0)
        c = lax.broadcasted_iota(jnp.int32, (tm, tm), 1)
        tri_scr[...] = jnp.where(r <= c, 1.0, 0.0).astype(BF16)

    expert = lax.broadcasted_iota(jnp.int32, (N_EXPERTS, tm), 0)
    base = base_scr[...]
    for k in range(TOP_K):
        hot = expert == e_ref[k:k + 1, :]
        onehot = jnp.where(hot, 1.0, 0.0)
        incl = jnp.dot(onehot.astype(BF16), tri_scr[...], preferred_element_type=F32)
        rank = jnp.sum(jnp.where(hot, base + incl - 1.0, 0.0), axis=0, keepdims=True)
        rank_ref[k:k + 1, :] = rank.astype(jnp.int32)
        base = base + jnp.sum(onehot, axis=1, keepdims=True)
    base_scr[...] = base
    cnt_ref[...] = jnp.broadcast_to(base, cnt_ref.shape).astype(jnp.int32)


def slot_rank(top_e, *, tm=512):
    K, T = top_e.shape
    tm = min(tm, T)
    return pl.pallas_call(
        _slot_rank_kernel,
        out_shape=(jax.ShapeDtypeStruct((K, T), jnp.int32),
                   jax.ShapeDtypeStruct((N_EXPERTS, LANES), jnp.int32)),
        grid=(T // tm,),
        in_specs=[pl.BlockSpec((K, tm), lambda i: (0, i))],
        out_specs=(pl.BlockSpec((K, tm), lambda i: (0, i)),
                   pl.BlockSpec((N_EXPERTS, LANES), lambda i: (0, 0))),
        scratch_shapes=[pltpu.VMEM((N_EXPERTS, 1), F32), pltpu.VMEM((tm, tm), BF16)],
        compiler_params=_cparams(("arbitrary",)),
        name="slot_rank",
    )(top_e)


def _slot_dest_kernel(pstart_ref, e_ref, rank_ref, dest_ref):
    e = e_ref[...]
    dest = rank_ref[...]
    for x in range(N_EXPERTS):
        dest = dest + jnp.where(e == x, pstart_ref[x], 0)
    dest_ref[...] = dest


def slot_dest(pstarts, top_e, rank, *, tm=4096):
    K, T = top_e.shape
    tm = min(tm, T)
    blockspec = pl.BlockSpec((K, tm), lambda i, ps: (0, i))
    return pl.pallas_call(
        _slot_dest_kernel,
        out_shape=jax.ShapeDtypeStruct((K, T), jnp.int32),
        grid_spec=pltpu.PrefetchScalarGridSpec(
            num_scalar_prefetch=1, grid=(T // tm,),
            in_specs=[blockspec, blockspec], out_specs=blockspec),
        compiler_params=_cparams(("arbitrary",)),
        name="slot_dest",
    )(pstarts, top_e, rank)


def _dispatch_kernel(zrow_ref, zon_ref, dest_ref, h_ref, xs_ref, zero_scr, sem, zsem, *, tm, s8, blk):
    def zero_copy(x):
        row = pl.multiple_of(zrow_ref[x] * s8, 8)
        return pltpu.make_async_copy(zero_scr, xs_ref.at[pl.ds(row, blk * s8)], zsem)

    @pl.when(pl.program_id(0) == 0)
    def _():
        zero_scr[...] = jnp.zeros_like(zero_scr)

        def start(x, c):
            @pl.when(zon_ref[x] > 0)
            def _():
                zero_copy(x).start()
            return c

        def wait(x, c):
            @pl.when(zon_ref[x] > 0)
            def _():
                zero_copy(x).wait()
            return c

        lax.fori_loop(0, 2 * N_EXPERTS, start, 0)
        lax.fori_loop(0, 2 * N_EXPERTS, wait, 0)

    def issue(t, c):
        src = h_ref.at[pl.ds(pl.multiple_of(t * s8, s8), s8)]
        for k in range(TOP_K):
            row = pl.multiple_of(dest_ref[k, t] * s8, s8)
            pltpu.make_async_copy(src, xs_ref.at[pl.ds(row, s8)], sem).start(priority=k % 2)
        return c

    lax.fori_loop(0, tm, issue, 0)
    for _ in range(TOP_K):
        pltpu.make_async_copy(h_ref, xs_ref.at[pl.ds(0, tm * s8)], sem).wait()


def dispatch(hp, dest, zrow, zon, *, n_rows, blk, tm=512):
    K, T = dest.shape
    s8 = hp.shape[0] // T
    tm = min(tm, T)
    return pl.pallas_call(
        functools.partial(_dispatch_kernel, tm=tm, s8=s8, blk=blk),
        out_shape=jax.ShapeDtypeStruct((n_rows * s8, LANES), jnp.uint32),
        grid_spec=pltpu.PrefetchScalarGridSpec(
            num_scalar_prefetch=2, grid=(T // tm,),
            in_specs=[
                pl.BlockSpec((K, tm), lambda i, zr, zc: (0, i), memory_space=pltpu.SMEM),
                pl.BlockSpec((tm * s8, LANES), lambda i, zr, zc: (i, 0)),
            ],
            out_specs=pl.BlockSpec(memory_space=pl.ANY),
            scratch_shapes=[pltpu.VMEM((blk * s8, LANES), jnp.uint32),
                            pltpu.SemaphoreType.DMA, pltpu.SemaphoreType.DMA],
        ),
        compiler_params=_cparams(("arbitrary",)),
        name="dispatch",
    )(zrow, zon, dest, hp)


def _expert_kernel(be_ref, nb_ref, slot_ref, nxt_ref, x_ref, wg_hbm, wu_hbm, wd_hbm, o_ref,
                   fg, fu, fd, wg_s, wu_s, wd_s, sems, *, layer, blk, s8):
    b = pl.program_id(0)
    e = be_ref[b]
    live = b < nb_ref[0]
    first = (b == 0) | (e != be_ref[jnp.maximum(b - 1, 0)])

    def fetch(expert, slot):
        return [pltpu.make_async_copy(w.at[layer, expert], f.at[slot], sems.at[slot])
                for w, f in ((wg_hbm, fg), (wu_hbm, fu), (wd_hbm, fd))]

    @pl.when(live & (b == 0))
    def _():
        for c in fetch(e, 0):
            c.start()

    @pl.when(live & first)
    def _():
        slot = slot_ref[b]
        nxt = nxt_ref[e]

        @pl.when(nxt >= 0)
        def _():
            for c in fetch(nxt, 1 - slot):
                c.start()

        for c in fetch(e, slot):
            c.wait()
        wg_s[...] = fg[slot].astype(BF16)
        wu_s[...] = fu[slot].astype(BF16)
        wd_s[...] = fd[slot].astype(BF16)

    @pl.when(live)
    def _():
        x = _load_token_major(x_ref, blk, s8)
        a = jnp.dot(x, wg_s[...], preferred_element_type=F32)
        u = jnp.dot(x, wu_s[...], preferred_element_type=F32)
        hmid = (a * _sigmoid(a) * u).astype(BF16)
        y = jnp.dot(hmid, wd_s[...], preferred_element_type=F32)
        _store_token_major(o_ref, _pack_halves(y), s8)

    @pl.when(b >= nb_ref[0])
    def _():
        o_ref[...] = jnp.zeros_like(o_ref)


def experts(xs, block_e, n_used, run_slot, next_expert, wg, wu, wd, *, layer, blk):
    D, E = wg.shape[2], wg.shape[3]
    s8 = D // (2 * LANES)
    nblk = xs.shape[0] // (blk * s8)
    hbm = pl.BlockSpec(memory_space=pl.ANY)
    grid_spec = pltpu.PrefetchScalarGridSpec(
        num_scalar_prefetch=4,
        grid=(nblk,),
        in_specs=[
            pl.BlockSpec((blk * s8, LANES), lambda b, be, nb, sl, nx: (jnp.minimum(b, nb[0] - 1), 0)),
            hbm, hbm, hbm,
        ],
        out_specs=pl.BlockSpec((blk * s8, LANES), lambda b, be, nb, sl, nx: (b, 0)),
        scratch_shapes=[pltpu.VMEM((2, D, E), F32), pltpu.VMEM((2, D, E), F32), pltpu.VMEM((2, E, D), F32),
                        pltpu.VMEM((D, E), BF16), pltpu.VMEM((D, E), BF16), pltpu.VMEM((E, D), BF16),
                        pltpu.SemaphoreType.DMA((2,))],
    )
    return pl.pallas_call(
        functools.partial(_expert_kernel, layer=layer, blk=blk, s8=s8),
        out_shape=jax.ShapeDtypeStruct(xs.shape, jnp.uint32),
        grid_spec=grid_spec,
        compiler_params=_cparams(("arbitrary",)),
        name="experts",
    )(block_e, n_used, run_slot, next_expert, xs, wg, wu, wd)


def _ffn_out_kernel(dest_ref, h_ref, wt_ref, wg_ref, wu_ref, wd_ref, x_ref, g2_ref, ys_ref, o_ref,
                    buf, sem, *, tm, s8):
    def issue(t, c):
        for k in range(TOP_K):
            row = pl.multiple_of(dest_ref[k, t] * s8, s8)
            pltpu.make_async_copy(ys_ref.at[pl.ds(row, s8)],
                                  buf.at[k, pl.ds(pl.multiple_of(t * s8, s8), s8)],
                                  sem).start(priority=k % 2)
        return c

    lax.fori_loop(0, tm, issue, 0)

    h = _load_token_major(h_ref, tm, s8)
    a = jnp.dot(h, wg_ref[...], preferred_element_type=F32)
    u = jnp.dot(h, wu_ref[...], preferred_element_type=F32)
    hmid = (a * _sigmoid(a) * u).astype(BF16)
    shared = jnp.dot(hmid, wd_ref[...], preferred_element_type=F32)

    for k in range(TOP_K):
        pltpu.make_async_copy(ys_ref.at[pl.ds(0, tm * s8)], buf.at[k], sem).wait()

    half = s8 * LANES
    wts = [jnp.broadcast_to(wt_ref[:, k:k + 1], (tm, LANES)) for k in range(TOP_K)]
    g2 = g2_ref[0]
    for s in range(s8):
        acc_lo = jnp.zeros((tm, LANES), F32)
        acc_hi = jnp.zeros((tm, LANES), F32)
        for k in range(TOP_K):
            lo, hi = _unpack_halves(buf.at[k][pl.ds(s, tm, stride=s8), :])
            acc_lo = acc_lo + wts[k] * lo
            acc_hi = acc_hi + wts[k] * hi
        for off, acc in ((s * LANES, acc_lo), (half + s * LANES, acc_hi)):
            cols = slice(off, off + LANES)
            o_ref[:, cols] = x_ref[:, cols] + g2[:, cols] * (shared[:, cols] + acc)


def ffn_out(hp, ys, dest, w_tok, wsg, wsu, wsd, x2, g2, *, seq, tm=256):
    T, D = x2.shape
    tm = min(tm, seq)
    bpb = seq // tm
    s8 = D // (2 * LANES)
    return pl.pallas_call(
        functools.partial(_ffn_out_kernel, tm=tm, s8=s8),
        out_shape=jax.ShapeDtypeStruct((T, D), F32),
        grid=(T // tm,),
        in_specs=[
            pl.BlockSpec((TOP_K, tm), lambda i: (0, i), memory_space=pltpu.SMEM),
            pl.BlockSpec((tm * s8, LANES), lambda i: (i, 0)),
            pl.BlockSpec((tm, LANES), lambda i: (i, 0)),
            _resident(wsg.shape), _resident(wsu.shape), _resident(wsd.shape),
            pl.BlockSpec((tm, D), lambda i: (i, 0)),
            pl.BlockSpec((1, 1, D), lambda i: (i // bpb, 0, 0)),
            pl.BlockSpec(memory_space=pl.ANY),
        ],
        out_specs=pl.BlockSpec((tm, D), lambda i: (i, 0)),
        scratch_shapes=[pltpu.VMEM((TOP_K, tm * s8, LANES), jnp.uint32), pltpu.SemaphoreType.DMA],
        compiler_params=_cparams(("arbitrary",)),
        name="ffn_out",
    )(dest, hp, w_tok, wsg, wsu, wsd, x2, g2, ys)


def block_plan(counts, *, blk, n_slots):
    padded = ((counts + blk - 1) // blk) * blk
    pends = jnp.cumsum(padded)
    pstarts = (pends - padded).astype(jnp.int32)
    nblk = -(-n_slots // blk) + N_EXPERTS
    first_row = jnp.arange(nblk, dtype=pends.dtype) * blk
    block_e = jnp.sum(pends[None, :] <= first_row[:, None], axis=1)
    block_e = jnp.minimum(block_e, N_EXPERTS - 1).astype(jnp.int32)
    n_used = (pends[-1] // blk).astype(jnp.int32)
    tail = n_used + jnp.arange(N_EXPERTS, dtype=jnp.int32)
    zrow = jnp.concatenate([pends - blk, jnp.minimum(tail, nblk - 1) * blk]).astype(jnp.int32)
    zon = jnp.concatenate([counts > 0, tail < nblk]).astype(jnp.int32)
    new_run = jnp.concatenate([jnp.ones((1,), jnp.int32),
                               (block_e[1:] != block_e[:-1]).astype(jnp.int32)])
    run_slot = ((jnp.cumsum(new_run) - 1) % 2).astype(jnp.int32)
    ids = jnp.arange(N_EXPERTS, dtype=jnp.int32)
    later = (ids[None, :] > ids[:, None]) & (counts[None, :] > 0)
    next_expert = jnp.min(jnp.where(later, ids[None, :], N_EXPERTS), axis=1)
    next_expert = jnp.where(next_expert < N_EXPERTS, next_expert, -1).astype(jnp.int32)
    return pstarts, block_e, n_used.reshape(1), zrow, zon, run_slot, next_expert, nblk


def _layer_weights(l, D, norm1_g, norm2_g, w_in, b_fgt, fox_qn_g, fox_kn_g, diff_qn_g, diff_kn_g,
                   diff_out_g, w_branch, w_out, w_router, w_exp_gate, w_exp_up, w_exp_down,
                   w_sh_gate, w_sh_up, w_sh_down):
    W = ATTN_WIDTH
    fcol = 3 * W
    w = w_in[l]
    dcol = fcol + N_FOX_HEADS
    w_cat = jnp.concatenate([w[:, :2 * W], w[:, dcol:dcol + 2 * W], w[:, dcol + 3 * W:],
                             w[:, 2 * W:fcol], w[:, dcol + 2 * W:dcol + 3 * W]], axis=1).astype(BF16)
    w_ff = jnp.zeros((D, LANES), BF16).at[:, :N_FOX_HEADS].set(
        w[:, fcol:fcol + N_FOX_HEADS].astype(BF16))
    qscale = HEAD_DIM ** -0.5 * LOG2E
    colgain = jnp.concatenate([
        jnp.tile(fox_qn_g[l].astype(F32) * qscale, N_FOX_HEADS),
        jnp.tile(fox_kn_g[l].astype(F32), N_FOX_HEADS),
        jnp.tile(diff_qn_g[l].astype(F32) * qscale, 2 * N_DIFF_HEADS),
        jnp.tile(diff_kn_g[l].astype(F32), 2 * N_DIFF_HEADS),
        jnp.ones((2 * D + 2 * W,), F32),
    ]).reshape(1, -1)
    return dict(
        w_cat=w_cat, w_ff=w_ff, colgain=colgain,
        wb0=w_branch[l, 0].astype(BF16), wb1=w_branch[l, 1].astype(BF16), wo=w_out[l].astype(BF16),
        wr_t=w_router[l].T.astype(BF16),
        wsg=w_sh_gate[l].astype(BF16), wsu=w_sh_up[l].astype(BF16), wsd=w_sh_down[l].astype(BF16),
    )


def kernel(x, c, positions, norm1_g, norm2_g, w_ada, b_ada, w_in, b_fgt, fox_qn_g, fox_kn_g,
           diff_qn_g, diff_kn_g, lam_q1, lam_k1, lam_q2, lam_k2, diff_out_g, w_branch, w_out,
           w_router, b_router, w_exp_gate, w_exp_up, w_exp_down, w_sh_gate, w_sh_up, w_sh_down):
    B, S, D = x.shape
    L = w_ada.shape[0]
    T = B * S
    blk = 512
    cos_t, sin_t = rope_tables(positions)
    mod = adaln(c, w_ada, b_ada)
    x2 = x.reshape(T, D)
    for l in range(L):
        lw = _layer_weights(l, D, norm1_g, norm2_g, w_in, b_fgt, fox_qn_g, fox_kn_g, diff_qn_g,
                            diff_kn_g, diff_out_g, w_branch, w_out, w_router, w_exp_gate,
                            w_exp_up, w_exp_down, w_sh_gate, w_sh_up, w_sh_down)
        sh1, sc1, g1, sh2, sc2, g2 = [mod[l, :, k * D:(k + 1) * D].reshape(B, 1, D) for k in range(6)]
        lam_init = 0.8 - 0.6 * math.exp(-0.3 * l)

        proj, vt, ff = inproj(x2, norm1_g[l].reshape(1, D), sh1, sc1, lw['w_cat'], lw['w_ff'],
                              lw['colgain'], cos_t, sin_t, seq=S)
        k_aug = fgate_keys(ff, b_fgt[l], proj, batch=B, seq=S)
        proj3 = proj.reshape(B, S, -1)
        o_fox = fox_attention(proj3, k_aug, vt).reshape(T, ATTN_WIDTH)
        lam_vecs = [v[l].reshape(1, HEAD_DIM).astype(F32) for v in (lam_q1, lam_k1, lam_q2, lam_k2)]
        o_diff = diff_attention(proj3, vt, lam_vecs, diff_out_g[l].reshape(-1, 1).astype(F32),
                                lam_init=lam_init).reshape(T, ATTN_WIDTH)
        x2 = post_attention(o_fox, o_diff, proj, lw['wb0'], lw['wb1'], lw['wo'], x2, g1, seq=S)

        hp, top_e, w_tok = router(x2, norm2_g[l].reshape(1, D), sh2, sc2, lw['wr_t'], b_router[l], seq=S)
        rank, counts = slot_rank(top_e)
        counts = counts[:, 0]
        pstarts, block_e, n_used, zrow, zon, run_slot, next_expert, nblk = block_plan(
            counts, blk=blk, n_slots=T * TOP_K)
        dest = slot_dest(pstarts, top_e, rank)
        xs = dispatch(hp, dest, zrow, zon, n_rows=nblk * blk, blk=blk)
        ys = experts(xs, block_e, n_used, run_slot, next_expert, w_exp_gate, w_exp_up, w_exp_down,
                     layer=l, blk=blk)
        x2 = ffn_out(hp, ys, dest, w_tok, lw['wsg'], lw['wsu'], lw['wsd'], x2, g2, seq=S)
    return x2.reshape(B, S, D)
```

```python
import functools
import math

import jax
import jax.numpy as jnp
import numpy as np
from jax import lax
from jax.experimental import pallas as pl
from jax.experimental.pallas import tpu as pltpu

F32 = jnp.float32
BF16 = jnp.bfloat16

HEAD_DIM = 128
N_FOX_HEADS = 8
N_DIFF_HEADS = 4
ATTN_WIDTH = N_FOX_HEADS * HEAD_DIM
CHUNK = 64
ROT_DIM = HEAD_DIM // 4
ROPE_THETA = 500000.0
N_EXPERTS = 64
N_GROUPS = 8
GROUP_SIZE = N_EXPERTS // N_GROUPS
TOPK_GROUPS = 4
TOP_K = 8
ROUTED_SCALE = 2.5
RMS_EPS = 1e-6
LANES = 128
V7X_VMEM_BYTES = 64 * 1024 * 1024
VMEM_LIMIT = V7X_VMEM_BYTES - 8 * 1024 * 1024

SEC_FQ, SEC_FK, SEC_DQ, SEC_DK, SEC_GATE = range(5)
LOG2E = math.log2(math.e)
BIAS_LANES = 3


def _cparams(sem):
    return pltpu.CompilerParams(dimension_semantics=sem, vmem_limit_bytes=VMEM_LIMIT)


def _rms(y, axis=-1):
    return y * lax.rsqrt(jnp.mean(y * y, axis=axis, keepdims=True) + RMS_EPS)


def _sigmoid(z):
    return 1.0 / (1.0 + jnp.exp(-z))


def _adaln_kernel(c_ref, w_ref, b_ref, o_ref):
    c = c_ref[...]
    cond = (c * _sigmoid(c)).astype(BF16)
    o_ref[0] = jnp.dot(cond, w_ref[0].astype(BF16), preferred_element_type=F32) + b_ref[0]


def adaln(c, w_ada, b_ada, *, tn=1024):
    L, D, N = w_ada.shape
    B = c.shape[0]
    tn = min(tn, N)
    return pl.pallas_call(
        _adaln_kernel,
        out_shape=jax.ShapeDtypeStruct((L, B, N), F32),
        grid=(L, N // tn),
        in_specs=[
            pl.BlockSpec((B, D), lambda l, j: (0, 0)),
            pl.BlockSpec((1, D, tn), lambda l, j: (l, 0, j)),
            pl.BlockSpec((1, 1, tn), lambda l, j: (l, 0, j)),
        ],
        out_specs=pl.BlockSpec((1, B, tn), lambda l, j: (l, 0, j)),
        compiler_params=_cparams(("arbitrary", "arbitrary")),
        name="adaln",
    )(c, w_ada, b_ada.reshape(L, 1, N))


def _rope_kernel(pos_ref, invf_ref, cos_ref, sin_ref):
    ang = pos_ref[...].astype(F32) * invf_ref[...]
    lane = lax.broadcasted_iota(jnp.int32, ang.shape, 1)
    c = jnp.cos(ang)
    s = jnp.sin(ang)
    cos_ref[...] = jnp.where(lane < ROT_DIM, c, 1.0)
    sin_ref[...] = jnp.where(lane < ROT_DIM // 2, -s, jnp.where(lane < ROT_DIM, s, 0.0))


def rope_tables(positions, *, tm=1024):
    T = positions.size
    tm = min(tm, T)
    half = ROT_DIM // 2
    inv_freq = ROPE_THETA ** (-jnp.arange(0, ROT_DIM, 2, dtype=F32) / ROT_DIM)
    invf = jnp.zeros((1, LANES), F32).at[0, :half].set(inv_freq).at[0, half:ROT_DIM].set(inv_freq)
    return pl.pallas_call(
        _rope_kernel,
        out_shape=(jax.ShapeDtypeStruct((T, LANES), F32),) * 2,
        grid=(T // tm,),
        in_specs=[
            pl.BlockSpec((tm, 1), lambda i: (i, 0)),
            pl.BlockSpec((1, LANES), lambda i: (0, 0)),
        ],
        out_specs=(pl.BlockSpec((tm, LANES), lambda i: (i, 0)),) * 2,
        compiler_params=_cparams(("arbitrary",)),
        name="rope_tables",
    )(positions.reshape(T, 1), invf)


def _inproj_kernel(x_ref, g_ref, sh_ref, sc_ref, w_ref, wff_ref, cg_ref, cos_ref, sin_ref,
                   o_ref, vt_ref, ff_ref, h_scr, *, sec_v):
    j = pl.program_id(1)

    @pl.when(j == 0)
    def _():
        y = _rms(x_ref[...]) * g_ref[...]
        h = (y * (1.0 + sc_ref[0]) + sh_ref[0]).astype(BF16)
        h_scr[...] = h
        ff_ref[...] = jnp.dot(h, wff_ref[...], preferred_element_type=F32)

    W = ATTN_WIDTH
    n_heads = W // HEAD_DIM

    def dots():
        h = h_scr[...]
        return [jnp.dot(h, w_ref[:, s * W:(s + 1) * W], preferred_element_type=F32) for s in range(2)]

    def normed_heads(acc, s):
        for hd in range(n_heads):
            cols = slice(s * W + hd * HEAD_DIM, s * W + (hd + 1) * HEAD_DIM)
            yield cols, _rms(acc[:, hd * HEAD_DIM:(hd + 1) * HEAD_DIM]) * cg_ref[:, cols]

    @pl.when(j == 0)
    def _():
        for s, acc in enumerate(dots()):
            for cols, y in normed_heads(acc, s):
                o_ref[:, cols] = y.astype(o_ref.dtype)

    @pl.when(j == 1)
    def _():
        cos_t = cos_ref[...]
        sin_t = sin_ref[...]
        low = lax.broadcasted_iota(jnp.int32, cos_t.shape, 1) < ROT_DIM // 2
        for s, acc in enumerate(dots()):
            for cols, y in normed_heads(acc, s):
                rot = jnp.where(low, pltpu.roll(y, HEAD_DIM - ROT_DIM // 2, 1),
                                pltpu.roll(y, ROT_DIM // 2, 1))
                o_ref[:, cols] = (y * cos_t + rot * sin_t).astype(o_ref.dtype)

    @pl.when((j >= 2) & (j < sec_v))
    def _():
        for s, acc in enumerate(dots()):
            o_ref[:, s * W:(s + 1) * W] = _sigmoid(acc).astype(o_ref.dtype)

    @pl.when(j >= sec_v)
    def _():
        for s, acc in enumerate(dots()):
            vt_ref[s * W:(s + 1) * W, :] = acc.T.astype(vt_ref.dtype)


def inproj(x2, g, shift, scale, w_cat, w_ff, colgain, cos_t, sin_t, *, seq, tm=512):
    T, D = x2.shape
    NC = w_cat.shape[1]
    tn = 2 * ATTN_WIDTH
    tm = min(tm, seq)
    bpb = seq // tm
    sec_v = NC // tn - 1
    return pl.pallas_call(
        functools.partial(_inproj_kernel, sec_v=sec_v),
        out_shape=(jax.ShapeDtypeStruct((T, sec_v * tn), BF16),
                   jax.ShapeDtypeStruct((tn, T), BF16),
                   jax.ShapeDtypeStruct((T, LANES), F32)),
        grid=(T // tm, NC // tn),
        in_specs=[
            pl.BlockSpec((tm, D), lambda i, j: (i, 0)),
            pl.BlockSpec((1, D), lambda i, j: (0, 0)),
            pl.BlockSpec((1, 1, D), lambda i, j: (i // bpb, 0, 0)),
            pl.BlockSpec((1, 1, D), lambda i, j: (i // bpb, 0, 0)),
            pl.BlockSpec((D, tn), lambda i, j: (0, j)),
            pl.BlockSpec((D, LANES), lambda i, j: (0, 0)),
            pl.BlockSpec((1, tn), lambda i, j: (0, j)),
            pl.BlockSpec((tm, LANES), lambda i, j: (i, 0)),
            pl.BlockSpec((tm, LANES), lambda i, j: (i, 0)),
        ],
        out_specs=(
            pl.BlockSpec((tm, tn), lambda i, j: (i, jnp.minimum(j, sec_v - 1))),
            pl.BlockSpec((tn, tm), lambda i, j: (0, i)),
            pl.BlockSpec((tm, LANES), lambda i, j: (i, 0)),
        ),
        scratch_shapes=[pltpu.VMEM((tm, D), BF16)],
        compiler_params=_cparams(("arbitrary", "arbitrary")),
        name="inproj",
    )(x2, g, shift, scale, w_cat, w_ff, colgain, cos_t, sin_t)


def _split3(v):
    hi = v.astype(BF16)
    r = v - hi.astype(F32)
    mid = r.astype(BF16)
    lo = (r - mid.astype(F32)).astype(BF16)
    return hi, mid, lo


def _fgate_kernel(ff_ref, b_ref, k_ref, o_ref, carry_scr):
    @pl.when(pl.program_id(1) == 0)
    def _():
        carry_scr[...] = jnp.zeros_like(carry_scr)

    z = ff_ref[...] + b_ref[...]
    logf = jnp.minimum(z, 0.0) - jnp.log(1.0 + jnp.exp(-jnp.abs(z)))
    tm = z.shape[0]
    row = lax.broadcasted_iota(jnp.int32, (tm, tm), 0)
    col = lax.broadcasted_iota(jnp.int32, (tm, tm), 1)
    tri = jnp.where(row >= col, 1.0, 0.0).astype(BF16)
    cum = carry_scr[...]
    for part in _split3(logf):
        cum = cum + jnp.dot(tri, part, preferred_element_type=F32)
    carry_scr[...] = cum[tm - 1:tm, :]
    bias = cum * (-LOG2E)
    lane = lax.broadcasted_iota(jnp.int32, (tm, HEAD_DIM), 1)
    for hd in range(N_FOX_HEADS):
        hi, mid, lo = [p.astype(F32) for p in _split3(bias[:, hd:hd + 1])]
        ext = jnp.where(lane == 0, hi, jnp.where(lane == 1, mid, jnp.where(lane == 2, lo, 0.0)))
        o_ref[0, hd, :, :HEAD_DIM] = k_ref[:, hd * HEAD_DIM:(hd + 1) * HEAD_DIM]
        o_ref[0, hd, :, HEAD_DIM:] = ext.astype(o_ref.dtype)


def fgate_keys(ff, b_fgt, proj, *, batch, seq, tm=512):
    tm = min(tm, seq)
    b_pad = jnp.zeros((1, LANES), F32).at[0, :N_FOX_HEADS].set(b_fgt.astype(F32))
    nb = seq // tm
    return pl.pallas_call(
        _fgate_kernel,
        out_shape=jax.ShapeDtypeStruct((batch, N_FOX_HEADS, seq, 2 * HEAD_DIM), BF16),
        grid=(batch, nb),
        in_specs=[
            pl.BlockSpec((tm, LANES), lambda b, i: (b * nb + i, 0)),
            pl.BlockSpec((1, LANES), lambda b, i: (0, 0)),
            pl.BlockSpec((tm, ATTN_WIDTH), lambda b, i: (b * nb + i, SEC_FK)),
        ],
        out_specs=pl.BlockSpec((1, N_FOX_HEADS, tm, 2 * HEAD_DIM), lambda b, i: (b, 0, i, 0)),
        scratch_shapes=[pltpu.VMEM((1, LANES), F32)],
        compiler_params=_cparams(("arbitrary", "arbitrary")),
        name="fgate",
    )(ff, b_pad, proj)


_NT = (((1,), (1,)), ((), ()))
ONES_ROWS = 16


def _flash_step(sT, vT, m_ref, acc_ref):
    m_old = m_ref[...]
    m_new = jnp.maximum(m_old, jnp.max(sT, axis=0, keepdims=True))
    alpha = jnp.exp2(m_old - m_new)
    pT = jnp.exp2(sT - m_new).astype(vT.dtype)
    acc_ref[...] = alpha * acc_ref[...] + jnp.dot(vT, pT, preferred_element_type=F32)
    m_ref[...] = m_new


def _kv_schedule(i, qk, spv):
    @pl.when(i == 0)
    def _():
        qk(0, True, 0)
        spv(0, 0)

    @pl.when(i > 0)
    def _():
        qk(0, False, 0)

        def pair(jj, carry):
            qk(2 * jj + 1, False, 1)
            spv(0, 2 * jj)
            qk(2 * jj + 2, False, 0)
            spv(1, 2 * jj + 1)
            return carry

        lax.fori_loop(0, (i - 1) // 2, pair, 0)

        @pl.when(i % 2 == 1)
        def _():
            qk(i, True, 1)
            spv(0, i - 1)
            spv(1, i)

        @pl.when(i % 2 == 0)
        def _():
            qk(i - 1, False, 1)
            spv(0, i - 2)
            qk(i, True, 0)
            spv(1, i - 1)
            spv(0, i)


def _with_ones(vT):
    return jnp.concatenate([vT, jnp.ones((ONES_ROWS, vT.shape[1]), vT.dtype)], axis=0)


FOX_HEADS_PER_STEP = 2


def _fox_kernel(q_ref, k_ref, vt_ref, o_ref, *scratch, tq):
    i = pl.program_id(2)
    hps = FOX_HEADS_PER_STEP
    heads = [scratch[4 * hh:4 * hh + 4] for hh in range(hps)]
    lane = lax.broadcasted_iota(jnp.int32, (tq, HEAD_DIM), 1)
    ones_ext = jnp.where(lane < BIAS_LANES, 1.0, 0.0).astype(q_ref.dtype)
    qs = []
    for hh, (m_scr, acc_scr, _, _) in enumerate(heads):
        qs.append(jnp.concatenate([q_ref[0, :, hh * HEAD_DIM:(hh + 1) * HEAD_DIM], ones_ext], axis=1))
        m_scr[...] = jnp.full_like(m_scr, -jnp.inf)
        acc_scr[...] = jnp.zeros_like(acc_scr)

    def qk(j, masked, slot):
        ks = pl.multiple_of(j * tq, tq)
        for hh in range(hps):
            k = k_ref[0, hh, pl.ds(ks, tq), :]
            sT = lax.dot_general(k, qs[hh], _NT, preferred_element_type=F32)
            if masked:
                key = lax.broadcasted_iota(jnp.int32, sT.shape, 0)
                qry = lax.broadcasted_iota(jnp.int32, sT.shape, 1)
                sT = jnp.where(key <= qry, sT, -jnp.inf)
            heads[hh][2 + slot][...] = sT

    def spv(slot, j):
        ks = pl.multiple_of(j * tq, tq)
        for hh, (m_scr, acc_scr, _, _) in enumerate(heads):
            vT = _with_ones(vt_ref[hh * HEAD_DIM:(hh + 1) * HEAD_DIM, pl.ds(ks, tq)])
            _flash_step(heads[hh][2 + slot][...], vT, m_scr, acc_scr)

    _kv_schedule(i, qk, spv)
    for hh, (_, acc_scr, _, _) in enumerate(heads):
        o = acc_scr[:HEAD_DIM, :] / acc_scr[HEAD_DIM:HEAD_DIM + 1, :]
        o_ref[0, :, hh * HEAD_DIM:(hh + 1) * HEAD_DIM] = o.T.astype(o_ref.dtype)


def fox_attention(proj3, k_aug, vt, *, tq=512):
    B, S, _ = proj3.shape
    tq = min(tq, S)
    hps = FOX_HEADS_PER_STEP
    G = N_FOX_HEADS // hps
    per_head = [pltpu.VMEM((1, tq), F32), pltpu.VMEM((HEAD_DIM + ONES_ROWS, tq), F32),
                pltpu.VMEM((tq, tq), F32), pltpu.VMEM((tq, tq), F32)]
    return pl.pallas_call(
        functools.partial(_fox_kernel, tq=tq),
        out_shape=jax.ShapeDtypeStruct((B, S, ATTN_WIDTH), BF16),
        grid=(B, G, S // tq),
        in_specs=[
            pl.BlockSpec((1, tq, hps * HEAD_DIM), lambda b, g, i: (b, i, SEC_FQ * G + g)),
            pl.BlockSpec((1, hps, S, 2 * HEAD_DIM), lambda b, g, i: (b, g, 0, 0)),
            pl.BlockSpec((hps * HEAD_DIM, S), lambda b, g, i: (g, b)),
        ],
        out_specs=pl.BlockSpec((1, tq, hps * HEAD_DIM), lambda b, g, i: (b, i, g)),
        scratch_shapes=per_head * hps,
        compiler_params=_cparams(("arbitrary", "arbitrary", "arbitrary")),
        name="fox_attn",
    )(proj3, k_aug, vt)


DIFF_HEADS_PER_STEP = 2


def _diff_kernel(q_ref, k_ref, vt_ref, lq1_ref, lk1_ref, lq2_ref, lk2_ref, g_ref, o_ref,
                 *scratch, tq, lam_init):
    i = pl.program_id(2)
    dv = 2 * HEAD_DIM
    hps = DIFF_HEADS_PER_STEP
    chains = [[scratch[4 * (2 * hh + mp):4 * (2 * hh + mp) + 4] for mp in range(2)] for hh in range(hps)]
    for hh in range(hps):
        for m, a, _, _ in chains[hh]:
            m[...] = jnp.full_like(m, -jnp.inf)
            a[...] = jnp.zeros_like(a)

    def qk(j, masked, slot):
        ks = pl.multiple_of(j * tq, tq)
        for hh in range(hps):
            for mp in range(2):
                cols = slice((2 * hh + mp) * HEAD_DIM, (2 * hh + mp + 1) * HEAD_DIM)
                sT = lax.dot_general(k_ref[0, pl.ds(ks, tq), cols], q_ref[0, :, cols], _NT,
                                     preferred_element_type=F32)
                if masked:
                    key = lax.broadcasted_iota(jnp.int32, sT.shape, 0)
                    qry = lax.broadcasted_iota(jnp.int32, sT.shape, 1)
                    sT = jnp.where(key // CHUNK <= qry // CHUNK, sT, -jnp.inf)
                chains[hh][mp][2 + slot][...] = sT

    def spv(slot, j):
        ks = pl.multiple_of(j * tq, tq)
        for hh in range(hps):
            vT = _with_ones(vt_ref[hh * dv:(hh + 1) * dv, pl.ds(ks, tq)])
            for m, a, *bufs in chains[hh]:
                _flash_step(bufs[slot][...], vT, m, a)

    _kv_schedule(i, qk, spv)
    lam = (jnp.exp(jnp.sum(lq1_ref[...] * lk1_ref[...], axis=-1, keepdims=True))
           - jnp.exp(jnp.sum(lq2_ref[...] * lk2_ref[...], axis=-1, keepdims=True)) + lam_init)
    for hh in range(hps):
        (_, a1, _, _), (_, a2, _, _) = chains[hh]
        o = a1[:dv, :] / a1[dv:dv + 1, :] - lam * (a2[:dv, :] / a2[dv:dv + 1, :])
        o = _rms(o, axis=0) * g_ref[...] * (1.0 - lam_init)
        o_ref[0, :, hh * dv:(hh + 1) * dv] = o.T.astype(o_ref.dtype)


def diff_attention(proj3, vt, lam_vecs, out_gain, *, lam_init, tq=512):
    B, S, _ = proj3.shape
    tq = min(tq, S)
    hps = DIFF_HEADS_PER_STEP
    G = N_DIFF_HEADS // hps
    dv = 2 * HEAD_DIM
    vec = pl.BlockSpec((1, HEAD_DIM), lambda b, g, i: (0, 0))
    per_chain = [pltpu.VMEM((1, tq), F32), pltpu.VMEM((dv + ONES_ROWS, tq), F32),
                 pltpu.VMEM((tq, tq), F32), pltpu.VMEM((tq, tq), F32)]
    return pl.pallas_call(
        functools.partial(_diff_kernel, tq=tq, lam_init=lam_init),
        out_shape=jax.ShapeDtypeStruct((B, S, ATTN_WIDTH), BF16),
        grid=(B, G, S // tq),
        in_specs=[
            pl.BlockSpec((1, tq, hps * dv), lambda b, g, i: (b, i, SEC_DQ * G + g)),
            pl.BlockSpec((1, S, hps * dv), lambda b, g, i: (b, 0, SEC_DK * G + g)),
            pl.BlockSpec((hps * dv, S), lambda b, g, i: (G + g, b)),
            vec, vec, vec, vec,
            pl.BlockSpec((dv, 1), lambda b, g, i: (0, 0)),
        ],
        out_specs=pl.BlockSpec((1, tq, hps * dv), lambda b, g, i: (b, i, g)),
        scratch_shapes=per_chain * (2 * hps),
        compiler_params=_cparams(("arbitrary", "arbitrary", "arbitrary")),
        name="diff_attn",
    )(proj3, proj3, vt, *lam_vecs, out_gain)


def _post_kernel(of_ref, od_ref, ga_ref, gb_ref, wb0_ref, wb1_ref, wo_ref, x_ref, g1_ref, o_ref):
    y0 = jnp.dot(of_ref[...], wb0_ref[...], preferred_element_type=F32)
    y1 = jnp.dot(od_ref[...], wb1_ref[...], preferred_element_type=F32)
    merged = ga_ref[...].astype(F32) * y0 + gb_ref[...].astype(F32) * y1
    mix = jnp.dot(merged.astype(BF16), wo_ref[...], preferred_element_type=F32)
    o_ref[...] = x_ref[...] + g1_ref[0] * mix


def _resident(shape):
    nd = len(shape)
    return pl.BlockSpec(shape, lambda *_: (0,) * nd, pipeline_mode=pl.Buffered(1))


def post_attention(o_fox, o_diff, proj, wb0, wb1, wo, x2, g1, *, seq, tm=256):
    T, D = x2.shape
    tm = min(tm, seq)
    bpb = seq // tm
    gate_blk = SEC_GATE * ATTN_WIDTH // D
    return pl.pallas_call(
        _post_kernel,
        out_shape=jax.ShapeDtypeStruct((T, D), F32),
        grid=(T // tm,),
        in_specs=[
            pl.BlockSpec((tm, ATTN_WIDTH), lambda i: (i, 0)),
            pl.BlockSpec((tm, ATTN_WIDTH), lambda i: (i, 0)),
            pl.BlockSpec((tm, D), lambda i: (i, gate_blk)),
            pl.BlockSpec((tm, D), lambda i: (i, gate_blk + 1)),
            _resident(wb0.shape), _resident(wb1.shape), _resident(wo.shape),
            pl.BlockSpec((tm, D), lambda i: (i, 0)),
            pl.BlockSpec((1, 1, D), lambda i: (i // bpb, 0, 0)),
        ],
        out_specs=pl.BlockSpec((tm, D), lambda i: (i, 0)),
        compiler_params=_cparams(("arbitrary",)),
        name="post_attn",
    )(o_fox, o_diff, proj, proj, wb0, wb1, wo, x2, g1)


def _pack_halves(y):
    half = y.shape[1] // 2
    bits = lax.bitcast_convert_type(y.astype(BF16).astype(F32), jnp.uint32)
    return (bits[:, half:] & jnp.uint32(0xFFFF0000)) | (bits[:, :half] >> 16)


def _unpack_halves(w):
    lo = lax.bitcast_convert_type(w << 16, F32)
    hi = lax.bitcast_convert_type(w & jnp.uint32(0xFFFF0000), F32)
    return lo, hi


def _store_token_major(ref, words, s8):
    m = words.shape[0]
    for s in range(s8):
        ref[pl.ds(s, m, stride=s8), :] = words[:, s * LANES:(s + 1) * LANES]


def _load_token_major(ref, m, s8):
    parts = [_unpack_halves(ref[pl.ds(s, m, stride=s8), :]) for s in range(s8)]
    return jnp.concatenate([p[0].astype(BF16) for p in parts] + [p[1].astype(BF16) for p in parts],
                           axis=1)


def _router_kernel(x_ref, g_ref, sh_ref, sc_ref, wr_ref, br_ref, h_ref, e_ref, w_ref, *, s8):
    y = _rms(x_ref[...]) * g_ref[...]
    h = y * (1.0 + sc_ref[0]) + sh_ref[0]
    hb = h.astype(BF16)
    _store_token_major(h_ref, _pack_halves(h), s8)
    logits = lax.dot_general(wr_ref[...], hb, _NT, preferred_element_type=F32)
    scores = _sigmoid(logits)
    tm = scores.shape[1]
    sc3 = scores.reshape(N_GROUPS, GROUP_SIZE, tm)
    sel = sc3 + br_ref[...]
    neg = -jnp.inf
    member = lax.broadcasted_iota(jnp.int32, sel.shape, 1)
    group = lax.broadcasted_iota(jnp.int32, sel.shape, 0)

    m1 = jnp.max(sel, axis=1, keepdims=True)
    i1 = jnp.min(jnp.where(sel == m1, member, GROUP_SIZE), axis=1, keepdims=True)
    m2 = jnp.max(jnp.where(member == i1, neg, sel), axis=1, keepdims=True)
    gscore = m1 + m2

    gidx = lax.broadcasted_iota(jnp.int32, gscore.shape, 0)
    gkeep = jnp.zeros(gscore.shape, jnp.int32)
    for _ in range(TOPK_GROUPS):
        best = jnp.max(gscore, axis=0, keepdims=True)
        bi = jnp.min(jnp.where(gscore == best, gidx, N_GROUPS), axis=0, keepdims=True)
        hit = gidx == bi
        gkeep = jnp.where(hit, 1, gkeep)
        gscore = jnp.where(hit, neg, gscore)

    cand = jnp.where(gkeep > 0, sel, neg)
    eidx = group * GROUP_SIZE + member
    weights = []
    for k in range(TOP_K):
        best = jnp.max(jnp.max(cand, axis=1, keepdims=True), axis=0, keepdims=True)
        bi = jnp.min(jnp.min(jnp.where(cand == best, eidx, N_EXPERTS), axis=1, keepdims=True),
                     axis=0, keepdims=True)
        hit = eidx == bi
        e_ref[k:k + 1, :] = bi[0]
        weights.append(jnp.sum(jnp.sum(jnp.where(hit, sc3, 0.0), axis=1, keepdims=True),
                               axis=0, keepdims=True)[0])
        cand = jnp.where(hit, neg, cand)
    wsum = functools.reduce(lambda a, b: a + b, weights)
    row = lax.broadcasted_iota(jnp.int32, (LANES, tm), 0)
    wmat = jnp.zeros((LANES, tm), F32)
    for k in range(TOP_K):
        wmat = jnp.where(row == k, weights[k] / wsum * ROUTED_SCALE, wmat)
    w_ref[...] = wmat.T


def router(x2, g, shift, scale, wr_t, b_router, *, seq, tm=512):
    T, D = x2.shape
    tm = min(tm, seq)
    bpb = seq // tm
    s8 = D // (2 * LANES)
    return pl.pallas_call(
        functools.partial(_router_kernel, s8=s8),
        out_shape=(jax.ShapeDtypeStruct((T * s8, LANES), jnp.uint32),
                   jax.ShapeDtypeStruct((TOP_K, T), jnp.int32),
                   jax.ShapeDtypeStruct((T, LANES), F32)),
        grid=(T // tm,),
        in_specs=[
            pl.BlockSpec((tm, D), lambda i: (i, 0)),
            pl.BlockSpec((1, D), lambda i: (0, 0)),
            pl.BlockSpec((1, 1, D), lambda i: (i // bpb, 0, 0)),
            pl.BlockSpec((1, 1, D), lambda i: (i // bpb, 0, 0)),
            pl.BlockSpec((N_EXPERTS, D), lambda i: (0, 0)),
            pl.BlockSpec((N_GROUPS, GROUP_SIZE, 1), lambda i: (0, 0, 0)),
        ],
        out_specs=(pl.BlockSpec((tm * s8, LANES), lambda i: (i, 0)),
                   pl.BlockSpec((TOP_K, tm), lambda i: (0, i)),
                   pl.BlockSpec((tm, LANES), lambda i: (i, 0))),
        compiler_params=_cparams(("arbitrary",)),
        name="router",
    )(x2, g, shift, scale, wr_t, b_router.reshape(N_GROUPS, GROUP_SIZE, 1).astype(F32))


def _slot_rank_kernel(e_ref, rank_ref, cnt_ref, base_scr, tri_scr):
    tm = e_ref.shape[1]

    @pl.when(pl.program_id(0) == 0)
    def _():
        base_scr[...] = jnp.zeros_like(base_scr)
        r = lax.broadcasted_iota(jnp.int32, (tm, tm), 0)
        c = lax.broadcasted_iota(jnp.int32, (tm, tm), 1)
        tri_scr[...] = jnp.where(r <= c, 1.0, 0.0).astype(BF16)

    expert = lax.broadcasted_iota(jnp.int32, (N_EXPERTS, tm), 0)
    base = base_scr[...]
    for k in range(TOP_K):
        hot = expert == e_ref[k:k + 1, :]
        onehot = jnp.where(hot, 1.0, 0.0)
        incl = jnp.dot(onehot.astype(BF16), tri_scr[...], preferred_element_type=F32)
        rank = jnp.sum(jnp.where(hot, base + incl - 1.0, 0.0), axis=0, keepdims=True)
        rank_ref[k:k + 1, :] = rank.astype(jnp.int32)
        base = base + jnp.sum(onehot, axis=1, keepdims=True)
    base_scr[...] = base
    cnt_ref[...] = jnp.broadcast_to(base, cnt_ref.shape).astype(jnp.int32)


def slot_rank(top_e, *, tm=512):
    K, T = top_e.shape
    tm = min(tm, T)
    return pl.pallas_call(
        _slot_rank_kernel,
        out_shape=(jax.ShapeDtypeStruct((K, T), jnp.int32),
                   jax.ShapeDtypeStruct((N_EXPERTS, LANES), jnp.int32)),
        grid=(T // tm,),
        in_specs=[pl.BlockSpec((K, tm), lambda i: (0, i))],
        out_specs=(pl.BlockSpec((K, tm), lambda i: (0, i)),
                   pl.BlockSpec((N_EXPERTS, LANES), lambda i: (0, 0))),
        scratch_shapes=[pltpu.VMEM((N_EXPERTS, 1), F32), pltpu.VMEM((tm, tm), BF16)],
        compiler_params=_cparams(("arbitrary",)),
        name="slot_rank",
    )(top_e)


def _slot_dest_kernel(pstart_ref, e_ref, rank_ref, dest_ref):
    e = e_ref[...]
    dest = rank_ref[...]
    for x in range(N_EXPERTS):
        dest = dest + jnp.where(e == x, pstart_ref[x], 0)
    dest_ref[...] = dest


def slot_dest(pstarts, top_e, rank, *, tm=4096):
    K, T = top_e.shape
    tm = min(tm, T)
    blockspec = pl.BlockSpec((K, tm), lambda i, ps: (0, i))
    return pl.pallas_call(
        _slot_dest_kernel,
        out_shape=jax.ShapeDtypeStruct((K, T), jnp.int32),
        grid_spec=pltpu.PrefetchScalarGridSpec(
            num_scalar_prefetch=1, grid=(T // tm,),
            in_specs=[blockspec, blockspec], out_specs=blockspec),
        compiler_params=_cparams(("arbitrary",)),
        name="slot_dest",
    )(pstarts, top_e, rank)


def _dispatch_kernel(zrow_ref, zon_ref, dest_ref, h_ref, xs_ref, zero_scr, sem, zsem, *, tm, s8, blk,
                     n_rows):
    def zero_copy(x):
        row = pl.multiple_of(zrow_ref[x] * s8, 8)
        return pltpu.make_async_copy(zero_scr, xs_ref.at[pl.ds(row, blk * s8)], zsem)

    @pl.when(pl.program_id(0) == 0)
    def _():
        zero_scr[...] = jnp.zeros_like(zero_scr)

        def start(x, c):
            @pl.when(zon_ref[x] > 0)
            def _():
                zero_copy(x).start()
            return c

        def wait(x, c):
            @pl.when(zon_ref[x] > 0)
            def _():
                zero_copy(x).wait()
            return c

        lax.fori_loop(0, 2 * N_EXPERTS, start, 0)
        lax.fori_loop(0, 2 * N_EXPERTS, wait, 0)

    def issue(t, c):
        src = h_ref.at[pl.ds(pl.multiple_of(t * s8, s8), s8)]
        for k in range(TOP_K):
            row = pl.multiple_of(dest_ref[k, t] * s8, s8)
            pltpu.make_async_copy(src, xs_ref.at[pl.ds(row, s8)], sem).start(priority=k % 2)
        return c

    tail = pl.multiple_of((n_rows + pl.program_id(0) * tm) * s8, 8)
    shared_copy = pltpu.make_async_copy(h_ref, xs_ref.at[pl.ds(tail, tm * s8)], zsem)
    shared_copy.start()
    lax.fori_loop(0, tm, issue, 0)
    for _ in range(TOP_K):
        pltpu.make_async_copy(h_ref, xs_ref.at[pl.ds(0, tm * s8)], sem).wait()
    shared_copy.wait()


def dispatch(hp, dest, zrow, zon, *, n_rows, blk, tm=512):
    K, T = dest.shape
    s8 = hp.shape[0] // T
    tm = min(tm, T)
    return pl.pallas_call(
        functools.partial(_dispatch_kernel, tm=tm, s8=s8, blk=blk, n_rows=n_rows),
        out_shape=jax.ShapeDtypeStruct(((n_rows + T) * s8, LANES), jnp.uint32),
        grid_spec=pltpu.PrefetchScalarGridSpec(
            num_scalar_prefetch=2, grid=(T // tm,),
            in_specs=[
                pl.BlockSpec((K, tm), lambda i, zr, zc: (0, i), memory_space=pltpu.SMEM),
                pl.BlockSpec((tm * s8, LANES), lambda i, zr, zc: (i, 0)),
            ],
            out_specs=pl.BlockSpec(memory_space=pl.ANY),
            scratch_shapes=[pltpu.VMEM((blk * s8, LANES), jnp.uint32),
                            pltpu.SemaphoreType.DMA, pltpu.SemaphoreType.DMA],
        ),
        compiler_params=_cparams(("arbitrary",)),
        name="dispatch",
    )(zrow, zon, dest, hp)


def _expert_kernel(be_ref, nb_ref, slot_ref, nxt_ref, x_ref, wg_hbm, wu_hbm, wd_hbm,
                   sg_hbm, su_hbm, sd_hbm, o_ref,
                   fg, fu, fd, wg_s, wu_s, wd_s, sems, *, layer, blk, s8, n_routed):
    b = pl.program_id(0)
    e = be_ref[b]
    live = (b < nb_ref[0]) | (b >= n_routed)
    first = (b == 0) | (e != be_ref[jnp.maximum(b - 1, 0)])

    def fetch(expert, slot, op):
        stage = (fg, fu, fd)

        @pl.when(expert < N_EXPERTS)
        def _():
            for w, f in zip((wg_hbm, wu_hbm, wd_hbm), stage):
                op(pltpu.make_async_copy(w.at[layer, expert], f.at[slot], sems.at[slot]))

        @pl.when(expert == N_EXPERTS)
        def _():
            for w, f in zip((sg_hbm, su_hbm, sd_hbm), stage):
                op(pltpu.make_async_copy(w.at[layer], f.at[slot], sems.at[slot]))

    @pl.when(live & (b == 0))
    def _():
        fetch(e, 0, lambda c: c.start())

    @pl.when(live & first)
    def _():
        slot = slot_ref[b]
        nxt = nxt_ref[e]

        @pl.when(nxt >= 0)
        def _():
            fetch(nxt, 1 - slot, lambda c: c.start())

        fetch(e, slot, lambda c: c.wait())
        wg_s[...] = fg[slot].astype(BF16)
        wu_s[...] = fu[slot].astype(BF16)
        wd_s[...] = fd[slot].astype(BF16)

    @pl.when(live)
    def _():
        x = _load_token_major(x_ref, blk, s8)
        a = jnp.dot(x, wg_s[...], preferred_element_type=F32)
        u = jnp.dot(x, wu_s[...], preferred_element_type=F32)
        hmid = (a * _sigmoid(a) * u).astype(BF16)
        y = jnp.dot(hmid, wd_s[...], preferred_element_type=F32)
        _store_token_major(o_ref, _pack_halves(y), s8)

    @pl.when(jnp.logical_not(live))
    def _():
        o_ref[...] = jnp.zeros_like(o_ref)


def experts(xs, block_e, n_used, run_slot, next_expert, wg, wu, wd, wsg, wsu, wsd, *, layer, blk,
            n_routed):
    D, E = wg.shape[2], wg.shape[3]
    s8 = D // (2 * LANES)
    nblk = xs.shape[0] // (blk * s8)
    hbm = pl.BlockSpec(memory_space=pl.ANY)

    def x_block(b, be, nb, sl, nx):
        return (jnp.where(b < n_routed, jnp.minimum(b, nb[0] - 1), b), 0)

    grid_spec = pltpu.PrefetchScalarGridSpec(
        num_scalar_prefetch=4,
        grid=(nblk,),
        in_specs=[
            pl.BlockSpec((blk * s8, LANES), x_block),
            hbm, hbm, hbm, hbm, hbm, hbm,
        ],
        out_specs=pl.BlockSpec((blk * s8, LANES), lambda b, be, nb, sl, nx: (b, 0)),
        scratch_shapes=[pltpu.VMEM((2, D, E), F32), pltpu.VMEM((2, D, E), F32), pltpu.VMEM((2, E, D), F32),
                        pltpu.VMEM((D, E), BF16), pltpu.VMEM((D, E), BF16), pltpu.VMEM((E, D), BF16),
                        pltpu.SemaphoreType.DMA((2,))],
    )
    return pl.pallas_call(
        functools.partial(_expert_kernel, layer=layer, blk=blk, s8=s8, n_routed=n_routed),
        out_shape=jax.ShapeDtypeStruct(xs.shape, jnp.uint32),
        grid_spec=grid_spec,
        compiler_params=_cparams(("arbitrary",)),
        name="experts",
    )(block_e, n_used, run_slot, next_expert, xs, wg, wu, wd, wsg, wsu, wsd)


def _ffn_out_kernel(dest_ref, sh_ref, wt_ref, x_ref, g2_ref, ys_ref, o_ref, buf, sem, *, tm, s8):
    def issue(t, c):
        for k in range(TOP_K):
            row = pl.multiple_of(dest_ref[k, t] * s8, s8)
            pltpu.make_async_copy(ys_ref.at[pl.ds(row, s8)],
                                  buf.at[k, pl.ds(pl.multiple_of(t * s8, s8), s8)],
                                  sem).start(priority=k % 2)
        return c

    lax.fori_loop(0, tm, issue, 0)
    for k in range(TOP_K):
        pltpu.make_async_copy(ys_ref.at[pl.ds(0, tm * s8)], buf.at[k], sem).wait()

    half = s8 * LANES
    wts = [jnp.broadcast_to(wt_ref[:, k:k + 1], (tm, LANES)) for k in range(TOP_K)]
    g2 = g2_ref[0]
    for s in range(s8):
        acc_lo, acc_hi = _unpack_halves(sh_ref[pl.ds(s, tm, stride=s8), :])
        for k in range(TOP_K):
            lo, hi = _unpack_halves(buf.at[k][pl.ds(s, tm, stride=s8), :])
            acc_lo = acc_lo + wts[k] * lo
            acc_hi = acc_hi + wts[k] * hi
        for off, acc in ((s * LANES, acc_lo), (half + s * LANES, acc_hi)):
            cols = slice(off, off + LANES)
            o_ref[:, cols] = x_ref[:, cols] + g2[:, cols] * acc


def ffn_out(ys, dest, w_tok, x2, g2, *, seq, n_routed_rows, tm=256):
    T, D = x2.shape
    tm = min(tm, seq)
    bpb = seq // tm
    s8 = D // (2 * LANES)
    first_shared = n_routed_rows // tm
    return pl.pallas_call(
        functools.partial(_ffn_out_kernel, tm=tm, s8=s8),
        out_shape=jax.ShapeDtypeStruct((T, D), F32),
        grid=(T // tm,),
        in_specs=[
            pl.BlockSpec((TOP_K, tm), lambda i: (0, i), memory_space=pltpu.SMEM),
            pl.BlockSpec((tm * s8, LANES), lambda i: (first_shared + i, 0)),
            pl.BlockSpec((tm, LANES), lambda i: (i, 0)),
            pl.BlockSpec((tm, D), lambda i: (i, 0)),
            pl.BlockSpec((1, 1, D), lambda i: (i // bpb, 0, 0)),
            pl.BlockSpec(memory_space=pl.ANY),
        ],
        out_specs=pl.BlockSpec((tm, D), lambda i: (i, 0)),
        scratch_shapes=[pltpu.VMEM((TOP_K, tm * s8, LANES), jnp.uint32), pltpu.SemaphoreType.DMA],
        compiler_params=_cparams(("arbitrary",)),
        name="ffn_out",
    )(dest, ys, w_tok, x2, g2, ys)


def block_plan(counts, *, blk, n_slots, n_tokens):
    padded = ((counts + blk - 1) // blk) * blk
    pends = jnp.cumsum(padded)
    pstarts = (pends - padded).astype(jnp.int32)
    nblk = -(-n_slots // blk) + N_EXPERTS
    first_row = jnp.arange(nblk, dtype=pends.dtype) * blk
    block_e = jnp.sum(pends[None, :] <= first_row[:, None], axis=1)
    block_e = jnp.minimum(block_e, N_EXPERTS - 1).astype(jnp.int32)
    n_used = (pends[-1] // blk).astype(jnp.int32)
    tail = n_used + jnp.arange(N_EXPERTS, dtype=jnp.int32)
    zrow = jnp.concatenate([pends - blk, jnp.minimum(tail, nblk - 1) * blk]).astype(jnp.int32)
    zon = jnp.concatenate([counts > 0, tail < nblk]).astype(jnp.int32)
    block_e = jnp.concatenate([block_e, jnp.full((n_tokens // blk,), N_EXPERTS, jnp.int32)])
    pos = jnp.arange(block_e.shape[0], dtype=jnp.int32)
    live = (pos < n_used) | (pos >= nblk)
    new_run = jnp.concatenate([jnp.ones((1,), jnp.bool_), block_e[1:] != block_e[:-1]]) & live
    run_slot = ((jnp.cumsum(new_run.astype(jnp.int32)) - 1) % 2).astype(jnp.int32)
    ids = jnp.arange(N_EXPERTS, dtype=jnp.int32)
    later = (ids[None, :] > ids[:, None]) & (counts[None, :] > 0)
    next_expert = jnp.min(jnp.where(later, ids[None, :], N_EXPERTS), axis=1)
    next_expert = jnp.concatenate([next_expert, jnp.full((1,), -1)]).astype(jnp.int32)
    return pstarts, block_e, n_used.reshape(1), zrow, zon, run_slot, next_expert, nblk


def _layer_weights(l, D, norm1_g, norm2_g, w_in, b_fgt, fox_qn_g, fox_kn_g, diff_qn_g, diff_kn_g,
                   diff_out_g, w_branch, w_out, w_router, w_exp_gate, w_exp_up, w_exp_down,
                   w_sh_gate, w_sh_up, w_sh_down):
    W = ATTN_WIDTH
    fcol = 3 * W
    w = w_in[l]
    dcol = fcol + N_FOX_HEADS
    w_cat = jnp.concatenate([w[:, :2 * W], w[:, dcol:dcol + 2 * W], w[:, dcol + 3 * W:],
                             w[:, 2 * W:fcol], w[:, dcol + 2 * W:dcol + 3 * W]], axis=1).astype(BF16)
    w_ff = jnp.zeros((D, LANES), BF16).at[:, :N_FOX_HEADS].set(
        w[:, fcol:fcol + N_FOX_HEADS].astype(BF16))
    qscale = HEAD_DIM ** -0.5 * LOG2E
    colgain = jnp.concatenate([
        jnp.tile(fox_qn_g[l].astype(F32) * qscale, N_FOX_HEADS),
        jnp.tile(fox_kn_g[l].astype(F32), N_FOX_HEADS),
        jnp.tile(diff_qn_g[l].astype(F32) * qscale, 2 * N_DIFF_HEADS),
        jnp.tile(diff_kn_g[l].astype(F32), 2 * N_DIFF_HEADS),
        jnp.ones((2 * D + 2 * W,), F32),
    ]).reshape(1, -1)
    return dict(
        w_cat=w_cat, w_ff=w_ff, colgain=colgain,
        wb0=w_branch[l, 0].astype(BF16), wb1=w_branch[l, 1].astype(BF16), wo=w_out[l].astype(BF16),
        wr_t=w_router[l].T.astype(BF16),
    )


def kernel(x, c, positions, norm1_g, norm2_g, w_ada, b_ada, w_in, b_fgt, fox_qn_g, fox_kn_g,
           diff_qn_g, diff_kn_g, lam_q1, lam_k1, lam_q2, lam_k2, diff_out_g, w_branch, w_out,
           w_router, b_router, w_exp_gate, w_exp_up, w_exp_down, w_sh_gate, w_sh_up, w_sh_down):
    B, S, D = x.shape
    L = w_ada.shape[0]
    T = B * S
    blk = 512
    cos_t, sin_t = rope_tables(positions)
    mod = adaln(c, w_ada, b_ada)
    x2 = x.reshape(T, D)
    for l in range(L):
        lw = _layer_weights(l, D, norm1_g, norm2_g, w_in, b_fgt, fox_qn_g, fox_kn_g, diff_qn_g,
                            diff_kn_g, diff_out_g, w_branch, w_out, w_router, w_exp_gate,
                            w_exp_up, w_exp_down, w_sh_gate, w_sh_up, w_sh_down)
        sh1, sc1, g1, sh2, sc2, g2 = [mod[l, :, k * D:(k + 1) * D].reshape(B, 1, D) for k in range(6)]
        lam_init = 0.8 - 0.6 * math.exp(-0.3 * l)

        proj, vt, ff = inproj(x2, norm1_g[l].reshape(1, D), sh1, sc1, lw['w_cat'], lw['w_ff'],
                              lw['colgain'], cos_t, sin_t, seq=S)
        k_aug = fgate_keys(ff, b_fgt[l], proj, batch=B, seq=S)
        proj3 = proj.reshape(B, S, -1)
        o_fox = fox_attention(proj3, k_aug, vt).reshape(T, ATTN_WIDTH)
        lam_vecs = [v[l].reshape(1, HEAD_DIM).astype(F32) for v in (lam_q1, lam_k1, lam_q2, lam_k2)]
        o_diff = diff_attention(proj3, vt, lam_vecs, diff_out_g[l].reshape(-1, 1).astype(F32),
                                lam_init=lam_init).reshape(T, ATTN_WIDTH)
        x2 = post_attention(o_fox, o_diff, proj, lw['wb0'], lw['wb1'], lw['wo'], x2, g1, seq=S)

        hp, top_e, w_tok = router(x2, norm2_g[l].reshape(1, D), sh2, sc2, lw['wr_t'], b_router[l], seq=S)
        rank, counts = slot_rank(top_e)
        counts = counts[:, 0]
        pstarts, block_e, n_used, zrow, zon, run_slot, next_expert, nblk = block_plan(
            counts, blk=blk, n_slots=T * TOP_K, n_tokens=T)
        dest = slot_dest(pstarts, top_e, rank)
        xs = dispatch(hp, dest, zrow, zon, n_rows=nblk * blk, blk=blk)
        ys = experts(xs, block_e, n_used, run_slot, next_expert, w_exp_gate, w_exp_up, w_exp_down,
                     w_sh_gate, w_sh_up, w_sh_down, layer=l, blk=blk, n_routed=nblk)
        x2 = ffn_out(ys, dest, w_tok, x2, g2, seq=S, n_routed_rows=nblk * blk)
    return x2.reshape(B, S, D)
```

```python
import functools
import math

import jax
import jax.numpy as jnp
import numpy as np
from jax import lax
from jax.experimental import pallas as pl
from jax.experimental.pallas import tpu as pltpu

F32 = jnp.float32
BF16 = jnp.bfloat16

HEAD_DIM = 128
N_FOX_HEADS = 8
N_DIFF_HEADS = 4
ATTN_WIDTH = N_FOX_HEADS * HEAD_DIM
CHUNK = 64
ROT_DIM = HEAD_DIM // 4
ROPE_THETA = 500000.0
N_EXPERTS = 64
N_GROUPS = 8
GROUP_SIZE = N_EXPERTS // N_GROUPS
TOPK_GROUPS = 4
TOP_K = 8
ROUTED_SCALE = 2.5
RMS_EPS = 1e-6
LANES = 128
V7X_VMEM_BYTES = 64 * 1024 * 1024
VMEM_LIMIT = V7X_VMEM_BYTES - 8 * 1024 * 1024

SEC_FQ, SEC_FK, SEC_DQ, SEC_DK, SEC_GATE = range(5)
LOG2E = math.log2(math.e)
BIAS_LANES = 3


def _cparams(sem):
    return pltpu.CompilerParams(dimension_semantics=sem, vmem_limit_bytes=VMEM_LIMIT)


def _rms(y, axis=-1):
    return y * lax.rsqrt(jnp.mean(y * y, axis=axis, keepdims=True) + RMS_EPS)


def _sigmoid(z):
    return 1.0 / (1.0 + jnp.exp(-z))


def _adaln_kernel(c_ref, w_ref, b_ref, o_ref):
    c = c_ref[...]
    cond = (c * _sigmoid(c)).astype(BF16)
    o_ref[0] = jnp.dot(cond, w_ref[0].astype(BF16), preferred_element_type=F32) + b_ref[0]


def adaln(c, w_ada, b_ada, *, tn=1024):
    L, D, N = w_ada.shape
    B = c.shape[0]
    tn = min(tn, N)
    return pl.pallas_call(
        _adaln_kernel,
        out_shape=jax.ShapeDtypeStruct((L, B, N), F32),
        grid=(L, N // tn),
        in_specs=[
            pl.BlockSpec((B, D), lambda l, j: (0, 0)),
            pl.BlockSpec((1, D, tn), lambda l, j: (l, 0, j)),
            pl.BlockSpec((1, 1, tn), lambda l, j: (l, 0, j)),
        ],
        out_specs=pl.BlockSpec((1, B, tn), lambda l, j: (l, 0, j)),
        compiler_params=_cparams(("arbitrary", "arbitrary")),
        name="adaln",
    )(c, w_ada, b_ada.reshape(L, 1, N))


def _rope_kernel(pos_ref, invf_ref, cos_ref, sin_ref):
    ang = pos_ref[...].astype(F32) * invf_ref[...]
    lane = lax.broadcasted_iota(jnp.int32, ang.shape, 1)
    c = jnp.cos(ang)
    s = jnp.sin(ang)
    cos_ref[...] = jnp.where(lane < ROT_DIM, c, 1.0)
    sin_ref[...] = jnp.where(lane < ROT_DIM // 2, -s, jnp.where(lane < ROT_DIM, s, 0.0))


def rope_tables(positions, *, tm=1024):
    T = positions.size
    tm = min(tm, T)
    half = ROT_DIM // 2
    inv_freq = ROPE_THETA ** (-jnp.arange(0, ROT_DIM, 2, dtype=F32) / ROT_DIM)
    invf = jnp.zeros((1, LANES), F32).at[0, :half].set(inv_freq).at[0, half:ROT_DIM].set(inv_freq)
    return pl.pallas_call(
        _rope_kernel,
        out_shape=(jax.ShapeDtypeStruct((T, LANES), F32),) * 2,
        grid=(T // tm,),
        in_specs=[
            pl.BlockSpec((tm, 1), lambda i: (i, 0)),
            pl.BlockSpec((1, LANES), lambda i: (0, 0)),
        ],
        out_specs=(pl.BlockSpec((tm, LANES), lambda i: (i, 0)),) * 2,
        compiler_params=_cparams(("arbitrary",)),
        name="rope_tables",
    )(positions.reshape(T, 1), invf)


def _inproj_kernel(x_ref, g_ref, sh_ref, sc_ref, w_ref, wff_ref, cg_ref, cos_ref, sin_ref,
                   o_ref, vt_ref, ff_ref, h_scr, *, sec_v):
    j = pl.program_id(1)

    @pl.when(j == 0)
    def _():
        y = _rms(x_ref[...]) * g_ref[...]
        h = (y * (1.0 + sc_ref[0]) + sh_ref[0]).astype(BF16)
        h_scr[...] = h
        ff_ref[...] = jnp.dot(h, wff_ref[...], preferred_element_type=F32)

    W = ATTN_WIDTH
    n_heads = W // HEAD_DIM

    def dots():
        h = h_scr[...]
        return [jnp.dot(h, w_ref[:, s * W:(s + 1) * W], preferred_element_type=F32) for s in range(2)]

    def normed_heads(acc, s):
        for hd in range(n_heads):
            cols = slice(s * W + hd * HEAD_DIM, s * W + (hd + 1) * HEAD_DIM)
            yield cols, _rms(acc[:, hd * HEAD_DIM:(hd + 1) * HEAD_DIM]) * cg_ref[:, cols]

    @pl.when(j == 0)
    def _():
        for s, acc in enumerate(dots()):
            for cols, y in normed_heads(acc, s):
                o_ref[:, cols] = y.astype(o_ref.dtype)

    @pl.when(j == 1)
    def _():
        cos_t = cos_ref[...]
        sin_t = sin_ref[...]
        low = lax.broadcasted_iota(jnp.int32, cos_t.shape, 1) < ROT_DIM // 2
        for s, acc in enumerate(dots()):
            for cols, y in normed_heads(acc, s):
                rot = jnp.where(low, pltpu.roll(y, HEAD_DIM - ROT_DIM // 2, 1),
                                pltpu.roll(y, ROT_DIM // 2, 1))
                o_ref[:, cols] = (y * cos_t + rot * sin_t).astype(o_ref.dtype)

    @pl.when((j >= 2) & (j < sec_v))
    def _():
        for s, acc in enumerate(dots()):
            o_ref[:, s * W:(s + 1) * W] = _sigmoid(acc).astype(o_ref.dtype)

    @pl.when(j >= sec_v)
    def _():
        for s, acc in enumerate(dots()):
            vt_ref[s * W:(s + 1) * W, :] = acc.T.astype(vt_ref.dtype)


def inproj(x2, g, shift, scale, w_cat, w_ff, colgain, cos_t, sin_t, *, seq, tm=512):
    T, D = x2.shape
    NC = w_cat.shape[1]
    tn = 2 * ATTN_WIDTH
    tm = min(tm, seq)
    bpb = seq // tm
    sec_v = NC // tn - 1
    return pl.pallas_call(
        functools.partial(_inproj_kernel, sec_v=sec_v),
        out_shape=(jax.ShapeDtypeStruct((T, sec_v * tn), BF16),
                   jax.ShapeDtypeStruct((tn, T), BF16),
                   jax.ShapeDtypeStruct((T, LANES), F32)),
        grid=(T // tm, NC // tn),
        in_specs=[
            pl.BlockSpec((tm, D), lambda i, j: (i, 0)),
            pl.BlockSpec((1, D), lambda i, j: (0, 0)),
            pl.BlockSpec((1, 1, D), lambda i, j: (i // bpb, 0, 0)),
            pl.BlockSpec((1, 1, D), lambda i, j: (i // bpb, 0, 0)),
            pl.BlockSpec((D, tn), lambda i, j: (0, j)),
            pl.BlockSpec((D, LANES), lambda i, j: (0, 0)),
            pl.BlockSpec((1, tn), lambda i, j: (0, j)),
            pl.BlockSpec((tm, LANES), lambda i, j: (i, 0)),
            pl.BlockSpec((tm, LANES), lambda i, j: (i, 0)),
        ],
        out_specs=(
            pl.BlockSpec((tm, tn), lambda i, j: (i, jnp.minimum(j, sec_v - 1))),
            pl.BlockSpec((tn, tm), lambda i, j: (0, i)),
            pl.BlockSpec((tm, LANES), lambda i, j: (i, 0)),
        ),
        scratch_shapes=[pltpu.VMEM((tm, D), BF16)],
        compiler_params=_cparams(("arbitrary", "arbitrary")),
        name="inproj",
    )(x2, g, shift, scale, w_cat, w_ff, colgain, cos_t, sin_t)


def _split3(v):
    hi = v.astype(BF16)
    r = v - hi.astype(F32)
    mid = r.astype(BF16)
    lo = (r - mid.astype(F32)).astype(BF16)
    return hi, mid, lo


def _fgate_kernel(ff_ref, b_ref, k_ref, o_ref, carry_scr):
    @pl.when(pl.program_id(1) == 0)
    def _():
        carry_scr[...] = jnp.zeros_like(carry_scr)

    z = ff_ref[...] + b_ref[...]
    logf = jnp.minimum(z, 0.0) - jnp.log(1.0 + jnp.exp(-jnp.abs(z)))
    tm = z.shape[0]
    row = lax.broadcasted_iota(jnp.int32, (tm, tm), 0)
    col = lax.broadcasted_iota(jnp.int32, (tm, tm), 1)
    tri = jnp.where(row >= col, 1.0, 0.0).astype(BF16)
    cum = carry_scr[...]
    for part in _split3(logf):
        cum = cum + jnp.dot(tri, part, preferred_element_type=F32)
    carry_scr[...] = cum[tm - 1:tm, :]
    bias = cum * (-LOG2E)
    lane = lax.broadcasted_iota(jnp.int32, (tm, HEAD_DIM), 1)
    for hd in range(N_FOX_HEADS):
        hi, mid, lo = [p.astype(F32) for p in _split3(bias[:, hd:hd + 1])]
        ext = jnp.where(lane == 0, hi, jnp.where(lane == 1, mid, jnp.where(lane == 2, lo, 0.0)))
        o_ref[0, hd, :, :HEAD_DIM] = k_ref[:, hd * HEAD_DIM:(hd + 1) * HEAD_DIM]
        o_ref[0, hd, :, HEAD_DIM:] = ext.astype(o_ref.dtype)


def fgate_keys(ff, b_fgt, proj, *, batch, seq, tm=512):
    tm = min(tm, seq)
    b_pad = jnp.zeros((1, LANES), F32).at[0, :N_FOX_HEADS].set(b_fgt.astype(F32))
    nb = seq // tm
    return pl.pallas_call(
        _fgate_kernel,
        out_shape=jax.ShapeDtypeStruct((batch, N_FOX_HEADS, seq, 2 * HEAD_DIM), BF16),
        grid=(batch, nb),
        in_specs=[
            pl.BlockSpec((tm, LANES), lambda b, i: (b * nb + i, 0)),
            pl.BlockSpec((1, LANES), lambda b, i: (0, 0)),
            pl.BlockSpec((tm, ATTN_WIDTH), lambda b, i: (b * nb + i, SEC_FK)),
        ],
        out_specs=pl.BlockSpec((1, N_FOX_HEADS, tm, 2 * HEAD_DIM), lambda b, i: (b, 0, i, 0)),
        scratch_shapes=[pltpu.VMEM((1, LANES), F32)],
        compiler_params=_cparams(("arbitrary", "arbitrary")),
        name="fgate",
    )(ff, b_pad, proj)


_NT = (((1,), (1,)), ((), ()))
ONES_ROWS = 16


def _flash_step(sT, vT, m_ref, acc_ref):
    m_old = m_ref[...]
    m_new = jnp.maximum(m_old, jnp.max(sT, axis=0, keepdims=True))
    alpha = jnp.exp2(m_old - m_new)
    pT = jnp.exp2(sT - m_new).astype(vT.dtype)
    acc_ref[...] = alpha * acc_ref[...] + jnp.dot(vT, pT, preferred_element_type=F32)
    m_ref[...] = m_new


def _kv_schedule(i, qk, spv):
    @pl.when(i == 0)
    def _():
        qk(0, True, 0)
        spv(0, 0)

    @pl.when(i > 0)
    def _():
        qk(0, False, 0)

        def pair(jj, carry):
            qk(2 * jj + 1, False, 1)
            spv(0, 2 * jj)
            qk(2 * jj + 2, False, 0)
            spv(1, 2 * jj + 1)
            return carry

        lax.fori_loop(0, (i - 1) // 2, pair, 0)

        @pl.when(i % 2 == 1)
        def _():
            qk(i, True, 1)
            spv(0, i - 1)
            spv(1, i)

        @pl.when(i % 2 == 0)
        def _():
            qk(i - 1, False, 1)
            spv(0, i - 2)
            qk(i, True, 0)
            spv(1, i - 1)
            spv(0, i)


def _with_ones(vT):
    return jnp.concatenate([vT, jnp.ones((ONES_ROWS, vT.shape[1]), vT.dtype)], axis=0)


FOX_HEADS_PER_STEP = 2


def _fox_kernel(q_ref, k_ref, vt_ref, o_ref, *scratch, tq):
    i = pl.program_id(2)
    hps = FOX_HEADS_PER_STEP
    heads = [scratch[4 * hh:4 * hh + 4] for hh in range(hps)]
    lane = lax.broadcasted_iota(jnp.int32, (tq, HEAD_DIM), 1)
    ones_ext = jnp.where(lane < BIAS_LANES, 1.0, 0.0).astype(q_ref.dtype)
    qs = []
    for hh, (m_scr, acc_scr, _, _) in enumerate(heads):
        qs.append(jnp.concatenate([q_ref[0, :, hh * HEAD_DIM:(hh + 1) * HEAD_DIM], ones_ext], axis=1))
        m_scr[...] = jnp.full_like(m_scr, -jnp.inf)
        acc_scr[...] = jnp.zeros_like(acc_scr)

    def qk(j, masked, slot):
        ks = pl.multiple_of(j * tq, tq)
        for hh in range(hps):
            k = k_ref[0, hh, pl.ds(ks, tq), :]
            sT = lax.dot_general(k, qs[hh], _NT, preferred_element_type=F32)
            if masked:
                key = lax.broadcasted_iota(jnp.int32, sT.shape, 0)
                qry = lax.broadcasted_iota(jnp.int32, sT.shape, 1)
                sT = jnp.where(key <= qry, sT, -jnp.inf)
            heads[hh][2 + slot][...] = sT

    def spv(slot, j):
        ks = pl.multiple_of(j * tq, tq)
        for hh, (m_scr, acc_scr, _, _) in enumerate(heads):
            vT = _with_ones(vt_ref[hh * HEAD_DIM:(hh + 1) * HEAD_DIM, pl.ds(ks, tq)])
            _flash_step(heads[hh][2 + slot][...], vT, m_scr, acc_scr)

    _kv_schedule(i, qk, spv)
    for hh, (_, acc_scr, _, _) in enumerate(heads):
        o = acc_scr[:HEAD_DIM, :] / acc_scr[HEAD_DIM:HEAD_DIM + 1, :]
        o_ref[0, :, hh * HEAD_DIM:(hh + 1) * HEAD_DIM] = o.T.astype(o_ref.dtype)


def fox_attention(proj3, k_aug, vt, *, tq=512):
    B, S, _ = proj3.shape
    tq = min(tq, S)
    hps = FOX_HEADS_PER_STEP
    G = N_FOX_HEADS // hps
    per_head = [pltpu.VMEM((1, tq), F32), pltpu.VMEM((HEAD_DIM + ONES_ROWS, tq), F32),
                pltpu.VMEM((tq, tq), F32), pltpu.VMEM((tq, tq), F32)]
    return pl.pallas_call(
        functools.partial(_fox_kernel, tq=tq),
        out_shape=jax.ShapeDtypeStruct((B, S, ATTN_WIDTH), BF16),
        grid=(B, G, S // tq),
        in_specs=[
            pl.BlockSpec((1, tq, hps * HEAD_DIM), lambda b, g, i: (b, i, SEC_FQ * G + g)),
            pl.BlockSpec((1, hps, S, 2 * HEAD_DIM), lambda b, g, i: (b, g, 0, 0)),
            pl.BlockSpec((hps * HEAD_DIM, S), lambda b, g, i: (g, b)),
        ],
        out_specs=pl.BlockSpec((1, tq, hps * HEAD_DIM), lambda b, g, i: (b, i, g)),
        scratch_shapes=per_head * hps,
        compiler_params=_cparams(("arbitrary", "arbitrary", "arbitrary")),
        name="fox_attn",
    )(proj3, k_aug, vt)


DIFF_HEADS_PER_STEP = 2


def _diff_kernel(q_ref, k_ref, vt_ref, lq1_ref, lk1_ref, lq2_ref, lk2_ref, g_ref, o_ref,
                 *scratch, tq, lam_init):
    i = pl.program_id(2)
    dv = 2 * HEAD_DIM
    hps = DIFF_HEADS_PER_STEP
    chains = [[scratch[4 * (2 * hh + mp):4 * (2 * hh + mp) + 4] for mp in range(2)] for hh in range(hps)]
    for hh in range(hps):
        for m, a, _, _ in chains[hh]:
            m[...] = jnp.full_like(m, -jnp.inf)
            a[...] = jnp.zeros_like(a)

    def qk(j, masked, slot):
        ks = pl.multiple_of(j * tq, tq)
        for hh in range(hps):
            for mp in range(2):
                cols = slice((2 * hh + mp) * HEAD_DIM, (2 * hh + mp + 1) * HEAD_DIM)
                sT = lax.dot_general(k_ref[0, pl.ds(ks, tq), cols], q_ref[0, :, cols], _NT,
                                     preferred_element_type=F32)
                if masked:
                    key = lax.broadcasted_iota(jnp.int32, sT.shape, 0)
                    qry = lax.broadcasted_iota(jnp.int32, sT.shape, 1)
                    sT = jnp.where(key // CHUNK <= qry // CHUNK, sT, -jnp.inf)
                chains[hh][mp][2 + slot][...] = sT

    def spv(slot, j):
        ks = pl.multiple_of(j * tq, tq)
        for hh in range(hps):
            vT = _with_ones(vt_ref[hh * dv:(hh + 1) * dv, pl.ds(ks, tq)])
            for m, a, *bufs in chains[hh]:
                _flash_step(bufs[slot][...], vT, m, a)

    _kv_schedule(i, qk, spv)
    lam = (jnp.exp(jnp.sum(lq1_ref[...] * lk1_ref[...], axis=-1, keepdims=True))
           - jnp.exp(jnp.sum(lq2_ref[...] * lk2_ref[...], axis=-1, keepdims=True)) + lam_init)
    for hh in range(hps):
        (_, a1, _, _), (_, a2, _, _) = chains[hh]
        o = a1[:dv, :] / a1[dv:dv + 1, :] - lam * (a2[:dv, :] / a2[dv:dv + 1, :])
        o = _rms(o, axis=0) * g_ref[...] * (1.0 - lam_init)
        o_ref[0, :, hh * dv:(hh + 1) * dv] = o.T.astype(o_ref.dtype)


def diff_attention(proj3, vt, lam_vecs, out_gain, *, lam_init, tq=512):
    B, S, _ = proj3.shape
    tq = min(tq, S)
    hps = DIFF_HEADS_PER_STEP
    G = N_DIFF_HEADS // hps
    dv = 2 * HEAD_DIM
    vec = pl.BlockSpec((1, HEAD_DIM), lambda b, g, i: (0, 0))
    per_chain = [pltpu.VMEM((1, tq), F32), pltpu.VMEM((dv + ONES_ROWS, tq), F32),
                 pltpu.VMEM((tq, tq), F32), pltpu.VMEM((tq, tq), F32)]
    return pl.pallas_call(
        functools.partial(_diff_kernel, tq=tq, lam_init=lam_init),
        out_shape=jax.ShapeDtypeStruct((B, S, ATTN_WIDTH), BF16),
        grid=(B, G, S // tq),
        in_specs=[
            pl.BlockSpec((1, tq, hps * dv), lambda b, g, i: (b, i, SEC_DQ * G + g)),
            pl.BlockSpec((1, S, hps * dv), lambda b, g, i: (b, 0, SEC_DK * G + g)),
            pl.BlockSpec((hps * dv, S), lambda b, g, i: (G + g, b)),
            vec, vec, vec, vec,
            pl.BlockSpec((dv, 1), lambda b, g, i: (0, 0)),
        ],
        out_specs=pl.BlockSpec((1, tq, hps * dv), lambda b, g, i: (b, i, g)),
        scratch_shapes=per_chain * (2 * hps),
        compiler_params=_cparams(("arbitrary", "arbitrary", "arbitrary")),
        name="diff_attn",
    )(proj3, proj3, vt, *lam_vecs, out_gain)


def _post_kernel(of_ref, od_ref, ga_ref, gb_ref, wb0_ref, wb1_ref, wo_ref, x_ref, g1_ref, o_ref):
    y0 = jnp.dot(of_ref[...], wb0_ref[...], preferred_element_type=F32)
    y1 = jnp.dot(od_ref[...], wb1_ref[...], preferred_element_type=F32)
    merged = ga_ref[...].astype(F32) * y0 + gb_ref[...].astype(F32) * y1
    mix = jnp.dot(merged.astype(BF16), wo_ref[...], preferred_element_type=F32)
    o_ref[...] = x_ref[...] + g1_ref[0] * mix


def _resident(shape):
    nd = len(shape)
    return pl.BlockSpec(shape, lambda *_: (0,) * nd, pipeline_mode=pl.Buffered(1))


def post_attention(o_fox, o_diff, proj, wb0, wb1, wo, x2, g1, *, seq, tm=256):
    T, D = x2.shape
    tm = min(tm, seq)
    bpb = seq // tm
    gate_blk = SEC_GATE * ATTN_WIDTH // D
    return pl.pallas_call(
        _post_kernel,
        out_shape=jax.ShapeDtypeStruct((T, D), F32),
        grid=(T // tm,),
        in_specs=[
            pl.BlockSpec((tm, ATTN_WIDTH), lambda i: (i, 0)),
            pl.BlockSpec((tm, ATTN_WIDTH), lambda i: (i, 0)),
            pl.BlockSpec((tm, D), lambda i: (i, gate_blk)),
            pl.BlockSpec((tm, D), lambda i: (i, gate_blk + 1)),
            _resident(wb0.shape), _resident(wb1.shape), _resident(wo.shape),
            pl.BlockSpec((tm, D), lambda i: (i, 0)),
            pl.BlockSpec((1, 1, D), lambda i: (i // bpb, 0, 0)),
        ],
        out_specs=pl.BlockSpec((tm, D), lambda i: (i, 0)),
        compiler_params=_cparams(("arbitrary",)),
        name="post_attn",
    )(o_fox, o_diff, proj, proj, wb0, wb1, wo, x2, g1)


def _pack_halves(y):
    half = y.shape[1] // 2
    bits = lax.bitcast_convert_type(y.astype(BF16).astype(F32), jnp.uint32)
    return (bits[:, half:] & jnp.uint32(0xFFFF0000)) | (bits[:, :half] >> 16)


def _unpack_halves(w):
    lo = lax.bitcast_convert_type(w << 16, F32)
    hi = lax.bitcast_convert_type(w & jnp.uint32(0xFFFF0000), F32)
    return lo, hi


def _store_token_major(ref, words, s8):
    m = words.shape[0]
    for s in range(s8):
        ref[pl.ds(s, m, stride=s8), :] = words[:, s * LANES:(s + 1) * LANES]


def _load_token_major(ref, m, s8):
    parts = [_unpack_halves(ref[pl.ds(s, m, stride=s8), :]) for s in range(s8)]
    return jnp.concatenate([p[0].astype(BF16) for p in parts] + [p[1].astype(BF16) for p in parts],
                           axis=1)


def _router_kernel(x_ref, g_ref, sh_ref, sc_ref, wr_ref, br_ref, h_ref, e_ref, w_ref, *, s8):
    y = _rms(x_ref[...]) * g_ref[...]
    h = y * (1.0 + sc_ref[0]) + sh_ref[0]
    hb = h.astype(BF16)
    _store_token_major(h_ref, _pack_halves(h), s8)
    logits = lax.dot_general(wr_ref[...], hb, _NT, preferred_element_type=F32)
    scores = _sigmoid(logits)
    tm = scores.shape[1]
    sc3 = scores.reshape(N_GROUPS, GROUP_SIZE, tm)
    sel = sc3 + br_ref[...]
    neg = -jnp.inf
    member = lax.broadcasted_iota(jnp.int32, sel.shape, 1)
    group = lax.broadcasted_iota(jnp.int32, sel.shape, 0)

    m1 = jnp.max(sel, axis=1, keepdims=True)
    i1 = jnp.min(jnp.where(sel == m1, member, GROUP_SIZE), axis=1, keepdims=True)
    m2 = jnp.max(jnp.where(member == i1, neg, sel), axis=1, keepdims=True)
    gscore = m1 + m2

    gidx = lax.broadcasted_iota(jnp.int32, gscore.shape, 0)
    gkeep = jnp.zeros(gscore.shape, jnp.int32)
    for _ in range(TOPK_GROUPS):
        best = jnp.max(gscore, axis=0, keepdims=True)
        bi = jnp.min(jnp.where(gscore == best, gidx, N_GROUPS), axis=0, keepdims=True)
        hit = gidx == bi
        gkeep = jnp.where(hit, 1, gkeep)
        gscore = jnp.where(hit, neg, gscore)

    cand = jnp.where(gkeep > 0, sel, neg)
    eidx = group * GROUP_SIZE + member
    weights = []
    for k in range(TOP_K):
        best = jnp.max(jnp.max(cand, axis=1, keepdims=True), axis=0, keepdims=True)
        bi = jnp.min(jnp.min(jnp.where(cand == best, eidx, N_EXPERTS), axis=1, keepdims=True),
                     axis=0, keepdims=True)
        hit = eidx == bi
        e_ref[k:k + 1, :] = bi[0]
        weights.append(jnp.sum(jnp.sum(jnp.where(hit, sc3, 0.0), axis=1, keepdims=True),
                               axis=0, keepdims=True)[0])
        cand = jnp.where(hit, neg, cand)
    wsum = functools.reduce(lambda a, b: a + b, weights)
    row = lax.broadcasted_iota(jnp.int32, (LANES, tm), 0)
    wmat = jnp.zeros((LANES, tm), F32)
    for k in range(TOP_K):
        wmat = jnp.where(row == k, weights[k] / wsum * ROUTED_SCALE, wmat)
    w_ref[...] = wmat.T


def router(x2, g, shift, scale, wr_t, b_router, *, seq, tm=512):
    T, D = x2.shape
    tm = min(tm, seq)
    bpb = seq // tm
    s8 = D // (2 * LANES)
    return pl.pallas_call(
        functools.partial(_router_kernel, s8=s8),
        out_shape=(jax.ShapeDtypeStruct((T * s8, LANES), jnp.uint32),
                   jax.ShapeDtypeStruct((TOP_K, T), jnp.int32),
                   jax.ShapeDtypeStruct((T, LANES), F32)),
        grid=(T // tm,),
        in_specs=[
            pl.BlockSpec((tm, D), lambda i: (i, 0)),
            pl.BlockSpec((1, D), lambda i: (0, 0)),
            pl.BlockSpec((1, 1, D), lambda i: (i // bpb, 0, 0)),
            pl.BlockSpec((1, 1, D), lambda i: (i // bpb, 0, 0)),
            pl.BlockSpec((N_EXPERTS, D), lambda i: (0, 0)),
            pl.BlockSpec((N_GROUPS, GROUP_SIZE, 1), lambda i: (0, 0, 0)),
        ],
        out_specs=(pl.BlockSpec((tm * s8, LANES), lambda i: (i, 0)),
                   pl.BlockSpec((TOP_K, tm), lambda i: (0, i)),
                   pl.BlockSpec((tm, LANES), lambda i: (i, 0))),
        compiler_params=_cparams(("arbitrary",)),
        name="router",
    )(x2, g, shift, scale, wr_t, b_router.reshape(N_GROUPS, GROUP_SIZE, 1).astype(F32))


def _slot_rank_kernel(e_ref, rank_ref, cnt_ref, base_scr, tri_scr):
    tm = e_ref.shape[1]

    @pl.when(pl.program_id(0) == 0)
    def _():
        base_scr[...] = jnp.zeros_like(base_scr)
        r = lax.broadcasted_iota(jnp.int32, (tm, tm), 0)
        c = lax.broadcasted_iota(jnp.int32, (tm, tm), 1)
        tri_scr[...] = jnp.where(r <= c, 1.0, 0.0).astype(BF16)

    expert = lax.broadcasted_iota(jnp.int32, (N_EXPERTS, tm), 0)
    base = base_scr[...]
    for k in range(TOP_K):
        hot = expert == e_ref[k:k + 1, :]
        onehot = jnp.where(hot, 1.0, 0.0)
        incl = jnp.dot(onehot.astype(BF16), tri_scr[...], preferred_element_type=F32)
        rank = jnp.sum(jnp.where(hot, base + incl - 1.0, 0.0), axis=0, keepdims=True)
        rank_ref[k:k + 1, :] = rank.astype(jnp.int32)
        base = base + jnp.sum(onehot, axis=1, keepdims=True)
    base_scr[...] = base
    cnt_ref[...] = jnp.broadcast_to(base, cnt_ref.shape).astype(jnp.int32)


def slot_rank(top_e, *, tm=512):
    K, T = top_e.shape
    tm = min(tm, T)
    return pl.pallas_call(
        _slot_rank_kernel,
        out_shape=(jax.ShapeDtypeStruct((K, T), jnp.int32),
                   jax.ShapeDtypeStruct((N_EXPERTS, LANES), jnp.int32)),
        grid=(T // tm,),
        in_specs=[pl.BlockSpec((K, tm), lambda i: (0, i))],
        out_specs=(pl.BlockSpec((K, tm), lambda i: (0, i)),
                   pl.BlockSpec((N_EXPERTS, LANES), lambda i: (0, 0))),
        scratch_shapes=[pltpu.VMEM((N_EXPERTS, 1), F32), pltpu.VMEM((tm, tm), BF16)],
        compiler_params=_cparams(("arbitrary",)),
        name="slot_rank",
    )(top_e)


def _slot_dest_kernel(pstart_ref, e_ref, rank_ref, dest_ref):
    e = e_ref[...]
    dest = rank_ref[...]
    for x in range(N_EXPERTS):
        dest = dest + jnp.where(e == x, pstart_ref[x], 0)
    dest_ref[...] = dest


def slot_dest(pstarts, top_e, rank, *, tm=4096):
    K, T = top_e.shape
    tm = min(tm, T)
    blockspec = pl.BlockSpec((K, tm), lambda i, ps: (0, i))
    return pl.pallas_call(
        _slot_dest_kernel,
        out_shape=jax.ShapeDtypeStruct((K, T), jnp.int32),
        grid_spec=pltpu.PrefetchScalarGridSpec(
            num_scalar_prefetch=1, grid=(T // tm,),
            in_specs=[blockspec, blockspec], out_specs=blockspec),
        compiler_params=_cparams(("arbitrary",)),
        name="slot_dest",
    )(pstarts, top_e, rank)


def _dispatch_kernel(zrow_ref, zon_ref, dest_ref, h_ref, xs_ref, zero_scr, sem, zsem, *, tm, s8, blk):
    def zero_copy(x):
        row = pl.multiple_of(zrow_ref[x] * s8, 8)
        return pltpu.make_async_copy(zero_scr, xs_ref.at[pl.ds(row, blk * s8)], zsem)

    @pl.when(pl.program_id(0) == 0)
    def _():
        zero_scr[...] = jnp.zeros_like(zero_scr)

        def start(x, c):
            @pl.when(zon_ref[x] > 0)
            def _():
                zero_copy(x).start()
            return c

        def wait(x, c):
            @pl.when(zon_ref[x] > 0)
            def _():
                zero_copy(x).wait()
            return c

        lax.fori_loop(0, 2 * N_EXPERTS, start, 0)
        lax.fori_loop(0, 2 * N_EXPERTS, wait, 0)

    for t in range(tm):
        src = h_ref.at[pl.ds(t * s8, s8)]
        for k in range(TOP_K):
            row = pl.multiple_of(dest_ref[k, t] * s8, s8)
            pltpu.make_async_copy(src, xs_ref.at[pl.ds(row, s8)], sem).start(priority=k % 2)
    for _ in range(TOP_K):
        pltpu.make_async_copy(h_ref, xs_ref.at[pl.ds(0, tm * s8)], sem).wait()


def dispatch(hp, dest, zrow, zon, *, n_rows, blk, tm=256):
    K, T = dest.shape
    s8 = hp.shape[0] // T
    tm = min(tm, T)
    return pl.pallas_call(
        functools.partial(_dispatch_kernel, tm=tm, s8=s8, blk=blk),
        out_shape=jax.ShapeDtypeStruct((n_rows * s8, LANES), jnp.uint32),
        grid_spec=pltpu.PrefetchScalarGridSpec(
            num_scalar_prefetch=2, grid=(T // tm,),
            in_specs=[
                pl.BlockSpec((K, tm), lambda i, zr, zc: (0, i), memory_space=pltpu.SMEM),
                pl.BlockSpec((tm * s8, LANES), lambda i, zr, zc: (i, 0)),
            ],
            out_specs=pl.BlockSpec(memory_space=pl.ANY),
            scratch_shapes=[pltpu.VMEM((blk * s8, LANES), jnp.uint32),
                            pltpu.SemaphoreType.DMA, pltpu.SemaphoreType.DMA],
        ),
        compiler_params=_cparams(("arbitrary",)),
        name="dispatch",
    )(zrow, zon, dest, hp)


def _expert_kernel(be_ref, nb_ref, slot_ref, nxt_ref, x_ref, wg_hbm, wu_hbm, wd_hbm, o_ref,
                   fg, fu, fd, wg_s, wu_s, wd_s, sems, *, layer, blk, s8):
    b = pl.program_id(0)
    e = be_ref[b]
    live = b < nb_ref[0]
    first = (b == 0) | (e != be_ref[jnp.maximum(b - 1, 0)])

    def fetch(expert, slot):
        return [pltpu.make_async_copy(w.at[layer, expert], f.at[slot], sems.at[slot])
                for w, f in ((wg_hbm, fg), (wu_hbm, fu), (wd_hbm, fd))]

    @pl.when(live & (b == 0))
    def _():
        for c in fetch(e, 0):
            c.start()

    @pl.when(live & first)
    def _():
        slot = slot_ref[b]
        nxt = nxt_ref[e]

        @pl.when(nxt >= 0)
        def _():
            for c in fetch(nxt, 1 - slot):
                c.start()

        for c in fetch(e, slot):
            c.wait()
        wg_s[...] = fg[slot].astype(BF16)
        wu_s[...] = fu[slot].astype(BF16)
        wd_s[...] = fd[slot].astype(BF16)

    @pl.when(live)
    def _():
        x = _load_token_major(x_ref, blk, s8)
        a = jnp.dot(x, wg_s[...], preferred_element_type=F32)
        u = jnp.dot(x, wu_s[...], preferred_element_type=F32)
        hmid = (a * _sigmoid(a) * u).astype(BF16)
        y = jnp.dot(hmid, wd_s[...], preferred_element_type=F32)
        _store_token_major(o_ref, _pack_halves(y), s8)

    @pl.when(b >= nb_ref[0])
    def _():
        o_ref[...] = jnp.zeros_like(o_ref)


def experts(xs, block_e, n_used, run_slot, next_expert, wg, wu, wd, *, layer, blk):
    D, E = wg.shape[2], wg.shape[3]
    s8 = D // (2 * LANES)
    nblk = xs.shape[0] // (blk * s8)
    hbm = pl.BlockSpec(memory_space=pl.ANY)
    grid_spec = pltpu.PrefetchScalarGridSpec(
        num_scalar_prefetch=4,
        grid=(nblk,),
        in_specs=[
            pl.BlockSpec((blk * s8, LANES), lambda b, be, nb, sl, nx: (jnp.minimum(b, nb[0] - 1), 0)),
            hbm, hbm, hbm,
        ],
        out_specs=pl.BlockSpec((blk * s8, LANES), lambda b, be, nb, sl, nx: (b, 0)),
        scratch_shapes=[pltpu.VMEM((2, D, E), F32), pltpu.VMEM((2, D, E), F32), pltpu.VMEM((2, E, D), F32),
                        pltpu.VMEM((D, E), BF16), pltpu.VMEM((D, E), BF16), pltpu.VMEM((E, D), BF16),
                        pltpu.SemaphoreType.DMA((2,))],
    )
    return pl.pallas_call(
        functools.partial(_expert_kernel, layer=layer, blk=blk, s8=s8),
        out_shape=jax.ShapeDtypeStruct(xs.shape, jnp.uint32),
        grid_spec=grid_spec,
        compiler_params=_cparams(("arbitrary",)),
        name="experts",
    )(block_e, n_used, run_slot, next_expert, xs, wg, wu, wd)


def _ffn_out_kernel(dest_ref, h_ref, wt_ref, wg_ref, wu_ref, wd_ref, x_ref, g2_ref, ys_ref, o_ref,
                    buf, sem, *, tm, s8):
    for t in range(tm):
        for k in range(TOP_K):
            row = pl.multiple_of(dest_ref[k, t] * s8, s8)
            pltpu.make_async_copy(ys_ref.at[pl.ds(row, s8)], buf.at[k, pl.ds(t * s8, s8)],
                                  sem).start(priority=k % 2)

    h = _load_token_major(h_ref, tm, s8)
    a = jnp.dot(h, wg_ref[...], preferred_element_type=F32)
    u = jnp.dot(h, wu_ref[...], preferred_element_type=F32)
    hmid = (a * _sigmoid(a) * u).astype(BF16)
    shared = jnp.dot(hmid, wd_ref[...], preferred_element_type=F32)

    for k in range(TOP_K):
        pltpu.make_async_copy(ys_ref.at[pl.ds(0, tm * s8)], buf.at[k], sem).wait()

    half = s8 * LANES
    wts = [jnp.broadcast_to(wt_ref[:, k:k + 1], (tm, LANES)) for k in range(TOP_K)]
    g2 = g2_ref[0]
    for s in range(s8):
        acc_lo = jnp.zeros((tm, LANES), F32)
        acc_hi = jnp.zeros((tm, LANES), F32)
        for k in range(TOP_K):
            lo, hi = _unpack_halves(buf.at[k][pl.ds(s, tm, stride=s8), :])
            acc_lo = acc_lo + wts[k] * lo
            acc_hi = acc_hi + wts[k] * hi
        for off, acc in ((s * LANES, acc_lo), (half + s * LANES, acc_hi)):
            cols = slice(off, off + LANES)
            o_ref[:, cols] = x_ref[:, cols] + g2[:, cols] * (shared[:, cols] + acc)


def ffn_out(hp, ys, dest, w_tok, wsg, wsu, wsd, x2, g2, *, seq, tm=256):
    T, D = x2.shape
    tm = min(tm, seq)
    bpb = seq // tm
    s8 = D // (2 * LANES)
    return pl.pallas_call(
        functools.partial(_ffn_out_kernel, tm=tm, s8=s8),
        out_shape=jax.ShapeDtypeStruct((T, D), F32),
        grid=(T // tm,),
        in_specs=[
            pl.BlockSpec((TOP_K, tm), lambda i: (0, i), memory_space=pltpu.SMEM),
            pl.BlockSpec((tm * s8, LANES), lambda i: (i, 0)),
            pl.BlockSpec((tm, LANES), lambda i: (i, 0)),
            _resident(wsg.shape), _resident(wsu.shape), _resident(wsd.shape),
            pl.BlockSpec((tm, D), lambda i: (i, 0)),
            pl.BlockSpec((1, 1, D), lambda i: (i // bpb, 0, 0)),
            pl.BlockSpec(memory_space=pl.ANY),
        ],
        out_specs=pl.BlockSpec((tm, D), lambda i: (i, 0)),
        scratch_shapes=[pltpu.VMEM((TOP_K, tm * s8, LANES), jnp.uint32), pltpu.SemaphoreType.DMA],
        compiler_params=_cparams(("arbitrary",)),
        name="ffn_out",
    )(dest, hp, w_tok, wsg, wsu, wsd, x2, g2, ys)


def block_plan(counts, *, blk, n_slots):
    padded = ((counts + blk - 1) // blk) * blk
    pends = jnp.cumsum(padded)
    pstarts = (pends - padded).astype(jnp.int32)
    nblk = -(-n_slots // blk) + N_EXPERTS
    first_row = jnp.arange(nblk, dtype=pends.dtype) * blk
    block_e = jnp.sum(pends[None, :] <= first_row[:, None], axis=1)
    block_e = jnp.minimum(block_e, N_EXPERTS - 1).astype(jnp.int32)
    n_used = (pends[-1] // blk).astype(jnp.int32)
    tail = n_used + jnp.arange(N_EXPERTS, dtype=jnp.int32)
    zrow = jnp.concatenate([pends - blk, jnp.minimum(tail, nblk - 1) * blk]).astype(jnp.int32)
    zon = jnp.concatenate([counts > 0, tail < nblk]).astype(jnp.int32)
    new_run = jnp.concatenate([jnp.ones((1,), jnp.int32),
                               (block_e[1:] != block_e[:-1]).astype(jnp.int32)])
    run_slot = ((jnp.cumsum(new_run) - 1) % 2).astype(jnp.int32)
    ids = jnp.arange(N_EXPERTS, dtype=jnp.int32)
    later = (ids[None, :] > ids[:, None]) & (counts[None, :] > 0)
    next_expert = jnp.min(jnp.where(later, ids[None, :], N_EXPERTS), axis=1)
    next_expert = jnp.where(next_expert < N_EXPERTS, next_expert, -1).astype(jnp.int32)
    return pstarts, block_e, n_used.reshape(1), zrow, zon, run_slot, next_expert, nblk


def _layer_weights(l, D, norm1_g, norm2_g, w_in, b_fgt, fox_qn_g, fox_kn_g, diff_qn_g, diff_kn_g,
                   diff_out_g, w_branch, w_out, w_router, w_exp_gate, w_exp_up, w_exp_down,
                   w_sh_gate, w_sh_up, w_sh_down):
    W = ATTN_WIDTH
    fcol = 3 * W
    w = w_in[l]
    dcol = fcol + N_FOX_HEADS
    w_cat = jnp.concatenate([w[:, :2 * W], w[:, dcol:dcol + 2 * W], w[:, dcol + 3 * W:],
                             w[:, 2 * W:fcol], w[:, dcol + 2 * W:dcol + 3 * W]], axis=1).astype(BF16)
    w_ff = jnp.zeros((D, LANES), BF16).at[:, :N_FOX_HEADS].set(
        w[:, fcol:fcol + N_FOX_HEADS].astype(BF16))
    qscale = HEAD_DIM ** -0.5 * LOG2E
    colgain = jnp.concatenate([
        jnp.tile(fox_qn_g[l].astype(F32) * qscale, N_FOX_HEADS),
        jnp.tile(fox_kn_g[l].astype(F32), N_FOX_HEADS),
        jnp.tile(diff_qn_g[l].astype(F32) * qscale, 2 * N_DIFF_HEADS),
        jnp.tile(diff_kn_g[l].astype(F32), 2 * N_DIFF_HEADS),
        jnp.ones((2 * D + 2 * W,), F32),
    ]).reshape(1, -1)
    return dict(
        w_cat=w_cat, w_ff=w_ff, colgain=colgain,
        wb0=w_branch[l, 0].astype(BF16), wb1=w_branch[l, 1].astype(BF16), wo=w_out[l].astype(BF16),
        wr_t=w_router[l].T.astype(BF16),
        wsg=w_sh_gate[l].astype(BF16), wsu=w_sh_up[l].astype(BF16), wsd=w_sh_down[l].astype(BF16),
    )


def kernel(x, c, positions, norm1_g, norm2_g, w_ada, b_ada, w_in, b_fgt, fox_qn_g, fox_kn_g,
           diff_qn_g, diff_kn_g, lam_q1, lam_k1, lam_q2, lam_k2, diff_out_g, w_branch, w_out,
           w_router, b_router, w_exp_gate, w_exp_up, w_exp_down, w_sh_gate, w_sh_up, w_sh_down):
    B, S, D = x.shape
    L = w_ada.shape[0]
    T = B * S
    blk = 512
    cos_t, sin_t = rope_tables(positions)
    mod = adaln(c, w_ada, b_ada)
    x2 = x.reshape(T, D)
    for l in range(L):
        lw = _layer_weights(l, D, norm1_g, norm2_g, w_in, b_fgt, fox_qn_g, fox_kn_g, diff_qn_g,
                            diff_kn_g, diff_out_g, w_branch, w_out, w_router, w_exp_gate,
                            w_exp_up, w_exp_down, w_sh_gate, w_sh_up, w_sh_down)
        sh1, sc1, g1, sh2, sc2, g2 = [mod[l, :, k * D:(k + 1) * D].reshape(B, 1, D) for k in range(6)]
        lam_init = 0.8 - 0.6 * math.exp(-0.3 * l)

        proj, vt, ff = inproj(x2, norm1_g[l].reshape(1, D), sh1, sc1, lw['w_cat'], lw['w_ff'],
                              lw['colgain'], cos_t, sin_t, seq=S)
        k_aug = fgate_keys(ff, b_fgt[l], proj, batch=B, seq=S)
        proj3 = proj.reshape(B, S, -1)
        o_fox = fox_attention(proj3, k_aug, vt).reshape(T, ATTN_WIDTH)
        lam_vecs = [v[l].reshape(1, HEAD_DIM).astype(F32) for v in (lam_q1, lam_k1, lam_q2, lam_k2)]
        o_diff = diff_attention(proj3, vt, lam_vecs, diff_out_g[l].reshape(-1, 1).astype(F32),
                                lam_init=lam_init).reshape(T, ATTN_WIDTH)
        x2 = post_attention(o_fox, o_diff, proj, lw['wb0'], lw['wb1'], lw['wo'], x2, g1, seq=S)

        hp, top_e, w_tok = router(x2, norm2_g[l].reshape(1, D), sh2, sc2, lw['wr_t'], b_router[l], seq=S)
        rank, counts = slot_rank(top_e)
        counts = counts[:, 0]
        pstarts, block_e, n_used, zrow, zon, run_slot, next_expert, nblk = block_plan(
            counts, blk=blk, n_slots=T * TOP_K)
        dest = slot_dest(pstarts, top_e, rank)
        xs = dispatch(hp, dest, zrow, zon, n_rows=nblk * blk, blk=blk)
        ys = experts(xs, block_e, n_used, run_slot, next_expert, w_exp_gate, w_exp_up, w_exp_down,
                     layer=l, blk=blk)
        x2 = ffn_out(hp, ys, dest, w_tok, lw['wsg'], lw['wsu'], lw['wsd'], x2, g2, seq=S)
    return x2.reshape(B, S, D)
```
